```python
import math
import jax, jax.numpy as jnp
from jax import lax
import numpy as np

D_MODEL = 1024
BATCH = 8
SEQ = 4096
DEPTH = 1

D_MIX = D_MODEL
DA_HEADS = 4
DA_HEAD_DIM = D_MODEL // 16
DA_V_DIM = 2 * DA_HEAD_DIM
DA_WIDTH = DA_HEADS * DA_V_DIM
SB_HEADS = 8
SB_HEAD_DIM = D_MODEL // 16
SB_WIDTH = SB_HEADS * SB_HEAD_DIM
N_BUCKETS = 32
MAX_DISTANCE = 128
Q_BLOCK = 128
EPS = 1e-6
NEG_INF = -1e30

COL_SIZES = (
    DA_HEADS * 2 * DA_HEAD_DIM,
    DA_HEADS * 2 * DA_HEAD_DIM,
    DA_WIDTH,
    DA_WIDTH,
    SB_WIDTH,
    SB_WIDTH,
    SB_WIDTH,
    SB_WIDTH,
)
D_IN = sum(COL_SIZES)
COL_SPLITS = tuple(int(v) for v in np.cumsum(COL_SIZES)[:-1])

kernel_name = "hybrid_diffattn_stickbreaking_adaln"


def lambda_init_fn(layer_idx):
    return 0.8 - 0.6 * math.exp(-0.3 * layer_idx)


def rmsnorm(x, g):
    xf = x.astype(jnp.float32)
    xf = xf * lax.rsqrt(jnp.mean(xf * xf, axis=-1, keepdims=True) + EPS)
    return (xf * g.astype(jnp.float32)).astype(x.dtype)


def rel_bucket(n):
    max_exact = N_BUCKETS // 2
    nf = jnp.maximum(n, 1).astype(jnp.float32)
    large = max_exact + (jnp.log(nf / max_exact) / math.log(MAX_DISTANCE / max_exact)
                         * (N_BUCKETS - max_exact)).astype(jnp.int32)
    large = jnp.minimum(large, N_BUCKETS - 1)
    return jnp.where(n < max_exact, n, large)


def differential_attention(q, k, v, rel_bias, lam, sub_g, lambda_init):
    B, S, H, _, d = q.shape
    nb = S // Q_BLOCK
    scale = 1.0 / math.sqrt(d)
    q_blocks = q.reshape(B, nb, Q_BLOCK, H, 2, d).transpose(1, 0, 3, 4, 2, 5)
    keys = k.transpose(0, 2, 3, 1, 4)
    vals = v.transpose(0, 2, 1, 3)
    key_pos = jnp.arange(S, dtype=jnp.int32)

    def block(args):
        q_blk, start = args
        q_pos = start + jnp.arange(Q_BLOCK, dtype=jnp.int32)
        rel = q_pos[:, None] - key_pos[None, :]
        causal = rel >= 0
        bias = rel_bias[rel_bucket(jnp.maximum(rel, 0))].astype(jnp.float32)
        bias = bias.transpose(2, 0, 1)
        logits = jnp.einsum('bhmqd,bhmkd->bhmqk', q_blk, keys).astype(jnp.float32) * scale
        logits = logits + bias[None, :, None]
        logits = jnp.where(causal, logits, NEG_INF)
        p = jax.nn.softmax(logits, axis=-1)
        attn = p[:, :, 0] - lam * p[:, :, 1]
        return jnp.einsum('bhqk,bhkv->bhqv', attn.astype(vals.dtype), vals)

    starts = jnp.arange(nb, dtype=jnp.int32) * Q_BLOCK
    out = lax.map(block, (q_blocks, starts))
    out = out.transpose(1, 0, 3, 2, 4).reshape(B, S, H, -1)
    out = rmsnorm(out, sub_g) * (1.0 - lambda_init)
    return out.reshape(B, S, -1)


def stick_breaking_attention(q, k, v):
    B, S, H, d = q.shape
    nb = S // Q_BLOCK
    scale = 1.0 / math.sqrt(d)
    q_blocks = q.reshape(B, nb, Q_BLOCK, H, d).transpose(1, 0, 3, 2, 4)
    keys = k.transpose(0, 2, 1, 3)
    vals = v.transpose(0, 2, 1, 3)
    key_pos = jnp.arange(S, dtype=jnp.int32)

    def block(args):
        q_blk, start = args
        q_pos = start + jnp.arange(Q_BLOCK, dtype=jnp.int32)
        strict = key_pos[None, :] < q_pos[:, None]
        z = jnp.einsum('bhqd,bhkd->bhqk', q_blk, keys).astype(jnp.float32) * scale
        log_beta = jax.nn.log_sigmoid(z)
        log_keep = jnp.where(strict, jax.nn.log_sigmoid(-z), 0.0)
        after = lax.cumsum(log_keep, axis=3, reverse=True) - log_keep
        w = jnp.where(strict, jnp.exp(log_beta + after), 0.0)
        return jnp.einsum('bhqk,bhkv->bhqv', w.astype(vals.dtype), vals)

    starts = jnp.arange(nb, dtype=jnp.int32) * Q_BLOCK
    out = lax.map(block, (q_blocks, starts))
    return out.transpose(1, 0, 3, 2, 4).reshape(B, S, -1)


def setup_inputs(seed: int = 0) -> dict:
    key = jax.random.key(seed)
    ks = jax.random.split(key, 15)
    f32 = jnp.float32
    x = jax.random.normal(ks[0], (BATCH, SEQ, D_MODEL), f32)
    c = jax.random.normal(ks[1], (BATCH, D_MODEL), f32)
    norm_g = 1.0 + 0.05 * jax.random.normal(ks[2], (DEPTH, D_MODEL), f32)
    w_ada = 0.5 * D_MODEL ** -0.5 * jax.random.normal(ks[3], (DEPTH, D_MODEL, 3 * D_MODEL), f32)
    b_ada = 0.01 * jax.random.normal(ks[4], (DEPTH, 3 * D_MODEL), f32)
    w_in = D_MODEL ** -0.5 * jax.random.normal(ks[5], (DEPTH, D_MODEL, D_IN), f32)
    lambda_q1 = 0.1 * jax.random.normal(ks[6], (DEPTH, DA_HEAD_DIM), f32)
    lambda_k1 = 0.1 * jax.random.normal(ks[7], (DEPTH, DA_HEAD_DIM), f32)
    lambda_q2 = 0.1 * jax.random.normal(ks[8], (DEPTH, DA_HEAD_DIM), f32)
    lambda_k2 = 0.1 * jax.random.normal(ks[9], (DEPTH, DA_HEAD_DIM), f32)
    subln_g = 1.0 + 0.05 * jax.random.normal(ks[10], (DEPTH, DA_V_DIM), f32)
    w_out = D_MIX ** -0.5 * jax.random.normal(ks[11], (DEPTH, D_MIX, D_MODEL), f32)
    rel_bias = 0.5 * jax.random.normal(ks[12], (N_BUCKETS, DA_HEADS), f32)
    final_g = 1.0 + 0.05 * jax.random.normal(ks[13], (D_MODEL,), f32)
    return {"x": x, "c": c, "norm_g": norm_g, "w_ada": w_ada, "b_ada": b_ada,
            "w_in": w_in, "lambda_q1": lambda_q1, "lambda_k1": lambda_k1,
            "lambda_q2": lambda_q2, "lambda_k2": lambda_k2, "subln_g": subln_g,
            "w_out": w_out, "rel_bias": rel_bias, "final_g": final_g}


def reference(x, c, norm_g, w_ada, b_ada, w_in, lambda_q1, lambda_k1, lambda_q2, lambda_k2,
              subln_g, w_out, rel_bias, final_g):
    B, S, D = x.shape
    for l in range(DEPTH):
        lambda_init = lambda_init_fn(l)
        mod = jax.nn.silu(c) @ w_ada[l] + b_ada[l]
        shift, scl, gate = jnp.split(mod, 3, axis=-1)
        h = rmsnorm(x, norm_g[l]) * (1.0 + scl[:, None, :]) + shift[:, None, :]
        proj = h @ w_in[l]
        q_da, k_da, v_da, g_da, q_sb, k_sb, v_sb, g_sb = jnp.split(proj, COL_SPLITS, axis=-1)
        lam = (jnp.exp(jnp.sum(lambda_q1[l] * lambda_k1[l]).astype(jnp.float32))
               - jnp.exp(jnp.sum(lambda_q2[l] * lambda_k2[l]).astype(jnp.float32))
               + lambda_init)
        o_da = differential_attention(
            q_da.reshape(B, S, DA_HEADS, 2, DA_HEAD_DIM),
            k_da.reshape(B, S, DA_HEADS, 2, DA_HEAD_DIM),
            v_da.reshape(B, S, DA_HEADS, DA_V_DIM),
            rel_bias, lam, subln_g[l], lambda_init)
        o_sb = stick_breaking_attention(
            q_sb.reshape(B, S, SB_HEADS, SB_HEAD_DIM),
            k_sb.reshape(B, S, SB_HEADS, SB_HEAD_DIM),
            v_sb.reshape(B, S, SB_HEADS, SB_HEAD_DIM))
        mixed = jnp.concatenate([o_da * jax.nn.silu(g_da), o_sb * jax.nn.silu(g_sb)], axis=-1)
        y = mixed @ w_out[l]
        x = x + gate[:, None, :] * y
    return rmsnorm(x, final_g)
```

```python
import functools
import math

import numpy as np
import jax
import jax.numpy as jnp
from jax import lax
from jax.experimental import pallas as pl
from jax.experimental.pallas import tpu as pltpu

DA_HEADS = 4
DA_HEAD_DIM = 64
DA_V_DIM = 2 * DA_HEAD_DIM
DA_WIDTH = DA_HEADS * DA_V_DIM
SB_HEADS = 8
SB_HEAD_DIM = 64
SB_WIDTH = SB_HEADS * SB_HEAD_DIM
N_BUCKETS = 32
MAX_DISTANCE = 128
EPS = 1e-6
NEG_INF = -1e30

SUBLANES = 8
LANES = 128
TOK_BLOCK = 256
CHUNK = TOK_BLOCK // SUBLANES
ROW_TILE = 512
VMEM_LIMIT = 48 * 1024 * 1024

F32 = jnp.float32
BF16 = jnp.bfloat16


def _lambda_init(layer_idx):
    return 0.8 - 0.6 * math.exp(-0.3 * layer_idx)


def _block_token(pos):
    return (pos % SUBLANES) * CHUNK + pos // SUBLANES


def _bucket_tiles():
    pos = np.arange(TOK_BLOCK)
    tok = _block_token(pos)
    rel0 = tok[None, :] - tok[:, None]
    rel = np.stack([np.maximum(rel0, 0), rel0 + TOK_BLOCK]).astype(np.int32)
    max_exact = N_BUCKETS // 2
    nf = np.maximum(rel, 1).astype(np.float32)
    large = max_exact + (np.log(nf / np.float32(max_exact)) / np.float32(math.log(MAX_DISTANCE / max_exact))
                         * np.float32(N_BUCKETS - max_exact)).astype(np.int32)
    large = np.minimum(large, N_BUCKETS - 1)
    return np.where(rel < max_exact, rel, large).astype(np.int32)


def _cparams(n_axes):
    return pltpu.CompilerParams(dimension_semantics=("arbitrary",) * n_axes, vmem_limit_bytes=VMEM_LIMIT)


def _mod_kernel(c_ref, w_ref, b_ref, lq1_ref, lk1_ref, lq2_ref, lk2_ref, mod_ref, lam_ref, *, lambda_init):
    c = c_ref[...]
    silu_c = c / (1.0 + jnp.exp(-c))
    mod_ref[...] = jnp.dot(silu_c, w_ref[...], preferred_element_type=F32,
                           precision=lax.Precision.HIGHEST) + b_ref[...]
    s1 = jnp.sum(lq1_ref[...] * lk1_ref[...], axis=-1, keepdims=True)
    s2 = jnp.sum(lq2_ref[...] * lk2_ref[...], axis=-1, keepdims=True)
    lam_ref[...] = jnp.broadcast_to(jnp.exp(s1) - jnp.exp(s2) + lambda_init, lam_ref.shape)


def _modulation(c, w_ada, b_ada, lq1, lk1, lq2, lk2, lambda_init):
    B, D = c.shape
    n_out = w_ada.shape[1]
    col = 512
    vec = lambda: pl.BlockSpec((1, DA_HEAD_DIM), lambda j: (0, 0))
    return pl.pallas_call(
        functools.partial(_mod_kernel, lambda_init=lambda_init),
        grid=(n_out // col,),
        in_specs=[pl.BlockSpec((B, D), lambda j: (0, 0)),
                  pl.BlockSpec((D, col), lambda j: (0, j)),
                  pl.BlockSpec((1, col), lambda j: (0, j)),
                  vec(), vec(), vec(), vec()],
        out_specs=[pl.BlockSpec((B, col), lambda j: (0, j)),
                   pl.BlockSpec((1, LANES), lambda j: (0, 0))],
        out_shape=[jax.ShapeDtypeStruct((B, n_out), F32), jax.ShapeDtypeStruct((1, LANES), F32)],
        compiler_params=_cparams(1),
        name="adaln_mod",
    )(c, w_ada, b_ada.reshape(1, n_out), lq1.reshape(1, -1), lk1.reshape(1, -1), lq2.reshape(1, -1),
      lk2.reshape(1, -1))


def _bias_kernel(rb_ref, bucket_ref, out_ref):
    h = pl.program_id(0)
    far = rb_ref[N_BUCKETS - 1, h]
    for t in range(2):
        bucket = bucket_ref[t]
        tile = jnp.zeros(bucket.shape, F32)
        for b in range(N_BUCKETS - 1):
            tile = jnp.where(bucket == b, rb_ref[b, h] - far, tile)
        out_ref[0, t] = tile


def _bias_tiles(rel_bias):
    buckets = jnp.asarray(_bucket_tiles())
    return pl.pallas_call(
        _bias_kernel,
        grid=(DA_HEADS,),
        in_specs=[pl.BlockSpec(memory_space=pltpu.SMEM),
                  pl.BlockSpec((2, TOK_BLOCK, TOK_BLOCK), lambda h: (0, 0, 0))],
        out_specs=pl.BlockSpec((1, 2, TOK_BLOCK, TOK_BLOCK), lambda h: (h, 0, 0, 0)),
        out_shape=jax.ShapeDtypeStruct((DA_HEADS, 2, TOK_BLOCK, TOK_BLOCK), F32),
        compiler_params=_cparams(1),
        name="rel_bias_tiles",
    )(rel_bias, buckets)


def _proj_kernel(x_ref, g_ref, scl_ref, shift_ref, wt_ref, wf_ref,
                 kda_ref, ksb_ref, gda_ref, gsb_ref, qda_ref, vda_ref, qsb_ref, vsb_ref):
    x = x_ref[0]
    ms = jnp.mean(x * x, axis=-1, keepdims=True)
    xn = x * lax.rsqrt(ms + EPS) * g_ref[...]
    h = (xn * (1.0 + scl_ref[0]) + shift_ref[0]).astype(BF16)
    width = kda_ref.shape[-1]
    for i, o_ref in enumerate((kda_ref, ksb_ref, gda_ref, gsb_ref)):
        o_ref[0] = jnp.dot(h, wt_ref[:, i * width:(i + 1) * width], preferred_element_type=F32).astype(BF16)
    for i, o_ref in enumerate((qda_ref, vda_ref, qsb_ref, vsb_ref)):
        r = lax.dot_general(wf_ref[i * width:(i + 1) * width, :], h, (((1,), (1,)), ((), ())),
                            preferred_element_type=F32).astype(BF16)
        for t in range(o_ref.shape[1]):
            o_ref[0, t] = r[:, t * TOK_BLOCK:(t + 1) * TOK_BLOCK]


def _project(x, norm_g, scl, shift, w_tok, w_feat_t):
    B, S, D = x.shape
    n_blk = S // TOK_BLOCK
    blk_per_tile = ROW_TILE // TOK_BLOCK
    width = DA_WIDTH
    tok_spec = pl.BlockSpec((1, ROW_TILE, width), lambda b, i: (b, i, 0))
    feat_spec = pl.BlockSpec((1, blk_per_tile, width, TOK_BLOCK), lambda b, i: (b, i, 0, 0))
    tok_shape = jax.ShapeDtypeStruct((B, S, width), BF16)
    feat_shape = jax.ShapeDtypeStruct((B, n_blk, width, TOK_BLOCK), BF16)
    mod_spec = pl.BlockSpec((1, 1, D), lambda b, i: (b, 0, 0))
    return pl.pallas_call(
        _proj_kernel,
        grid=(B, S // ROW_TILE),
        in_specs=[pl.BlockSpec((1, ROW_TILE, D), lambda b, i: (b, i, 0)),
                  pl.BlockSpec((1, D), lambda b, i: (0, 0)),
                  mod_spec, mod_spec,
                  pl.BlockSpec(w_tok.shape, lambda b, i: (0, 0)),
                  pl.BlockSpec(w_feat_t.shape, lambda b, i: (0, 0))],
        out_specs=[tok_spec] * 4 + [feat_spec] * 4,
        out_shape=[tok_shape] * 4 + [feat_shape] * 4,
        compiler_params=_cparams(2),
        name="norm_in_proj",
    )(x, norm_g.reshape(1, D), scl, shift, w_tok, w_feat_t)


def _tile_tokens():
    row = lax.broadcasted_iota(jnp.int32, (TOK_BLOCK, TOK_BLOCK), 0)
    col = lax.broadcasted_iota(jnp.int32, (TOK_BLOCK, TOK_BLOCK), 1)
    return _block_token(row), _block_token(col)


def _split_rows(q_t, half):
    row = lax.broadcasted_iota(jnp.int32, q_t.shape, 0)
    zero = jnp.zeros_like(q_t)
    return jnp.where(row < half, q_t, zero), jnp.where(row >= half, q_t, zero)


def _da_kernel(lam_ref, q_ref, k_ref, v_ref, bias_ref, g_ref, o_ref, acc_ref, m_ref, l_ref, *, out_scale):
    qi = pl.program_id(2)
    qz = _split_rows(q_ref[0, 0], DA_HEAD_DIM)
    acc_ref[...] = jnp.zeros(acc_ref.shape, F32)
    m_ref[...] = jnp.full(m_ref.shape, NEG_INF, F32)
    l_ref[...] = jnp.zeros(l_ref.shape, F32)

    def tile(kj, bias=None, valid=None):
        kb = k_ref[0, pl.ds(pl.multiple_of(kj * TOK_BLOCK, TOK_BLOCK), TOK_BLOCK), :]
        vb = v_ref[0, kj]
        for mp in range(2):
            s = jnp.dot(kb, qz[mp], preferred_element_type=F32)
            if bias is not None:
                s = s + bias
            if valid is not None:
                s = jnp.where(valid, s, NEG_INF)
            m_old = m_ref[mp]
            m_new = jnp.maximum(m_old, jnp.max(s, axis=0, keepdims=True))
            alpha = jnp.exp(m_old - m_new)
            p = jnp.exp(s - m_new)
            l_ref[mp] = alpha * l_ref[mp] + jnp.sum(p, axis=0, keepdims=True)
            m_ref[mp] = m_new
            acc_ref[mp] = alpha * acc_ref[mp] + jnp.dot(vb, p.astype(BF16), preferred_element_type=F32)

    def far_tile(kj, carry):
        tile(kj)
        return carry

    lax.fori_loop(0, jnp.maximum(qi - 1, 0), far_tile, 0)

    @pl.when(qi >= 1)
    def _():
        tile(qi - 1, bias=bias_ref[0, 1])

    tok_k, tok_q = _tile_tokens()
    tile(qi, bias=bias_ref[0, 0], valid=tok_k <= tok_q)

    lam = lam_ref[0, 0]
    o_t = acc_ref[0] / l_ref[0] - lam * (acc_ref[1] / l_ref[1])
    o = o_t.T
    o = o * lax.rsqrt(jnp.mean(o * o, axis=-1, keepdims=True) + EPS) * g_ref[...]
    o_ref[0] = (o * out_scale).astype(o_ref.dtype)


def _diff_attention(q_t, k, v_t, bias, lam, subln_g, lambda_init):
    B, S, _ = k.shape
    n_blk = S // TOK_BLOCK
    return pl.pallas_call(
        functools.partial(_da_kernel, out_scale=1.0 - lambda_init),
        grid=(B, DA_HEADS, n_blk),
        in_specs=[pl.BlockSpec(memory_space=pltpu.SMEM),
                  pl.BlockSpec((1, 1, DA_V_DIM, TOK_BLOCK), lambda b, h, i: (b, i, h, 0)),
                  pl.BlockSpec((1, S, 2 * DA_HEAD_DIM), lambda b, h, i: (b, 0, h)),
                  pl.BlockSpec((1, n_blk, DA_V_DIM, TOK_BLOCK), lambda b, h, i: (b, 0, h, 0)),
                  pl.BlockSpec((1, 2, TOK_BLOCK, TOK_BLOCK), lambda b, h, i: (h, 0, 0, 0)),
                  pl.BlockSpec((1, DA_V_DIM), lambda b, h, i: (0, 0))],
        out_specs=pl.BlockSpec((1, TOK_BLOCK, DA_V_DIM), lambda b, h, i: (b, i, h)),
        out_shape=jax.ShapeDtypeStruct((B, S, DA_WIDTH), BF16),
        scratch_shapes=[pltpu.VMEM((2, DA_V_DIM, TOK_BLOCK), F32),
                        pltpu.VMEM((2, 1, TOK_BLOCK), F32),
                        pltpu.VMEM((2, 1, TOK_BLOCK), F32)],
        compiler_params=_cparams(3),
        name="diff_attention",
    )(lam, q_t, k, v_t, bias, subln_g.reshape(1, DA_V_DIM))


def _sb_kernel(q_ref, k_ref, v_ref, o_ref, acc_ref, carry_ref):
    qi = pl.program_id(2)
    qz = _split_rows(q_ref[0, 0], SB_HEAD_DIM)
    acc_ref[...] = jnp.zeros(acc_ref.shape, F32)
    carry_ref[...] = jnp.ones(carry_ref.shape, F32)
    sub = lax.broadcasted_iota(jnp.int32, (SUBLANES, TOK_BLOCK), 0)

    def tile(kj, valid=None):
        kb = k_ref[0, pl.ds(pl.multiple_of(kj * TOK_BLOCK, TOK_BLOCK), TOK_BLOCK), :]
        vb = v_ref[0, kj]
        for hh in range(2):
            z = jnp.dot(kb, qz[hh], preferred_element_type=F32)
            ez = jnp.exp(-jnp.abs(z))
            r = 1.0 / (1.0 + ez)
            er = ez * r
            pos = z >= 0.0
            beta = jnp.where(pos, r, er)
            keep = jnp.where(pos, er, r)
            if valid is not None:
                beta = jnp.where(valid, beta, 0.0)
                keep = jnp.where(valid, keep, 1.0)
            run = jnp.ones((SUBLANES, TOK_BLOCK), F32)
            later = [None] * CHUNK
            for g in reversed(range(CHUNK)):
                later[g] = run
                run = run * keep[g * SUBLANES:(g + 1) * SUBLANES]
            suffix = run
            for d in (1, 2, 4):
                shifted = pltpu.roll(suffix, SUBLANES - d, axis=0)
                suffix = jnp.where(sub + d < SUBLANES, suffix * shifted, suffix)
            above = jnp.where(sub + 1 < SUBLANES, pltpu.roll(suffix, SUBLANES - 1, axis=0), 1.0)
            carry = carry_ref[hh]
            scale = above * carry
            w = jnp.concatenate(
                [beta[g * SUBLANES:(g + 1) * SUBLANES] * (later[g] * scale) for g in range(CHUNK)], axis=0)
            carry_ref[hh] = carry * jnp.broadcast_to(suffix[0:1], carry.shape)
            vh = vb[hh * SB_HEAD_DIM:(hh + 1) * SB_HEAD_DIM]
            acc_ref[hh] = acc_ref[hh] + jnp.dot(vh, w.astype(BF16), preferred_element_type=F32)

    tok_k, tok_q = _tile_tokens()
    tile(qi, valid=tok_k < tok_q)

    def left_tile(i, carry):
        tile(qi - 1 - i)
        return carry

    lax.fori_loop(0, qi, left_tile, 0)

    o_t = jnp.concatenate([acc_ref[0], acc_ref[1]], axis=0)
    o_ref[0] = o_t.T.astype(o_ref.dtype)


def _sb_attention(q_t, k, v_t):
    B, S, _ = k.shape
    n_blk = S // TOK_BLOCK
    pair = 2 * SB_HEAD_DIM
    return pl.pallas_call(
        _sb_kernel,
        grid=(B, SB_HEADS // 2, n_blk),
        in_specs=[pl.BlockSpec((1, 1, pair, TOK_BLOCK), lambda b, h, i: (b, i, h, 0)),
                  pl.BlockSpec((1, S, pair), lambda b, h, i: (b, 0, h)),
                  pl.BlockSpec((1, n_blk, pair, TOK_BLOCK), lambda b, h, i: (b, 0, h, 0))],
        out_specs=pl.BlockSpec((1, TOK_BLOCK, pair), lambda b, h, i: (b, i, h)),
        out_shape=jax.ShapeDtypeStruct((B, S, SB_WIDTH), BF16),
        scratch_shapes=[pltpu.VMEM((2, SB_HEAD_DIM, TOK_BLOCK), F32),
                        pltpu.VMEM((2, SUBLANES, TOK_BLOCK), F32)],
        compiler_params=_cparams(3),
        name="stick_breaking_attention",
    )(q_t, k, v_t)


def _out_kernel(oda_ref, osb_ref, gda_ref, gsb_ref, x_ref, gate_ref, w_ref, fg_ref, out_ref, *, final_norm):
    def gated(o_ref, g_ref):
        g = g_ref[0].astype(F32)
        return (o_ref[0].astype(F32) * (g / (1.0 + jnp.exp(-g)))).astype(BF16)

    split = oda_ref.shape[-1]
    y = jnp.dot(gated(oda_ref, gda_ref), w_ref[0:split, :], preferred_element_type=F32)
    y = y + jnp.dot(gated(osb_ref, gsb_ref), w_ref[split:, :], preferred_element_type=F32)
    xo = x_ref[0] + gate_ref[0] * y
    if final_norm:
        xo = xo * lax.rsqrt(jnp.mean(xo * xo, axis=-1, keepdims=True) + EPS) * fg_ref[...]
    out_ref[0] = xo


def _out_project(o_da, o_sb, g_da, g_sb, x, gate, w_out, final_g, final_norm):
    B, S, D = x.shape
    half_spec = pl.BlockSpec((1, ROW_TILE, DA_WIDTH), lambda b, i: (b, i, 0))
    row_spec = pl.BlockSpec((1, ROW_TILE, D), lambda b, i: (b, i, 0))
    return pl.pallas_call(
        functools.partial(_out_kernel, final_norm=final_norm),
        grid=(B, S // ROW_TILE),
        in_specs=[half_spec] * 4 + [row_spec,
                                    pl.BlockSpec((1, 1, D), lambda b, i: (b, 0, 0)),
                                    pl.BlockSpec(w_out.shape, lambda b, i: (0, 0)),
                                    pl.BlockSpec((1, D), lambda b, i: (0, 0))],
        out_specs=row_spec,
        out_shape=jax.ShapeDtypeStruct((B, S, D), F32),
        compiler_params=_cparams(2),
        name="gate_out_proj",
    )(o_da, o_sb, g_da, g_sb, x, gate, w_out, final_g.reshape(1, D))


def _split_w_in(w):
    sizes = (DA_WIDTH, DA_WIDTH, DA_WIDTH, DA_WIDTH, SB_WIDTH, SB_WIDTH, SB_WIDTH, SB_WIDTH)
    q_da, k_da, v_da, g_da, q_sb, k_sb, v_sb, g_sb = jnp.split(w, np.cumsum(sizes)[:-1].tolist(), axis=1)
    w_tok = jnp.concatenate([k_da, k_sb, g_da, g_sb], axis=1).astype(BF16)
    q_scale = 1.0 / math.sqrt(DA_HEAD_DIM)
    w_feat_t = jnp.concatenate([q_da * q_scale, v_da, q_sb * q_scale, v_sb], axis=1).T.astype(BF16)
    return w_tok, w_feat_t


def kernel(x, c, norm_g, w_ada, b_ada, w_in, lambda_q1, lambda_k1, lambda_q2, lambda_k2, subln_g, w_out,
           rel_bias, final_g):
    B, S, D = x.shape
    depth = w_in.shape[0]
    assert S % ROW_TILE == 0 and ROW_TILE % TOK_BLOCK == 0
    assert DA_HEAD_DIM == SB_HEAD_DIM and DA_WIDTH == SB_WIDTH and DA_WIDTH + SB_WIDTH == w_out.shape[1]
    n_blk = S // TOK_BLOCK
    xp = x.reshape(B, n_blk, SUBLANES, CHUNK, D).transpose(0, 1, 3, 2, 4).reshape(B, S, D)
    bias = _bias_tiles(rel_bias)
    for l in range(depth):
        lambda_init = _lambda_init(l)
        mod, lam = _modulation(c, w_ada[l], b_ada[l], lambda_q1[l], lambda_k1[l], lambda_q2[l], lambda_k2[l],
                               lambda_init)
        shift, scl, gate = (m.reshape(B, 1, D) for m in jnp.split(mod, 3, axis=-1))
        w_tok, w_feat_t = _split_w_in(w_in[l])
        k_da, k_sb, g_da, g_sb, q_da, v_da, q_sb, v_sb = _project(xp, norm_g[l], scl, shift, w_tok, w_feat_t)
        o_da = _diff_attention(q_da, k_da, v_da, bias, lam[:, :1], subln_g[l], lambda_init)
        o_sb = _sb_attention(q_sb, k_sb, v_sb)
        xp = _out_project(o_da, o_sb, g_da, g_sb, xp, gate, w_out[l].astype(BF16), final_g, l == depth - 1)
    return xp.reshape(B, n_blk, CHUNK, SUBLANES, D).transpose(0, 1, 3, 2, 4).reshape(B, S, D)
```

```python
import functools
import math

import numpy as np
import jax
import jax.numpy as jnp
from jax import lax
from jax.experimental import pallas as pl
from jax.experimental.pallas import tpu as pltpu

DA_HEADS = 4
DA_HEAD_DIM = 64
DA_V_DIM = 2 * DA_HEAD_DIM
DA_WIDTH = DA_HEADS * DA_V_DIM
SB_HEADS = 8
SB_HEAD_DIM = 64
SB_WIDTH = SB_HEADS * SB_HEAD_DIM
N_BUCKETS = 32
MAX_DISTANCE = 128
EPS = 1e-6
NEG_INF = -1e30

SUBLANES = 8
LANES = 128
TOK_BLOCK = 256
CHUNK = TOK_BLOCK // SUBLANES
ROW_TILE = 512
VMEM_LIMIT = 48 * 1024 * 1024

F32 = jnp.float32
BF16 = jnp.bfloat16


def _lambda_init(layer_idx):
    return 0.8 - 0.6 * math.exp(-0.3 * layer_idx)


def _block_token(pos):
    return (pos % SUBLANES) * CHUNK + pos // SUBLANES


def _bucket_tiles():
    pos = np.arange(TOK_BLOCK)
    tok = _block_token(pos)
    rel0 = tok[None, :] - tok[:, None]
    rel = np.stack([np.maximum(rel0, 0), rel0 + TOK_BLOCK]).astype(np.int32)
    max_exact = N_BUCKETS // 2
    nf = np.maximum(rel, 1).astype(np.float32)
    large = max_exact + (np.log(nf / np.float32(max_exact)) / np.float32(math.log(MAX_DISTANCE / max_exact))
                         * np.float32(N_BUCKETS - max_exact)).astype(np.int32)
    large = np.minimum(large, N_BUCKETS - 1)
    return np.where(rel < max_exact, rel, large).astype(np.int32)


def _cparams(n_axes):
    return pltpu.CompilerParams(dimension_semantics=("arbitrary",) * n_axes, vmem_limit_bytes=VMEM_LIMIT)


def _mod_kernel(c_ref, w_ref, b_ref, lq1_ref, lk1_ref, lq2_ref, lk2_ref, mod_ref, lam_ref, *, lambda_init):
    c = c_ref[...]
    silu_c = c / (1.0 + jnp.exp(-c))
    mod_ref[...] = jnp.dot(silu_c, w_ref[...], preferred_element_type=F32,
                           precision=lax.Precision.HIGHEST) + b_ref[...]
    s1 = jnp.sum(lq1_ref[...] * lk1_ref[...], axis=-1, keepdims=True)
    s2 = jnp.sum(lq2_ref[...] * lk2_ref[...], axis=-1, keepdims=True)
    lam_ref[...] = jnp.broadcast_to(jnp.exp(s1) - jnp.exp(s2) + lambda_init, lam_ref.shape)


def _modulation(c, w_ada, b_ada, lq1, lk1, lq2, lk2, lambda_init):
    B, D = c.shape
    n_out = w_ada.shape[1]
    col = 512
    vec = lambda: pl.BlockSpec((1, DA_HEAD_DIM), lambda j: (0, 0))
    return pl.pallas_call(
        functools.partial(_mod_kernel, lambda_init=lambda_init),
        grid=(n_out // col,),
        in_specs=[pl.BlockSpec((B, D), lambda j: (0, 0)),
                  pl.BlockSpec((D, col), lambda j: (0, j)),
                  pl.BlockSpec((1, col), lambda j: (0, j)),
                  vec(), vec(), vec(), vec()],
        out_specs=[pl.BlockSpec((B, col), lambda j: (0, j)),
                   pl.BlockSpec((1, LANES), lambda j: (0, 0))],
        out_shape=[jax.ShapeDtypeStruct((B, n_out), F32), jax.ShapeDtypeStruct((1, LANES), F32)],
        compiler_params=_cparams(1),
        name="adaln_mod",
    )(c, w_ada, b_ada.reshape(1, n_out), lq1.reshape(1, -1), lk1.reshape(1, -1), lq2.reshape(1, -1),
      lk2.reshape(1, -1))


def _bias_kernel(rb_ref, bucket_ref, out_ref):
    h = pl.program_id(0)
    far = rb_ref[N_BUCKETS - 1, h]
    for t in range(2):
        bucket = bucket_ref[t]
        tile = jnp.zeros(bucket.shape, F32)
        for b in range(N_BUCKETS - 1):
            tile = jnp.where(bucket == b, rb_ref[b, h] - far, tile)
        out_ref[0, t] = tile


def _bias_tiles(rel_bias):
    buckets = jnp.asarray(_bucket_tiles())
    return pl.pallas_call(
        _bias_kernel,
        grid=(DA_HEADS,),
        in_specs=[pl.BlockSpec(memory_space=pltpu.SMEM),
                  pl.BlockSpec((2, TOK_BLOCK, TOK_BLOCK), lambda h: (0, 0, 0))],
        out_specs=pl.BlockSpec((1, 2, TOK_BLOCK, TOK_BLOCK), lambda h: (h, 0, 0, 0)),
        out_shape=jax.ShapeDtypeStruct((DA_HEADS, 2, TOK_BLOCK, TOK_BLOCK), F32),
        compiler_params=_cparams(1),
        name="rel_bias_tiles",
    )(rel_bias, buckets)


def _proj_kernel(x_ref, g_ref, scl_ref, shift_ref, wt_ref, wf_ref,
                 kda_ref, ksb_ref, gda_ref, gsb_ref, qda_ref, vda_ref, qsb_ref, vsb_ref):
    x = x_ref[0]
    ms = jnp.mean(x * x, axis=-1, keepdims=True)
    xn = x * lax.rsqrt(ms + EPS) * g_ref[...]
    h = (xn * (1.0 + scl_ref[0]) + shift_ref[0]).astype(BF16)
    width = kda_ref.shape[-1]
    for i, o_ref in enumerate((kda_ref, ksb_ref, gda_ref, gsb_ref)):
        o_ref[0] = jnp.dot(h, wt_ref[:, i * width:(i + 1) * width], preferred_element_type=F32).astype(BF16)
    for i, o_ref in enumerate((qda_ref, vda_ref, qsb_ref, vsb_ref)):
        r = lax.dot_general(wf_ref[i * width:(i + 1) * width, :], h, (((1,), (1,)), ((), ())),
                            preferred_element_type=F32).astype(BF16)
        for t in range(o_ref.shape[1]):
            o_ref[0, t] = r[:, t * TOK_BLOCK:(t + 1) * TOK_BLOCK]


def _project(x, norm_g, scl, shift, w_tok, w_feat_t):
    B, S, D = x.shape
    n_blk = S // TOK_BLOCK
    blk_per_tile = ROW_TILE // TOK_BLOCK
    width = DA_WIDTH
    tok_spec = pl.BlockSpec((1, ROW_TILE, width), lambda b, i: (b, i, 0))
    feat_spec = pl.BlockSpec((1, blk_per_tile, width, TOK_BLOCK), lambda b, i: (b, i, 0, 0))
    tok_shape = jax.ShapeDtypeStruct((B, S, width), BF16)
    feat_shape = jax.ShapeDtypeStruct((B, n_blk, width, TOK_BLOCK), BF16)
    mod_spec = pl.BlockSpec((1, 1, D), lambda b, i: (b, 0, 0))
    return pl.pallas_call(
        _proj_kernel,
        grid=(B, S // ROW_TILE),
        in_specs=[pl.BlockSpec((1, ROW_TILE, D), lambda b, i: (b, i, 0)),
                  pl.BlockSpec((1, D), lambda b, i: (0, 0)),
                  mod_spec, mod_spec,
                  pl.BlockSpec(w_tok.shape, lambda b, i: (0, 0)),
                  pl.BlockSpec(w_feat_t.shape, lambda b, i: (0, 0))],
        out_specs=[tok_spec] * 4 + [feat_spec] * 4,
        out_shape=[tok_shape] * 4 + [feat_shape] * 4,
        compiler_params=_cparams(2),
        name="norm_in_proj",
    )(x, norm_g.reshape(1, D), scl, shift, w_tok, w_feat_t)


def _tile_tokens():
    row = lax.broadcasted_iota(jnp.int32, (TOK_BLOCK, TOK_BLOCK), 0)
    col = lax.broadcasted_iota(jnp.int32, (TOK_BLOCK, TOK_BLOCK), 1)
    return _block_token(row), _block_token(col)


def _split_rows(q_t, half):
    row = lax.broadcasted_iota(jnp.int32, q_t.shape, 0)
    zero = jnp.zeros_like(q_t)
    return jnp.where(row < half, q_t, zero), jnp.where(row >= half, q_t, zero)


def _key_block(k_ref, kj):
    return k_ref[0, pl.ds(pl.multiple_of(kj * TOK_BLOCK, TOK_BLOCK), TOK_BLOCK), :]


def _da_kernel(lam_ref, q_ref, k_ref, v_ref, bias_ref, g_ref, o_ref,
               acc_ref, m_ref, l_ref, s_ref, p_ref, a_ref, *, out_scale):
    qi = pl.program_id(2)
    qz = _split_rows(q_ref[0, 0], DA_HEAD_DIM)
    acc_ref[...] = jnp.zeros(acc_ref.shape, F32)
    m_ref[...] = jnp.full(m_ref.shape, NEG_INF, F32)
    l_ref[...] = jnp.zeros(l_ref.shape, F32)
    p_ref[...] = jnp.zeros(p_ref.shape, BF16)
    a_ref[...] = jnp.ones(a_ref.shape, F32)

    def scores(kj):
        kb = _key_block(k_ref, kj)
        return [jnp.dot(kb, qz[mp], preferred_element_type=F32) for mp in range(2)]

    def put_scores(s):
        for mp in range(2):
            s_ref[mp] = s[mp]

    def accumulate(kj):
        vb = v_ref[0, kj]
        for mp in range(2):
            acc_ref[mp] = a_ref[mp] * acc_ref[mp] + jnp.dot(vb, p_ref[mp], preferred_element_type=F32)

    def softmax_step(bias=None, valid=None):
        for mp in range(2):
            s = s_ref[mp]
            if bias is not None:
                s = s + bias
            if valid is not None:
                s = jnp.where(valid, s, NEG_INF)
            m_old = m_ref[mp]
            m_new = jnp.maximum(m_old, jnp.max(s, axis=0, keepdims=True))
            alpha = jnp.exp(m_old - m_new)
            p = jnp.exp(s - m_new)
            l_ref[mp] = alpha * l_ref[mp] + jnp.sum(p, axis=0, keepdims=True)
            m_ref[mp] = m_new
            a_ref[mp] = alpha
            p_ref[mp] = p.astype(BF16)

    put_scores(scores(0))

    def far_body(t, carry):
        nxt = scores(t + 1)
        accumulate(jnp.maximum(t - 1, 0))
        softmax_step()
        put_scores(nxt)
        return carry

    lax.fori_loop(0, jnp.maximum(qi - 1, 0), far_body, 0)

    @pl.when(qi >= 1)
    def _():
        nxt = scores(qi)
        accumulate(jnp.maximum(qi - 2, 0))
        softmax_step(bias=bias_ref[0, 1])
        put_scores(nxt)

    tok_k, tok_q = _tile_tokens()
    accumulate(jnp.maximum(qi - 1, 0))
    softmax_step(bias=bias_ref[0, 0], valid=tok_k <= tok_q)
    accumulate(qi)

    lam = lam_ref[0, 0]
    o_t = acc_ref[0] / l_ref[0] - lam * (acc_ref[1] / l_ref[1])
    o = o_t.T
    o = o * lax.rsqrt(jnp.mean(o * o, axis=-1, keepdims=True) + EPS) * g_ref[...]
    o_ref[0] = (o * out_scale).astype(o_ref.dtype)


def _diff_attention(q_t, k, v_t, bias, lam, subln_g, lambda_init):
    B, S, _ = k.shape
    n_blk = S // TOK_BLOCK
    return pl.pallas_call(
        functools.partial(_da_kernel, out_scale=1.0 - lambda_init),
        grid=(B, DA_HEADS, n_blk),
        in_specs=[pl.BlockSpec(memory_space=pltpu.SMEM),
                  pl.BlockSpec((1, 1, DA_V_DIM, TOK_BLOCK), lambda b, h, i: (b, i, h, 0)),
                  pl.BlockSpec((1, S, 2 * DA_HEAD_DIM), lambda b, h, i: (b, 0, h)),
                  pl.BlockSpec((1, n_blk, DA_V_DIM, TOK_BLOCK), lambda b, h, i: (b, 0, h, 0)),
                  pl.BlockSpec((1, 2, TOK_BLOCK, TOK_BLOCK), lambda b, h, i: (h, 0, 0, 0)),
                  pl.BlockSpec((1, DA_V_DIM), lambda b, h, i: (0, 0))],
        out_specs=pl.BlockSpec((1, TOK_BLOCK, DA_V_DIM), lambda b, h, i: (b, i, h)),
        out_shape=jax.ShapeDtypeStruct((B, S, DA_WIDTH), BF16),
        scratch_shapes=[pltpu.VMEM((2, DA_V_DIM, TOK_BLOCK), F32),
                        pltpu.VMEM((2, 1, TOK_BLOCK), F32),
                        pltpu.VMEM((2, 1, TOK_BLOCK), F32),
                        pltpu.VMEM((2, TOK_BLOCK, TOK_BLOCK), F32),
                        pltpu.VMEM((2, TOK_BLOCK, TOK_BLOCK), BF16),
                        pltpu.VMEM((2, 1, TOK_BLOCK), F32)],
        compiler_params=_cparams(3),
        name="diff_attention",
    )(lam, q_t, k, v_t, bias, subln_g.reshape(1, DA_V_DIM))


def _sb_kernel(q_ref, k_ref, v_ref, o_ref, acc_ref, carry_ref, z_ref, w_ref):
    qi = pl.program_id(2)
    qz = _split_rows(q_ref[0, 0], SB_HEAD_DIM)
    acc_ref[...] = jnp.zeros(acc_ref.shape, F32)
    carry_ref[...] = jnp.ones(carry_ref.shape, F32)
    sub = lax.broadcasted_iota(jnp.int32, (SUBLANES, TOK_BLOCK), 0)

    def logits(kj):
        kb = _key_block(k_ref, kj)
        return [jnp.dot(kb, qz[hh], preferred_element_type=F32) for hh in range(2)]

    def put_logits(z):
        for hh in range(2):
            z_ref[hh] = z[hh]

    def accumulate(kj):
        vb = v_ref[0, kj]
        for hh in range(2):
            vh = vb[hh * SB_HEAD_DIM:(hh + 1) * SB_HEAD_DIM]
            acc_ref[hh] = acc_ref[hh] + jnp.dot(vh, w_ref[hh], preferred_element_type=F32)

    def weights(valid=None):
        for hh in range(2):
            z = z_ref[hh]
            ez = jnp.exp(-jnp.abs(z))
            r = 1.0 / (1.0 + ez)
            er = ez * r
            pos = z >= 0.0
            beta = jnp.where(pos, r, er)
            keep = jnp.where(pos, er, r)
            if valid is not None:
                beta = jnp.where(valid, beta, 0.0)
                keep = jnp.where(valid, keep, 1.0)
            run = jnp.ones((SUBLANES, TOK_BLOCK), F32)
            later = [None] * CHUNK
            for g in reversed(range(CHUNK)):
                later[g] = run
                run = run * keep[g * SUBLANES:(g + 1) * SUBLANES]
            suffix = run
            for d in (1, 2, 4):
                shifted = pltpu.roll(suffix, SUBLANES - d, axis=0)
                suffix = jnp.where(sub + d < SUBLANES, suffix * shifted, suffix)
            above = jnp.where(sub + 1 < SUBLANES, pltpu.roll(suffix, SUBLANES - 1, axis=0), 1.0)
            carry = carry_ref[hh]
            scale = above * carry
            w = jnp.concatenate(
                [beta[g * SUBLANES:(g + 1) * SUBLANES] * (later[g] * scale) for g in range(CHUNK)], axis=0)
            carry_ref[hh] = carry * jnp.broadcast_to(suffix[0:1], carry.shape)
            w_ref[hh] = w.astype(BF16)

    put_logits(logits(qi))
    tok_k, tok_q = _tile_tokens()
    nxt = logits(jnp.maximum(qi - 1, 0))
    weights(valid=tok_k < tok_q)
    put_logits(nxt)

    def left_body(i, carry):
        kj = qi - 1 - i
        nxt = logits(jnp.maximum(kj - 1, 0))
        accumulate(kj + 1)
        weights()
        put_logits(nxt)
        return carry

    lax.fori_loop(0, qi, left_body, 0)
    accumulate(0)

    o_t =jnp.concatenate([acc_ref[0], acc_ref[1]], axis=0)
    o_ref[0] = o_t.T.astype(o_ref.dtype)


def _sb_attention(q_t, k, v_t):
    B, S, _ = k.shape
    n_blk = S // TOK_BLOCK
    pair = 2 * SB_HEAD_DIM
    return pl.pallas_call(
        _sb_kernel,
        grid=(B, SB_HEADS // 2, n_blk),
        in_specs=[pl.BlockSpec((1, 1, pair, TOK_BLOCK), lambda b, h, i: (b, i, h, 0)),
                  pl.BlockSpec((1, S, pair), lambda b, h, i: (b, 0, h)),
                  pl.BlockSpec((1, n_blk, pair, TOK_BLOCK), lambda b, h, i: (b, 0, h, 0))],
        out_specs=pl.BlockSpec((1, TOK_BLOCK, pair), lambda b, h, i: (b, i, h)),
        out_shape=jax.ShapeDtypeStruct((B, S, SB_WIDTH), BF16),
        scratch_shapes=[pltpu.VMEM((2, SB_HEAD_DIM, TOK_BLOCK), F32),
                        pltpu.VMEM((2, SUBLANES, TOK_BLOCK), F32),
                        pltpu.VMEM((2, TOK_BLOCK, TOK_BLOCK), F32),
                        pltpu.VMEM((2, TOK_BLOCK, TOK_BLOCK), BF16)],
        compiler_params=_cparams(3),
        name="stick_breaking_attention",
    )(q_t, k, v_t)


def _out_kernel(oda_ref, osb_ref, gda_ref, gsb_ref, x_ref, gate_ref, w_ref, fg_ref, out_ref, *, final_norm):
    def gated(o_ref, g_ref):
        g = g_ref[0].astype(F32)
        return (o_ref[0].astype(F32) * (g / (1.0 + jnp.exp(-g)))).astype(BF16)

    split = oda_ref.shape[-1]
    y = jnp.dot(gated(oda_ref, gda_ref), w_ref[0:split, :], preferred_element_type=F32)
    y = y + jnp.dot(gated(osb_ref, gsb_ref), w_ref[split:, :], preferred_element_type=F32)
    xo = x_ref[0] + gate_ref[0] * y
    if final_norm:
        xo = xo * lax.rsqrt(jnp.mean(xo * xo, axis=-1, keepdims=True) + EPS) * fg_ref[...]
    out_ref[0] = xo


def _out_project(o_da, o_sb, g_da, g_sb, x, gate, w_out, final_g, final_norm):
    B, S, D = x.shape
    half_spec = pl.BlockSpec((1, ROW_TILE, DA_WIDTH), lambda b, i: (b, i, 0))
    row_spec = pl.BlockSpec((1, ROW_TILE, D), lambda b, i: (b, i, 0))
    return pl.pallas_call(
        functools.partial(_out_kernel, final_norm=final_norm),
        grid=(B, S // ROW_TILE),
        in_specs=[half_spec] * 4 + [row_spec,
                                    pl.BlockSpec((1, 1, D), lambda b, i: (b, 0, 0)),
                                    pl.BlockSpec(w_out.shape, lambda b, i: (0, 0)),
                                    pl.BlockSpec((1, D), lambda b, i: (0, 0))],
        out_specs=row_spec,
        out_shape=jax.ShapeDtypeStruct((B, S, D), F32),
        compiler_params=_cparams(2),
        name="gate_out_proj",
    )(o_da, o_sb, g_da, g_sb, x, gate, w_out, final_g.reshape(1, D))


def _split_w_in(w):
    sizes = (DA_WIDTH, DA_WIDTH, DA_WIDTH, DA_WIDTH, SB_WIDTH, SB_WIDTH, SB_WIDTH, SB_WIDTH)
    q_da, k_da, v_da, g_da, q_sb, k_sb, v_sb, g_sb = jnp.split(w, np.cumsum(sizes)[:-1].tolist(), axis=1)
    w_tok = jnp.concatenate([k_da, k_sb, g_da, g_sb], axis=1).astype(BF16)
    q_scale = 1.0 / math.sqrt(DA_HEAD_DIM)
    w_feat_t = jnp.concatenate([q_da * q_scale, v_da, q_sb * q_scale, v_sb], axis=1).T.astype(BF16)
    return w_tok, w_feat_t


def kernel(x, c, norm_g, w_ada, b_ada, w_in, lambda_q1, lambda_k1, lambda_q2, lambda_k2, subln_g, w_out,
           rel_bias, final_g):
    B, S, D = x.shape
    depth = w_in.shape[0]
    assert S % ROW_TILE == 0 and ROW_TILE % TOK_BLOCK == 0
    assert DA_HEAD_DIM == SB_HEAD_DIM and DA_WIDTH == SB_WIDTH and DA_WIDTH + SB_WIDTH == w_out.shape[1]
    n_blk = S // TOK_BLOCK
    xp = x.reshape(B, n_blk, SUBLANES, CHUNK, D).transpose(0, 1, 3, 2, 4).reshape(B, S, D)
    bias = _bias_tiles(rel_bias)
    for l in range(depth):
        lambda_init = _lambda_init(l)
        mod, lam = _modulation(c, w_ada[l], b_ada[l], lambda_q1[l], lambda_k1[l], lambda_q2[l], lambda_k2[l],
                               lambda_init)
        shift, scl, gate = (m.reshape(B, 1, D) for m in jnp.split(mod, 3, axis=-1))
        w_tok, w_feat_t = _split_w_in(w_in[l])
        k_da, k_sb, g_da, g_sb, q_da, v_da, q_sb, v_sb = _project(xp, norm_g[l], scl, shift, w_tok, w_feat_t)
        o_da = _diff_attention(q_da, k_da, v_da, bias, lam[:, :1], subln_g[l], lambda_init)
        o_sb = _sb_attention(q_sb, k_sb, v_sb)
        xp = _out_project(o_da, o_sb, g_da, g_sb, xp, gate, w_out[l].astype(BF16), final_g, l == depth - 1)
    return xp.reshape(B, n_blk, CHUNK, SUBLANES, D).transpose(0, 1, 3, 2, 4).reshape(B, S, D)
```

```python
import functools
import math

import numpy as np
import jax
import jax.numpy as jnp
from jax import lax
from jax.experimental import pallas as pl
from jax.experimental.pallas import tpu as pltpu

DA_HEADS = 4
DA_HEAD_DIM = 64
DA_V_DIM = 2 * DA_HEAD_DIM
DA_WIDTH = DA_HEADS * DA_V_DIM
SB_HEADS = 8
SB_HEAD_DIM = 64
SB_WIDTH = SB_HEADS * SB_HEAD_DIM
N_BUCKETS = 32
MAX_DISTANCE = 128
EPS = 1e-6
NEG_INF = -1e30

SUBLANES = 8
LANES = 128
TOK_BLOCK = 256
CHUNK = TOK_BLOCK // SUBLANES
ROW_TILE = 512
VMEM_LIMIT = 48 * 1024 * 1024

F32 = jnp.float32
BF16 = jnp.bfloat16


def _lambda_init(layer_idx):
    return 0.8 - 0.6 * math.exp(-0.3 * layer_idx)


def _block_token(pos):
    return (pos % SUBLANES) * CHUNK + pos // SUBLANES


def _bucket_tiles():
    pos = np.arange(TOK_BLOCK)
    tok = _block_token(pos)
    rel0 = tok[None, :] - tok[:, None]
    rel = np.stack([np.maximum(rel0, 0), rel0 + TOK_BLOCK]).astype(np.int32)
    max_exact = N_BUCKETS // 2
    nf = np.maximum(rel, 1).astype(np.float32)
    large = max_exact + (np.log(nf / np.float32(max_exact)) / np.float32(math.log(MAX_DISTANCE / max_exact))
                         * np.float32(N_BUCKETS - max_exact)).astype(np.int32)
    large = np.minimum(large, N_BUCKETS - 1)
    bucket = np.where(rel < max_exact, rel, large).astype(np.int32)
    bucket[0] = np.where(rel0 < 0, N_BUCKETS, bucket[0])
    return bucket


def _strict_causal_tile():
    tok = _block_token(np.arange(TOK_BLOCK))
    return (tok[:, None] < tok[None, :]).astype(np.float32)


def _value_column_scale(n):
    g = (np.arange(n) % TOK_BLOCK) // SUBLANES
    return (2.0 ** -(CHUNK - g)).astype(np.float32).reshape(1, n)


def _cparams(n_axes):
    return pltpu.CompilerParams(dimension_semantics=("arbitrary",) * n_axes, vmem_limit_bytes=VMEM_LIMIT)


def _mod_kernel(c_ref, w_ref, b_ref, lq1_ref, lk1_ref, lq2_ref, lk2_ref, mod_ref, lam_ref, *, lambda_init):
    c = c_ref[...]
    silu_c = c / (1.0 + jnp.exp(-c))
    mod_ref[...] = jnp.dot(silu_c, w_ref[...], preferred_element_type=F32,
                           precision=lax.Precision.HIGHEST) + b_ref[...]
    s1 = jnp.sum(lq1_ref[...] * lk1_ref[...], axis=-1, keepdims=True)
    s2 = jnp.sum(lq2_ref[...] * lk2_ref[...], axis=-1, keepdims=True)
    lam_ref[...] = jnp.broadcast_to(jnp.exp(s1) - jnp.exp(s2) + lambda_init, lam_ref.shape)


def _modulation(c, w_ada, b_ada, lq1, lk1, lq2, lk2, lambda_init):
    B, D = c.shape
    n_out = w_ada.shape[1]
    col = 512
    vec = lambda: pl.BlockSpec((1, DA_HEAD_DIM), lambda j: (0, 0))
    return pl.pallas_call(
        functools.partial(_mod_kernel, lambda_init=lambda_init),
        grid=(n_out // col,),
        in_specs=[pl.BlockSpec((B, D), lambda j: (0, 0)),
                  pl.BlockSpec((D, col), lambda j: (0, j)),
                  pl.BlockSpec((1, col), lambda j: (0, j)),
                  vec(), vec(), vec(), vec()],
        out_specs=[pl.BlockSpec((B, col), lambda j: (0, j)),
                   pl.BlockSpec((1, LANES), lambda j: (0, 0))],
        out_shape=[jax.ShapeDtypeStruct((B, n_out), F32), jax.ShapeDtypeStruct((1, LANES), F32)],
        compiler_params=_cparams(1),
        name="adaln_mod",
    )(c, w_ada, b_ada.reshape(1, n_out), lq1.reshape(1, -1), lk1.reshape(1, -1), lq2.reshape(1, -1),
      lk2.reshape(1, -1))


def _bias_kernel(rb_ref, bucket_ref, out_ref):
    h = pl.program_id(0)
    far = rb_ref[N_BUCKETS - 1, h]
    for t in range(2):
        bucket = bucket_ref[t]
        tile = jnp.zeros(bucket.shape, F32)
        for b in range(N_BUCKETS - 1):
            tile = jnp.where(bucket == b, rb_ref[b, h] - far, tile)
        out_ref[0, t] = jnp.where(bucket == N_BUCKETS, NEG_INF, tile)


def _bias_tiles(rel_bias):
    buckets = jnp.asarray(_bucket_tiles())
    return pl.pallas_call(
        _bias_kernel,
        grid=(DA_HEADS,),
        in_specs=[pl.BlockSpec(memory_space=pltpu.SMEM),
                  pl.BlockSpec((2, TOK_BLOCK, TOK_BLOCK), lambda h: (0, 0, 0))],
        out_specs=pl.BlockSpec((1, 2, TOK_BLOCK, TOK_BLOCK), lambda h: (h, 0, 0, 0)),
        out_shape=jax.ShapeDtypeStruct((DA_HEADS, 2, TOK_BLOCK, TOK_BLOCK), F32),
        compiler_params=_cparams(1),
        name="rel_bias_tiles",
    )(rel_bias, buckets)


def _proj_kernel(x_ref, g_ref, scl_ref, shift_ref, wt_ref, wf_ref, vscale_ref,
                 kda_ref, ksb_ref, gda_ref, gsb_ref, qda_ref, vda_ref, qsb_ref, vsb_ref):
    x = x_ref[0]
    ms = jnp.mean(x * x, axis=-1, keepdims=True)
    xn = x * lax.rsqrt(ms + EPS) * g_ref[...]
    h = (xn * (1.0 + scl_ref[0]) + shift_ref[0]).astype(BF16)
    width = kda_ref.shape[-1]
    for i, o_ref in enumerate((kda_ref, ksb_ref, gda_ref, gsb_ref)):
        o_ref[0] = jnp.dot(h, wt_ref[:, i * width:(i + 1) * width], preferred_element_type=F32).astype(BF16)
    for i, o_ref in enumerate((qda_ref, vda_ref, qsb_ref, vsb_ref)):
        r = lax.dot_general(wf_ref[i * width:(i + 1) * width, :], h, (((1,), (1,)), ((), ())),
                            preferred_element_type=F32)
        if o_ref is vsb_ref:
            r = r * vscale_ref[...]
        r = r.astype(BF16)
        for t in range(o_ref.shape[1]):
            o_ref[0, t] = r[:, t * TOK_BLOCK:(t + 1) * TOK_BLOCK]


def _project(x, norm_g, scl, shift, w_tok, w_feat_t):
    B, S, D = x.shape
    n_blk = S // TOK_BLOCK
    blk_per_tile = ROW_TILE // TOK_BLOCK
    width = DA_WIDTH
    tok_spec = pl.BlockSpec((1, ROW_TILE, width), lambda b, i: (b, i, 0))
    feat_spec = pl.BlockSpec((1, blk_per_tile, width, TOK_BLOCK), lambda b, i: (b, i, 0, 0))
    tok_shape = jax.ShapeDtypeStruct((B, S, width), BF16)
    feat_shape = jax.ShapeDtypeStruct((B, n_blk, width, TOK_BLOCK), BF16)
    mod_spec = pl.BlockSpec((1, 1, D), lambda b, i: (b, 0, 0))
    return pl.pallas_call(
        _proj_kernel,
        grid=(B, S // ROW_TILE),
        in_specs=[pl.BlockSpec((1, ROW_TILE, D), lambda b, i: (b, i, 0)),
                  pl.BlockSpec((1, D), lambda b, i: (0, 0)),
                  mod_spec, mod_spec,
                  pl.BlockSpec(w_tok.shape, lambda b, i: (0, 0)),
                  pl.BlockSpec(w_feat_t.shape, lambda b, i: (0, 0)),
                  pl.BlockSpec((1, ROW_TILE), lambda b, i: (0, 0))],
        out_specs=[tok_spec] * 4 + [feat_spec] * 4,
        out_shape=[tok_shape] * 4 + [feat_shape] * 4,
        compiler_params=_cparams(2),
        name="norm_in_proj",
    )(x, norm_g.reshape(1, D), scl, shift, w_tok, w_feat_t, jnp.asarray(_value_column_scale(ROW_TILE)))


def _store_row_groups(q_t, qz_ref, group):
    row = lax.broadcasted_iota(jnp.int32, q_t.shape, 0)
    for i in range(qz_ref.shape[0]):
        inside = (row >= i * group) & (row < (i + 1) * group)
        qz_ref[i] = jnp.where(inside, q_t, jnp.zeros_like(q_t))


def _key_block(k_ref, kj):
    return k_ref[0, pl.ds(pl.multiple_of(kj * TOK_BLOCK, TOK_BLOCK), TOK_BLOCK), :]


def _da_kernel(lam_ref, q_ref, k_ref, v_ref, bias_ref, g_ref, o_ref,
               acc_ref, m_ref, l_ref, qz_ref, s_ref, smax_ref, p_ref, a_ref, *, out_scale):
    qi = pl.program_id(2)
    n_map = acc_ref.shape[0]
    _store_row_groups(q_ref[0, 0], qz_ref, DA_HEAD_DIM)
    acc_ref[...] = jnp.zeros(acc_ref.shape, F32)
    m_ref[...] = jnp.full(m_ref.shape, NEG_INF, F32)
    l_ref[...] = jnp.zeros(l_ref.shape, F32)
    p_ref[...] = jnp.zeros(p_ref.shape, BF16)
    a_ref[...] = jnp.ones(a_ref.shape, F32)

    def scores(kj):
        kb = _key_block(k_ref, kj)
        return [jnp.dot(kb, qz_ref[mp], preferred_element_type=F32) for mp in range(n_map)]

    def put_scores(ss, kind=None):
        for mp in range(n_map):
            s = ss[mp]
            if kind is not None:
                s = s + bias_ref[mp // 2, kind]
            s_ref[mp] = s
            smax_ref[mp] = jnp.max(s, axis=0, keepdims=True)

    def softmax_step():
        for mp in range(n_map):
            m_old = m_ref[mp]
            m_new = jnp.maximum(m_old, smax_ref[mp])
            alpha = jnp.exp(m_old - m_new)
            p = jnp.exp(s_ref[mp] - m_new)
            l_ref[mp] = alpha * l_ref[mp] + jnp.sum(p, axis=0, keepdims=True)
            m_ref[mp] = m_new
            a_ref[mp] = alpha
            p_ref[mp] = p.astype(BF16)

    def weighted_values(kj):
        vb = v_ref[0, kj]
        return [jnp.dot(vb[(mp // 2) * DA_V_DIM:(mp // 2 + 1) * DA_V_DIM], p_ref[mp], preferred_element_type=F32)
                for mp in range(n_map)]

    def add_weighted(pv):
        for mp in range(n_map):
            acc_ref[mp] = a_ref[mp] * acc_ref[mp] + pv[mp]

    def accumulate(kj):
        add_weighted(weighted_values(kj))

    def step(t, next_kind=None, has_next=True):
        pv = weighted_values(jnp.maximum(t - 1, 0))
        if has_next:
            nxt = scores(t + 1)
        add_weighted(pv)
        softmax_step()
        if has_next:
            put_scores(nxt, next_kind)

    first = scores(0)
    pl.when(qi == 0)(lambda: put_scores(first, 0))
    pl.when(qi == 1)(lambda: put_scores(first, 1))
    pl.when(qi >= 2)(lambda: put_scores(first))

    def plain_body(t, carry):
        step(t)
        return carry

    lax.fori_loop(0, jnp.maximum(qi - 2, 0), plain_body, 0)
    pl.when(qi >= 2)(lambda: step(qi - 2, 1))
    pl.when(qi >= 1)(lambda: step(qi - 1, 0))
    step(qi, has_next=False)
    accumulate(qi)

    lam = lam_ref[0, 0]
    outs = []
    for hd in range(n_map // 2):
        o_t = acc_ref[2 * hd] / l_ref[2 * hd] - lam * (acc_ref[2 * hd + 1] / l_ref[2 * hd + 1])
        o = o_t.T
        o = o * lax.rsqrt(jnp.mean(o * o, axis=-1, keepdims=True) + EPS) * g_ref[...]
        outs.append((o * out_scale).astype(o_ref.dtype))
    o_ref[0] = jnp.concatenate(outs, axis=-1)


DA_GROUP = 2


def _diff_attention(q_t, k, v_t, bias, lam, subln_g, lambda_init):
    B, S, _ = k.shape
    n_blk = S // TOK_BLOCK
    nhd = DA_GROUP
    n_map = 2 * nhd
    wide = nhd * DA_V_DIM
    return pl.pallas_call(
        functools.partial(_da_kernel, out_scale=1.0 - lambda_init),
        grid=(B, DA_HEADS // nhd, n_blk),
        in_specs=[pl.BlockSpec(memory_space=pltpu.SMEM),
                  pl.BlockSpec((1, 1, wide, TOK_BLOCK), lambda b, h, i: (b, i, h, 0)),
                  pl.BlockSpec((1, S, wide), lambda b, h, i: (b, 0, h)),
                  pl.BlockSpec((1, n_blk, wide, TOK_BLOCK), lambda b, h, i: (b, 0, h, 0)),
                  pl.BlockSpec((nhd, 2, TOK_BLOCK, TOK_BLOCK), lambda b, h, i: (h, 0, 0, 0)),
                  pl.BlockSpec((1, DA_V_DIM), lambda b, h, i: (0, 0))],
        out_specs=pl.BlockSpec((1, TOK_BLOCK, wide), lambda b, h, i: (b, i, h)),
        out_shape=jax.ShapeDtypeStruct((B, S, DA_WIDTH), BF16),
        scratch_shapes=[pltpu.VMEM((n_map, DA_V_DIM, TOK_BLOCK), F32),
                        pltpu.VMEM((n_map, 1, TOK_BLOCK), F32),
                        pltpu.VMEM((n_map, 1, TOK_BLOCK), F32),
                        pltpu.VMEM((n_map, wide, TOK_BLOCK), BF16),
                        pltpu.VMEM((n_map, TOK_BLOCK, TOK_BLOCK), F32),
                        pltpu.VMEM((n_map, 1, TOK_BLOCK), F32),
                        pltpu.VMEM((n_map, TOK_BLOCK, TOK_BLOCK), BF16),
                        pltpu.VMEM((n_map, 1, TOK_BLOCK), F32)],
        compiler_params=_cparams(3),
        name="diff_attention",
    )(lam, q_t, k, v_t, bias, subln_g.reshape(1, DA_V_DIM))


def _sb_kernel(q_ref, k_ref, v_ref, mask_ref, o_ref, acc_ref, carry_ref, qz_ref, u_ref, tot_ref, w_ref):
    qi = pl.program_id(2)
    nh = acc_ref.shape[0]
    _store_row_groups(q_ref[0, 0], qz_ref, SB_HEAD_DIM)
    acc_ref[...] = jnp.zeros(acc_ref.shape, F32)
    carry_ref[...] = jnp.ones(carry_ref.shape, F32)
    sub = lax.broadcasted_iota(jnp.int32, (SUBLANES, TOK_BLOCK), 0)
    rows = lambda x, g: x[g * SUBLANES:(g + 1) * SUBLANES]

    def logits(kj):
        kb = _key_block(k_ref, kj)
        return [jnp.dot(kb, qz_ref[hh], preferred_element_type=F32) for hh in range(nh)]

    def put_logits(zs, mask=None):
        for hh in range(nh):
            t = jnp.tanh(zs[hh])
            if mask is not None:
                t = jnp.where(mask > 0.5, t, -1.0)
            u = 1.0 - t
            u_ref[hh] = u
            tot = rows(u, CHUNK - 1)
            for g in reversed(range(CHUNK - 1)):
                tot = tot * rows(u, g)
            tot_ref[hh] = tot * 2.0 ** -CHUNK

    def weights():
        for hh in range(nh):
            suffix = tot_ref[hh]
            for d in (1, 2, 4):
                shifted = pltpu.roll(suffix, SUBLANES - d, axis=0)
                suffix = jnp.where(sub + d < SUBLANES, suffix * shifted, suffix)
            above = jnp.where(sub + 1 < SUBLANES, pltpu.roll(suffix, SUBLANES - 1, axis=0), 1.0)
            carry = carry_ref[hh]
            carry_ref[hh] = carry * jnp.broadcast_to(suffix[0:1], carry.shape)
            run = above * carry
            w = [None] * CHUNK
            for g in reversed(range(CHUNK)):
                ug = u_ref[hh, g * SUBLANES:(g + 1) * SUBLANES, :]
                w[g] = (2.0 - ug) * run
                run = run * ug
            w_ref[hh] = jnp.concatenate(w, axis=0).astype(BF16)

    def accumulate(kj):
        vb = v_ref[0, kj]
        for hh in range(nh):
            vh = vb[hh * SB_HEAD_DIM:(hh + 1) * SB_HEAD_DIM]
            acc_ref[hh] = acc_ref[hh] + jnp.dot(vh, w_ref[hh], preferred_element_type=F32)

    def alive():
        return (jnp.max(carry_ref[...]) > 0.0).astype(jnp.int32)

    put_logits(logits(qi), mask_ref[...])
    nxt = logits(jnp.maximum(qi - 1, 0))
    weights()
    put_logits(nxt)

    def left_cond(state):
        i, live = state
        return jnp.logical_and(i < qi, live > 0)

    def left_body(state):
        i, _ = state
        kj = qi - 1 - i
        nxt = logits(jnp.maximum(kj - 1, 0))
        accumulate(kj + 1)
        weights()
        put_logits(nxt)
        return i + 1, alive()

    n_left, _ = lax.while_loop(left_cond, left_body, (jnp.int32(0), alive()))
    accumulate(qi - n_left)

    o_t = jnp.concatenate([acc_ref[hh] for hh in range(nh)], axis=0)
    o_ref[0] = o_t.T.astype(o_ref.dtype)


SB_GROUP = 4


def _sb_attention(q_t, k, v_t):
    B, S, _ = k.shape
    n_blk = S // TOK_BLOCK
    nh = SB_GROUP
    pair = nh * SB_HEAD_DIM
    return pl.pallas_call(
        _sb_kernel,
        grid=(B, SB_HEADS // nh, n_blk),
        in_specs=[pl.BlockSpec((1, 1, pair, TOK_BLOCK), lambda b, h, i: (b, i, h, 0)),
                  pl.BlockSpec((1, S, pair), lambda b, h, i: (b, 0, h)),
                  pl.BlockSpec((1, n_blk, pair, TOK_BLOCK), lambda b, h, i: (b, 0, h, 0)),
                  pl.BlockSpec((TOK_BLOCK, TOK_BLOCK), lambda b, h, i: (0, 0))],
        out_specs=pl.BlockSpec((1, TOK_BLOCK, pair), lambda b, h, i: (b, i, h)),
        out_shape=jax.ShapeDtypeStruct((B, S, SB_WIDTH), BF16),
        scratch_shapes=[pltpu.VMEM((nh, SB_HEAD_DIM, TOK_BLOCK), F32),
                        pltpu.VMEM((nh, SUBLANES, TOK_BLOCK), F32),
                        pltpu.VMEM((nh, pair, TOK_BLOCK), BF16),
                        pltpu.VMEM((nh, TOK_BLOCK, TOK_BLOCK), F32),
                        pltpu.VMEM((nh, SUBLANES, TOK_BLOCK), F32),
                        pltpu.VMEM((nh, TOK_BLOCK, TOK_BLOCK), BF16)],
        compiler_params=_cparams(3),
        name="stick_breaking_attention",
    )(q_t, k, v_t, jnp.asarray(_strict_causal_tile()))


def _out_kernel(oda_ref, osb_ref, gda_ref, gsb_ref, x_ref, gate_ref, w_ref, fg_ref, out_ref, *, final_norm):
    def gated(o_ref, g_ref):
        g = g_ref[0].astype(F32)
        return (o_ref[0].astype(F32) * (g / (1.0 + jnp.exp(-g)))).astype(BF16)

    split = oda_ref.shape[-1]
    y = jnp.dot(gated(oda_ref, gda_ref), w_ref[0:split, :], preferred_element_type=F32)
    y = y + jnp.dot(gated(osb_ref, gsb_ref), w_ref[split:, :], preferred_element_type=F32)
    xo = x_ref[0] + gate_ref[0] * y
    if final_norm:
        xo = xo * lax.rsqrt(jnp.mean(xo * xo, axis=-1, keepdims=True) + EPS) * fg_ref[...]
    out_ref[0] = xo


def _out_project(o_da, o_sb, g_da, g_sb, x, gate, w_out, final_g, final_norm):
    B, S, D = x.shape
    half_spec = pl.BlockSpec((1, ROW_TILE, DA_WIDTH), lambda b, i: (b, i, 0))
    row_spec = pl.BlockSpec((1, ROW_TILE, D), lambda b, i: (b, i, 0))
    return pl.pallas_call(
        functools.partial(_out_kernel, final_norm=final_norm),
        grid=(B, S // ROW_TILE),
        in_specs=[half_spec] * 4 + [row_spec,
                                    pl.BlockSpec((1, 1, D), lambda b, i: (b, 0, 0)),
                                    pl.BlockSpec(w_out.shape, lambda b, i: (0, 0)),
                                    pl.BlockSpec((1, D), lambda b, i: (0, 0))],
        out_specs=row_spec,
        out_shape=jax.ShapeDtypeStruct((B, S, D), F32),
        compiler_params=_cparams(2),
        name="gate_out_proj",
    )(o_da, o_sb, g_da, g_sb, x, gate, w_out, final_g.reshape(1, D))


def _split_w_in(w):
    sizes = (DA_WIDTH, DA_WIDTH, DA_WIDTH, DA_WIDTH, SB_WIDTH, SB_WIDTH, SB_WIDTH, SB_WIDTH)
    q_da, k_da, v_da, g_da, q_sb, k_sb, v_sb, g_sb = jnp.split(w, np.cumsum(sizes)[:-1].tolist(), axis=1)
    w_tok = jnp.concatenate([k_da, k_sb, g_da, g_sb], axis=1).astype(BF16)
    q_scale = 1.0 / math.sqrt(DA_HEAD_DIM)
    w_feat_t = jnp.concatenate([q_da * q_scale, v_da, q_sb * (0.5 * q_scale), v_sb], axis=1).T.astype(BF16)
    return w_tok, w_feat_t


def kernel(x, c, norm_g, w_ada, b_ada, w_in, lambda_q1, lambda_k1, lambda_q2, lambda_k2, subln_g, w_out,
           rel_bias, final_g):
    B, S, D = x.shape
    depth = w_in.shape[0]
    assert S % ROW_TILE == 0 and ROW_TILE % TOK_BLOCK == 0
    assert DA_HEAD_DIM == SB_HEAD_DIM and DA_WIDTH == SB_WIDTH and DA_WIDTH + SB_WIDTH == w_out.shape[1]
    n_blk = S // TOK_BLOCK
    xp = x.reshape(B, n_blk, SUBLANES, CHUNK, D).transpose(0, 1, 3, 2, 4).reshape(B, S, D)
    bias = _bias_tiles(rel_bias)
    for l in range(depth):
        lambda_init = _lambda_init(l)
        mod, lam = _modulation(c, w_ada[l], b_ada[l], lambda_q1[l], lambda_k1[l], lambda_q2[l], lambda_k2[l],
                               lambda_init)
        shift, scl, gate = (m.reshape(B, 1, D) for m in jnp.split(mod, 3, axis=-1))
        w_tok, w_feat_t = _split_w_in(w_in[l])
        k_da, k_sb, g_da, g_sb, q_da, v_da, q_sb, v_sb = _project(xp, norm_g[l], scl, shift, w_tok, w_feat_t)
        o_da = _diff_attention(q_da, k_da, v_da, bias, lam[:, :1], subln_g[l], lambda_init)
        o_sb = _sb_attention(q_sb, k_sb, v_sb)
        xp = _out_project(o_da, o_sb, g_da, g_sb, xp, gate, w_out[l].astype(BF16), final_g, l == depth - 1)
    return xp.reshape(B, n_blk, CHUNK, SUBLANES, D).transpose(0, 1, 3, 2, 4).reshape(B, S, D)
```

```python
import functools
import math

import numpy as np
import jax
import jax.numpy as jnp
from jax import lax
from jax.experimental import pallas as pl
from jax.experimental.pallas import tpu as pltpu

DA_HEADS = 4
DA_HEAD_DIM = 64
DA_V_DIM = 2 * DA_HEAD_DIM
DA_WIDTH = DA_HEADS * DA_V_DIM
SB_HEADS = 8
SB_HEAD_DIM = 64
SB_WIDTH = SB_HEADS * SB_HEAD_DIM
N_BUCKETS = 32
MAX_DISTANCE = 128
EPS = 1e-6
NEG_INF = -1e30

SUBLANES = 8
LANES = 128
TOK_BLOCK = 256
CHUNK = TOK_BLOCK // SUBLANES
DA_BLOCK = 2 * TOK_BLOCK
LOG2E = math.log2(math.e)
ROW_TILE = 512
VMEM_LIMIT = 48 * 1024 * 1024

F32 = jnp.float32
BF16 = jnp.bfloat16


def _lambda_init(layer_idx):
    return 0.8 - 0.6 * math.exp(-0.3 * layer_idx)


def _block_token(pos):
    return (pos % SUBLANES) * CHUNK + pos // SUBLANES


def _bucket_tiles():
    pos = np.arange(DA_BLOCK)
    tok = (pos // TOK_BLOCK) * TOK_BLOCK + _block_token(pos % TOK_BLOCK)
    rel0 = tok[None, :] - tok[:, None]
    rel = np.stack([np.maximum(rel0, 0), rel0 + DA_BLOCK]).astype(np.int32)
    max_exact = N_BUCKETS // 2
    nf = np.maximum(rel, 1).astype(np.float32)
    large = max_exact + (np.log(nf / np.float32(max_exact)) / np.float32(math.log(MAX_DISTANCE / max_exact))
                         * np.float32(N_BUCKETS - max_exact)).astype(np.int32)
    large = np.minimum(large, N_BUCKETS - 1)
    bucket = np.where(rel < max_exact, rel, large).astype(np.int32)
    bucket[0] = np.where(rel0 < 0, N_BUCKETS, bucket[0])
    return bucket


def _position_to_token():
    pos = np.arange(TOK_BLOCK)
    m = np.zeros((TOK_BLOCK, TOK_BLOCK), np.float32)
    m[pos, _block_token(pos)] = 1.0
    return m


def _strict_causal_tile():
    tok = _block_token(np.arange(TOK_BLOCK))
    return (tok[:, None] < tok[None, :]).astype(np.float32)


def _value_column_scale(n):
    g = (np.arange(n) % TOK_BLOCK) // SUBLANES
    return (2.0 ** -(CHUNK - g)).astype(np.float32).reshape(1, n)


def _cparams(n_axes):
    return pltpu.CompilerParams(dimension_semantics=("arbitrary",) * n_axes, vmem_limit_bytes=VMEM_LIMIT)


def _mod_kernel(c_ref, w_ref, b_ref, lq1_ref, lk1_ref, lq2_ref, lk2_ref, mod_ref, lam_ref, *, lambda_init):
    c = c_ref[...]
    silu_c = c / (1.0 + jnp.exp(-c))
    mod_ref[...] = jnp.dot(silu_c, w_ref[...], preferred_element_type=F32,
                           precision=lax.Precision.HIGHEST) + b_ref[...]
    s1 = jnp.sum(lq1_ref[...] * lk1_ref[...], axis=-1, keepdims=True)
    s2 = jnp.sum(lq2_ref[...] * lk2_ref[...], axis=-1, keepdims=True)
    lam_ref[...] = jnp.broadcast_to(jnp.exp(s1) - jnp.exp(s2) + lambda_init, lam_ref.shape)


def _modulation(c, w_ada, b_ada, lq1, lk1, lq2, lk2, lambda_init):
    B, D = c.shape
    n_out = w_ada.shape[1]
    col = 512
    vec = lambda: pl.BlockSpec((1, DA_HEAD_DIM), lambda j: (0, 0))
    return pl.pallas_call(
        functools.partial(_mod_kernel, lambda_init=lambda_init),
        grid=(n_out // col,),
        in_specs=[pl.BlockSpec((B, D), lambda j: (0, 0)),
                  pl.BlockSpec((D, col), lambda j: (0, j)),
                  pl.BlockSpec((1, col), lambda j: (0, j)),
                  vec(), vec(), vec(), vec()],
        out_specs=[pl.BlockSpec((B, col), lambda j: (0, j)),
                   pl.BlockSpec((1, LANES), lambda j: (0, 0))],
        out_shape=[jax.ShapeDtypeStruct((B, n_out), F32), jax.ShapeDtypeStruct((1, LANES), F32)],
        compiler_params=_cparams(1),
        name="adaln_mod",
    )(c, w_ada, b_ada.reshape(1, n_out), lq1.reshape(1, -1), lk1.reshape(1, -1), lq2.reshape(1, -1),
      lk2.reshape(1, -1))


def _bias_kernel(rb_ref, bucket_ref, out_ref):
    h = pl.program_id(0)
    far = rb_ref[N_BUCKETS - 1, h]
    for t in range(2):
        bucket = bucket_ref[t]
        tile = jnp.zeros(bucket.shape, F32)
        for b in range(N_BUCKETS - 1):
            tile = jnp.where(bucket == b, (rb_ref[b, h] - far) * LOG2E, tile)
        out_ref[0, t] = jnp.where(bucket == N_BUCKETS, NEG_INF, tile)


def _bias_tiles(rel_bias):
    buckets = jnp.asarray(_bucket_tiles())
    return pl.pallas_call(
        _bias_kernel,
        grid=(DA_HEADS,),
        in_specs=[pl.BlockSpec(memory_space=pltpu.SMEM),
                  pl.BlockSpec((2, DA_BLOCK, DA_BLOCK), lambda h: (0, 0, 0))],
        out_specs=pl.BlockSpec((1, 2, DA_BLOCK, DA_BLOCK), lambda h: (h, 0, 0, 0)),
        out_shape=jax.ShapeDtypeStruct((DA_HEADS, 2, DA_BLOCK, DA_BLOCK), F32),
        compiler_params=_cparams(1),
        name="rel_bias_tiles",
    )(rel_bias, buckets)


def _proj_kernel(x_ref, g_ref, scl_ref, shift_ref, wt_ref, wf_ref, vscale_ref, perm_ref,
                 kda_ref, ksb_ref, gda_ref, gsb_ref, qda_ref, vda_ref, qsb_ref, vsb_ref):
    x = x_ref[0]
    ms = jnp.mean(x * x, axis=-1, keepdims=True)
    xn = x * lax.rsqrt(ms + EPS) * g_ref[...]
    h = (xn * (1.0 + scl_ref[0]) + shift_ref[0]).astype(BF16)
    hp = jnp.concatenate(
        [jnp.dot(perm_ref[...], h[j * TOK_BLOCK:(j + 1) * TOK_BLOCK], preferred_element_type=F32).astype(BF16)
         for j in range(h.shape[0] // TOK_BLOCK)], axis=0)
    width = kda_ref.shape[-1]
    for i, (o_ref, src) in enumerate(((kda_ref, hp), (ksb_ref, hp), (gda_ref, h), (gsb_ref, h))):
        o_ref[0] = jnp.dot(src, wt_ref[:, i * width:(i + 1) * width], preferred_element_type=F32).astype(BF16)
    for i, o_ref in enumerate((qda_ref, vda_ref, qsb_ref, vsb_ref)):
        r = lax.dot_general(wf_ref[i * width:(i + 1) * width, :], hp, (((1,), (1,)), ((), ())),
                            preferred_element_type=F32)
        if o_ref is vsb_ref:
            r = r * vscale_ref[...]
        r = r.astype(BF16)
        for t in range(o_ref.shape[1]):
            o_ref[0, t] = r[:, t * TOK_BLOCK:(t + 1) * TOK_BLOCK]


def _project(x, norm_g, scl, shift, w_tok, w_feat_t):
    B, S, D = x.shape
    n_blk = S // TOK_BLOCK
    blk_per_tile = ROW_TILE // TOK_BLOCK
    width = DA_WIDTH
    tok_spec = pl.BlockSpec((1, ROW_TILE, width), lambda b, i: (b, i, 0))
    feat_spec = pl.BlockSpec((1, blk_per_tile, width, TOK_BLOCK), lambda b, i: (b, i, 0, 0))
    tok_shape = jax.ShapeDtypeStruct((B, S, width), BF16)
    feat_shape = jax.ShapeDtypeStruct((B, n_blk, width, TOK_BLOCK), BF16)
    mod_spec = pl.BlockSpec((1, 1, D), lambda b, i: (b, 0, 0))
    return pl.pallas_call(
        _proj_kernel,
        grid=(B, S // ROW_TILE),
        in_specs=[pl.BlockSpec((1, ROW_TILE, D), lambda b, i: (b, i, 0)),
                  pl.BlockSpec((1, D), lambda b, i: (0, 0)),
                  mod_spec, mod_spec,
                  pl.BlockSpec(w_tok.shape, lambda b, i: (0, 0)),
                  pl.BlockSpec(w_feat_t.shape, lambda b, i: (0, 0)),
                  pl.BlockSpec((1, ROW_TILE), lambda b, i: (0, 0)),
                  pl.BlockSpec((TOK_BLOCK, TOK_BLOCK), lambda b, i: (0, 0))],
        out_specs=[tok_spec] * 4 + [feat_spec] * 4,
        out_shape=[tok_shape] * 4 + [feat_shape] * 4,
        compiler_params=_cparams(2),
        name="norm_in_proj",
    )(x, norm_g.reshape(1, D), scl, shift, w_tok, w_feat_t, jnp.asarray(_value_column_scale(ROW_TILE)),
      jnp.asarray(_position_to_token(), BF16))


def _store_row_groups(q_t, qz_ref, group):
    row = lax.broadcasted_iota(jnp.int32, q_t.shape, 0)
    for i in range(qz_ref.shape[0]):
        inside = (row >= i * group) & (row < (i + 1) * group)
        qz_ref[i] = jnp.where(inside, q_t, jnp.zeros_like(q_t))


def _key_block(k_ref, kj, rows=TOK_BLOCK):
    return k_ref[0, pl.ds(pl.multiple_of(kj * rows, rows), rows), :]


def _da_kernel(lam_ref, q_ref, k_ref, v_ref, bias_ref, g_ref, o_ref,
               acc_ref, m_ref, l_ref, qz_ref, s_ref, smax_ref, p_ref, a_ref, *, out_scale):
    qi = pl.program_id(2)
    n_map = acc_ref.shape[0]
    n_sub = q_ref.shape[1]
    _store_row_groups(jnp.concatenate([q_ref[0, j] for j in range(n_sub)], axis=-1), qz_ref, DA_HEAD_DIM)
    acc_ref[...] = jnp.zeros(acc_ref.shape, F32)
    m_ref[...] = jnp.full(m_ref.shape, NEG_INF, F32)
    l_ref[...] = jnp.zeros(l_ref.shape, F32)
    p_ref[...] = jnp.zeros(p_ref.shape, BF16)
    a_ref[...] = jnp.ones(a_ref.shape, F32)

    def scores(kj):
        kb = _key_block(k_ref, kj, DA_BLOCK)
        return [jnp.dot(kb, qz_ref[mp], preferred_element_type=F32) for mp in range(n_map)]

    def put_scores(ss, kind=None):
        for mp in range(n_map):
            s = ss[mp]
            if kind is not None:
                s = s + bias_ref[mp // 2, kind]
            s_ref[mp] = s
            smax_ref[mp] = jnp.max(s, axis=0, keepdims=True)

    def softmax_step():
        for mp in range(n_map):
            m_old = m_ref[mp]
            m_new = jnp.maximum(m_old, smax_ref[mp])
            alpha = jnp.exp2(m_old - m_new)
            p = jnp.exp2(s_ref[mp] - m_new)
            l_ref[mp] = alpha * l_ref[mp] + jnp.sum(p, axis=0, keepdims=True)
            m_ref[mp] = m_new
            a_ref[mp] = alpha
            p_ref[mp] = p.astype(BF16)

    def weighted_values(kj):
        out = []
        for mp in range(n_map):
            pv = None
            for j in range(n_sub):
                vh = v_ref[0, n_sub * kj + j, (mp // 2) * DA_V_DIM:(mp // 2 + 1) * DA_V_DIM, :]
                part = jnp.dot(vh, p_ref[mp, j * TOK_BLOCK:(j + 1) * TOK_BLOCK, :], preferred_element_type=F32)
                pv = part if pv is None else pv + part
            out.append(pv)
        return out

    def add_weighted(pv):
        for mp in range(n_map):
            acc_ref[mp] = a_ref[mp] * acc_ref[mp] + pv[mp]

    def accumulate(kj):
        add_weighted(weighted_values(kj))

    def step(t, next_kind=None, has_next=True):
        pv = weighted_values(jnp.maximum(t - 1, 0))
        if has_next:
            nxt = scores(t + 1)
        add_weighted(pv)
        softmax_step()
        if has_next:
            put_scores(nxt, next_kind)

    first = scores(0)
    pl.when(qi == 0)(lambda: put_scores(first, 0))
    pl.when(qi == 1)(lambda: put_scores(first, 1))
    pl.when(qi >= 2)(lambda: put_scores(first))

    def plain_body(t, carry):
        step(t)
        return carry

    lax.fori_loop(0, jnp.maximum(qi - 2, 0), plain_body, 0)
    pl.when(qi >= 2)(lambda: step(qi - 2, 1))
    pl.when(qi >= 1)(lambda: step(qi - 1, 0))
    step(qi, has_next=False)
    accumulate(qi)

    lam = lam_ref[0, 0]
    outs = []
    for hd in range(n_map // 2):
        o_t = acc_ref[2 * hd] / l_ref[2 * hd] - lam * (acc_ref[2 * hd + 1] / l_ref[2 * hd + 1])
        o = o_t.T
        o = o * lax.rsqrt(jnp.mean(o * o, axis=-1, keepdims=True) + EPS) * g_ref[...]
        outs.append((o * out_scale).astype(o_ref.dtype))
    o_ref[0] = jnp.concatenate(outs, axis=-1)


DA_GROUP = 2


def _diff_attention(q_t, k, v_t, bias, lam, subln_g, lambda_init):
    B, S, _ = k.shape
    n_blk = S // TOK_BLOCK
    n_sub = DA_BLOCK // TOK_BLOCK
    nhd = DA_GROUP
    n_map = 2 * nhd
    wide = nhd * DA_V_DIM
    return pl.pallas_call(
        functools.partial(_da_kernel, out_scale=1.0 - lambda_init),
        grid=(B, DA_HEADS // nhd, S // DA_BLOCK),
        in_specs=[pl.BlockSpec(memory_space=pltpu.SMEM),
                  pl.BlockSpec((1, n_sub, wide, TOK_BLOCK), lambda b, h, i: (b, i, h, 0)),
                  pl.BlockSpec((1, S, wide), lambda b, h, i: (b, 0, h)),
                  pl.BlockSpec((1, n_blk, wide, TOK_BLOCK), lambda b, h, i: (b, 0, h, 0)),
                  pl.BlockSpec((nhd, 2, DA_BLOCK, DA_BLOCK), lambda b, h, i: (h, 0, 0, 0)),
                  pl.BlockSpec((1, DA_V_DIM), lambda b, h, i: (0, 0))],
        out_specs=pl.BlockSpec((1, DA_BLOCK, wide), lambda b, h, i: (b, i, h)),
        out_shape=jax.ShapeDtypeStruct((B, S, DA_WIDTH), BF16),
        scratch_shapes=[pltpu.VMEM((n_map, DA_V_DIM, DA_BLOCK), F32),
                        pltpu.VMEM((n_map, 1, DA_BLOCK), F32),
                        pltpu.VMEM((n_map, 1, DA_BLOCK), F32),
                        pltpu.VMEM((n_map, wide, DA_BLOCK), BF16),
                        pltpu.VMEM((n_map, DA_BLOCK, DA_BLOCK), F32),
                        pltpu.VMEM((n_map, 1, DA_BLOCK), F32),
                        pltpu.VMEM((n_map, DA_BLOCK, DA_BLOCK), BF16),
                        pltpu.VMEM((n_map, 1, DA_BLOCK), F32)],
        compiler_params=_cparams(3),
        name="diff_attention",
    )(lam, q_t, k, v_t, bias, subln_g.reshape(1, DA_V_DIM))


def _sb_kernel(q_ref, k_ref, v_ref, mask_ref, o_ref, acc_ref, carry_ref, qz_ref, u_ref, tot_ref, w_ref):
    qi = pl.program_id(2)
    nh = acc_ref.shape[0]
    _store_row_groups(q_ref[0, 0], qz_ref, SB_HEAD_DIM)
    acc_ref[...] = jnp.zeros(acc_ref.shape, F32)
    carry_ref[...] = jnp.ones(carry_ref.shape, F32)
    sub = lax.broadcasted_iota(jnp.int32, (SUBLANES, TOK_BLOCK), 0)
    rows = lambda x, g: x[g * SUBLANES:(g + 1) * SUBLANES]

    def logits(kj):
        kb = _key_block(k_ref, kj)
        return [jnp.dot(kb, qz_ref[hh], preferred_element_type=F32) for hh in range(nh)]

    def put_logits(zs, mask=None):
        for hh in range(nh):
            t = jnp.tanh(zs[hh])
            if mask is not None:
                t = jnp.where(mask > 0.5, t, -1.0)
            u = 1.0 - t
            u_ref[hh] = u
            tot = rows(u, CHUNK - 1)
            for g in reversed(range(CHUNK - 1)):
                tot = tot * rows(u, g)
            tot_ref[hh] = tot * 2.0 ** -CHUNK

    def weights():
        for hh in range(nh):
            suffix = tot_ref[hh]
            for d in (1, 2, 4):
                shifted = pltpu.roll(suffix, SUBLANES - d, axis=0)
                suffix = jnp.where(sub + d < SUBLANES, suffix * shifted, suffix)
            above = jnp.where(sub + 1 < SUBLANES, pltpu.roll(suffix, SUBLANES - 1, axis=0), 1.0)
            carry = carry_ref[hh]
            carry_ref[hh] = carry * jnp.broadcast_to(suffix[0:1], carry.shape)
            run = above * carry
            w = [None] * CHUNK
            for g in reversed(range(CHUNK)):
                ug = u_ref[hh, g * SUBLANES:(g + 1) * SUBLANES, :]
                w[g] = (2.0 - ug) * run
                run = run * ug
            w_ref[hh] = jnp.concatenate(w, axis=0).astype(BF16)

    def accumulate(kj):
        vb = v_ref[0, kj]
        for hh in range(nh):
            vh = vb[hh * SB_HEAD_DIM:(hh + 1) * SB_HEAD_DIM]
            acc_ref[hh] = acc_ref[hh] + jnp.dot(vh, w_ref[hh], preferred_element_type=F32)

    def alive():
        return (jnp.max(carry_ref[...]) > 0.0).astype(jnp.int32)

    put_logits(logits(qi), mask_ref[...])
    nxt = logits(jnp.maximum(qi - 1, 0))
    weights()
    put_logits(nxt)

    def left_cond(state):
        i, live = state
        return jnp.logical_and(i < qi, live > 0)

    def left_body(state):
        i, _ = state
        kj = qi - 1 - i
        nxt = logits(jnp.maximum(kj - 1, 0))
        accumulate(kj + 1)
        weights()
        put_logits(nxt)
        return i + 1, alive()

    n_left, _ = lax.while_loop(left_cond, left_body, (jnp.int32(0), alive()))
    accumulate(qi - n_left)

    o_t = jnp.concatenate([acc_ref[hh] for hh in range(nh)], axis=0)
    o_ref[0] = o_t.T.astype(o_ref.dtype)


SB_GROUP = 4


def _sb_attention(q_t, k, v_t):
    B, S, _ = k.shape
    n_blk = S // TOK_BLOCK
    nh = SB_GROUP
    pair = nh * SB_HEAD_DIM
    return pl.pallas_call(
        _sb_kernel,
        grid=(B, SB_HEADS // nh, n_blk),
        in_specs=[pl.BlockSpec((1, 1, pair, TOK_BLOCK), lambda b, h, i: (b, i, h, 0)),
                  pl.BlockSpec((1, S, pair), lambda b, h, i: (b, 0, h)),
                  pl.BlockSpec((1, n_blk, pair, TOK_BLOCK), lambda b, h, i: (b, 0, h, 0)),
                  pl.BlockSpec((TOK_BLOCK, TOK_BLOCK), lambda b, h, i: (0, 0))],
        out_specs=pl.BlockSpec((1, TOK_BLOCK, pair), lambda b, h, i: (b, i, h)),
        out_shape=jax.ShapeDtypeStruct((B, S, SB_WIDTH), BF16),
        scratch_shapes=[pltpu.VMEM((nh, SB_HEAD_DIM, TOK_BLOCK), F32),
                        pltpu.VMEM((nh, SUBLANES, TOK_BLOCK), F32),
                        pltpu.VMEM((nh, pair, TOK_BLOCK), BF16),
                        pltpu.VMEM((nh, TOK_BLOCK, TOK_BLOCK), F32),
                        pltpu.VMEM((nh, SUBLANES, TOK_BLOCK), F32),
                        pltpu.VMEM((nh, TOK_BLOCK, TOK_BLOCK), BF16)],
        compiler_params=_cparams(3),
        name="stick_breaking_attention",
    )(q_t, k, v_t, jnp.asarray(_strict_causal_tile()))


def _out_kernel(oda_ref, osb_ref, gda_ref, gsb_ref, x_ref, gate_ref, w_ref, fg_ref, unperm_ref, out_ref, *,
                final_norm):
    def gated(o_ref, g_ref):
        o = jnp.concatenate(
            [jnp.dot(unperm_ref[...], o_ref[0, j * TOK_BLOCK:(j + 1) * TOK_BLOCK, :], preferred_element_type=F32)
             for j in range(o_ref.shape[1] // TOK_BLOCK)], axis=0)
        g = g_ref[0].astype(F32)
        return (o * (g / (1.0 + jnp.exp(-g)))).astype(BF16)

    split = oda_ref.shape[-1]
    y = jnp.dot(gated(oda_ref, gda_ref), w_ref[0:split, :], preferred_element_type=F32)
    y = y + jnp.dot(gated(osb_ref, gsb_ref), w_ref[split:, :], preferred_element_type=F32)
    xo = x_ref[0] + gate_ref[0] * y
    if final_norm:
        xo = xo * lax.rsqrt(jnp.mean(xo * xo, axis=-1, keepdims=True) + EPS) * fg_ref[...]
    out_ref[0] = xo


def _out_project(o_da, o_sb, g_da, g_sb, x, gate, w_out, final_g, final_norm):
    B, S, D = x.shape
    half_spec = pl.BlockSpec((1, ROW_TILE, DA_WIDTH), lambda b, i: (b, i, 0))
    row_spec = pl.BlockSpec((1, ROW_TILE, D), lambda b, i: (b, i, 0))
    return pl.pallas_call(
        functools.partial(_out_kernel, final_norm=final_norm),
        grid=(B, S // ROW_TILE),
        in_specs=[half_spec] * 4 + [row_spec,
                                    pl.BlockSpec((1, 1, D), lambda b, i: (b, 0, 0)),
                                    pl.BlockSpec(w_out.shape, lambda b, i: (0, 0)),
                                    pl.BlockSpec((1, D), lambda b, i: (0, 0)),
                                    pl.BlockSpec((TOK_BLOCK, TOK_BLOCK), lambda b, i: (0, 0))],
        out_specs=row_spec,
        out_shape=jax.ShapeDtypeStruct((B, S, D), F32),
        compiler_params=_cparams(2),
        name="gate_out_proj",
    )(o_da, o_sb, g_da, g_sb, x, gate, w_out, final_g.reshape(1, D), jnp.asarray(_position_to_token().T, BF16))


def _split_w_in(w):
    sizes = (DA_WIDTH, DA_WIDTH, DA_WIDTH, DA_WIDTH, SB_WIDTH, SB_WIDTH, SB_WIDTH, SB_WIDTH)
    q_da, k_da, v_da, g_da, q_sb, k_sb, v_sb, g_sb = jnp.split(w, np.cumsum(sizes)[:-1].tolist(), axis=1)
    w_tok = jnp.concatenate([k_da, k_sb, g_da, g_sb], axis=1).astype(BF16)
    q_scale = 1.0 / math.sqrt(DA_HEAD_DIM)
    w_feat_t = jnp.concatenate([q_da * (LOG2E * q_scale), v_da, q_sb * (0.5 * q_scale), v_sb], axis=1)
    return w_tok, w_feat_t.T.astype(BF16)


def kernel(x, c, norm_g, w_ada, b_ada, w_in, lambda_q1, lambda_k1, lambda_q2, lambda_k2, subln_g, w_out,
           rel_bias, final_g):
    B, S, D = x.shape
    depth = w_in.shape[0]
    assert S % ROW_TILE == 0 and ROW_TILE % TOK_BLOCK == 0 and S % DA_BLOCK == 0
    assert DA_HEAD_DIM == SB_HEAD_DIM and DA_WIDTH == SB_WIDTH and DA_WIDTH + SB_WIDTH == w_out.shape[1]
    bias = _bias_tiles(rel_bias)
    for l in range(depth):
        lambda_init = _lambda_init(l)
        mod, lam = _modulation(c, w_ada[l], b_ada[l], lambda_q1[l], lambda_k1[l], lambda_q2[l], lambda_k2[l],
                               lambda_init)
        shift, scl, gate = (m.reshape(B, 1, D) for m in jnp.split(mod, 3, axis=-1))
        w_tok, w_feat_t = _split_w_in(w_in[l])
        k_da, k_sb, g_da, g_sb, q_da, v_da, q_sb, v_sb = _project(x, norm_g[l], scl, shift, w_tok, w_feat_t)
        o_da = _diff_attention(q_da, k_da, v_da, bias, lam[:, :1], subln_g[l], lambda_init)
        o_sb = _sb_attention(q_sb, k_sb, v_sb)
        x = _out_project(o_da, o_sb, g_da, g_sb, x, gate, w_out[l].astype(BF16), final_g, l == depth - 1)
    return x
```

```python
import functools
import math

import numpy as np
import jax
import jax.numpy as jnp
from jax import lax
from jax.experimental import pallas as pl
from jax.experimental.pallas import tpu as pltpu

DA_HEADS = 4
DA_HEAD_DIM = 64
DA_V_DIM = 2 * DA_HEAD_DIM
DA_WIDTH = DA_HEADS * DA_V_DIM
SB_HEADS = 8
SB_HEAD_DIM = 64
SB_WIDTH = SB_HEADS * SB_HEAD_DIM
N_BUCKETS = 32
MAX_DISTANCE = 128
EPS = 1e-6
NEG_INF = -1e30

SUBLANES = 8
LANES = 128
TOK_BLOCK = 256
CHUNK = TOK_BLOCK // SUBLANES
DA_BLOCK = 2 * TOK_BLOCK
LOG2E = math.log2(math.e)
ROW_TILE = 512
VMEM_LIMIT = 48 * 1024 * 1024

F32 = jnp.float32
BF16 = jnp.bfloat16


def _lambda_init(layer_idx):
    return 0.8 - 0.6 * math.exp(-0.3 * layer_idx)


def _block_token(pos):
    return (pos % SUBLANES) * CHUNK + pos // SUBLANES


def _bucket_tiles():
    pos = np.arange(DA_BLOCK)
    tok = (pos // TOK_BLOCK) * TOK_BLOCK + _block_token(pos % TOK_BLOCK)
    rel0 = tok[None, :] - tok[:, None]
    rel = np.stack([np.maximum(rel0, 0), rel0 + DA_BLOCK]).astype(np.int32)
    max_exact = N_BUCKETS // 2
    nf = np.maximum(rel, 1).astype(np.float32)
    large = max_exact + (np.log(nf / np.float32(max_exact)) / np.float32(math.log(MAX_DISTANCE / max_exact))
                         * np.float32(N_BUCKETS - max_exact)).astype(np.int32)
    large = np.minimum(large, N_BUCKETS - 1)
    bucket = np.where(rel < max_exact, rel, large).astype(np.int32)
    bucket[0] = np.where(rel0 < 0, N_BUCKETS, bucket[0])
    return bucket


def _position_to_token():
    pos = np.arange(TOK_BLOCK)
    m = np.zeros((TOK_BLOCK, TOK_BLOCK), np.float32)
    m[pos, _block_token(pos)] = 1.0
    return m


def _strict_causal_tile():
    tok = _block_token(np.arange(TOK_BLOCK))
    return (tok[:, None] < tok[None, :]).astype(np.float32)


def _value_column_scale(n):
    g = (np.arange(n) % TOK_BLOCK) // SUBLANES
    return (2.0 ** -(CHUNK - g)).astype(np.float32).reshape(1, n)


def _cparams(n_axes):
    return pltpu.CompilerParams(dimension_semantics=("arbitrary",) * n_axes, vmem_limit_bytes=VMEM_LIMIT)


def _mod_kernel(c_ref, w_ref, b_ref, lq1_ref, lk1_ref, lq2_ref, lk2_ref, mod_ref, lam_ref, *, lambda_init):
    c = c_ref[...]
    silu_c = c / (1.0 + jnp.exp(-c))
    mod_ref[...] = jnp.dot(silu_c, w_ref[...], preferred_element_type=F32,
                           precision=lax.Precision.HIGHEST) + b_ref[...]
    s1 = jnp.sum(lq1_ref[...] * lk1_ref[...], axis=-1, keepdims=True)
    s2 = jnp.sum(lq2_ref[...] * lk2_ref[...], axis=-1, keepdims=True)
    lam_ref[...] = jnp.broadcast_to(jnp.exp(s1) - jnp.exp(s2) + lambda_init, lam_ref.shape)


def _modulation(c, w_ada, b_ada, lq1, lk1, lq2, lk2, lambda_init):
    B, D = c.shape
    n_out = w_ada.shape[1]
    col = 512
    vec = lambda: pl.BlockSpec((1, DA_HEAD_DIM), lambda j: (0, 0))
    return pl.pallas_call(
        functools.partial(_mod_kernel, lambda_init=lambda_init),
        grid=(n_out // col,),
        in_specs=[pl.BlockSpec((B, D), lambda j: (0, 0)),
                  pl.BlockSpec((D, col), lambda j: (0, j)),
                  pl.BlockSpec((1, col), lambda j: (0, j)),
                  vec(), vec(), vec(), vec()],
        out_specs=[pl.BlockSpec((B, col), lambda j: (0, j)),
                   pl.BlockSpec((1, LANES), lambda j: (0, 0))],
        out_shape=[jax.ShapeDtypeStruct((B, n_out), F32), jax.ShapeDtypeStruct((1, LANES), F32)],
        compiler_params=_cparams(1),
        name="adaln_mod",
    )(c, w_ada, b_ada.reshape(1, n_out), lq1.reshape(1, -1), lk1.reshape(1, -1), lq2.reshape(1, -1),
      lk2.reshape(1, -1))


def _bias_kernel(rb_ref, bucket_ref, out_ref):
    h = pl.program_id(0)
    far = rb_ref[N_BUCKETS - 1, h]
    for t in range(2):
        bucket = bucket_ref[t]
        tile = jnp.zeros(bucket.shape, F32)
        for b in range(N_BUCKETS - 1):
            tile = jnp.where(bucket == b, (rb_ref[b, h] - far) * LOG2E, tile)
        out_ref[0, t] = jnp.where(bucket == N_BUCKETS, NEG_INF, tile)


def _bias_tiles(rel_bias):
    buckets = jnp.asarray(_bucket_tiles())
    return pl.pallas_call(
        _bias_kernel,
        grid=(DA_HEADS,),
        in_specs=[pl.BlockSpec(memory_space=pltpu.SMEM),
                  pl.BlockSpec((2, DA_BLOCK, DA_BLOCK), lambda h: (0, 0, 0))],
        out_specs=pl.BlockSpec((1, 2, DA_BLOCK, DA_BLOCK), lambda h: (h, 0, 0, 0)),
        out_shape=jax.ShapeDtypeStruct((DA_HEADS, 2, DA_BLOCK, DA_BLOCK), F32),
        compiler_params=_cparams(1),
        name="rel_bias_tiles",
    )(rel_bias, buckets)


def _proj_kernel(x_ref, g_ref, scl_ref, shift_ref, wt_ref, wf_ref, vscale_ref, perm_ref,
                 kda_ref, ksb_ref, gda_ref, gsb_ref, qda_ref, vda_ref, qsb_ref, vsb_ref):
    x = x_ref[0]
    ms = jnp.mean(x * x, axis=-1, keepdims=True)
    xn = x * lax.rsqrt(ms + EPS) * g_ref[...]
    h = (xn * (1.0 + scl_ref[0]) + shift_ref[0]).astype(BF16)
    hp = jnp.concatenate(
        [jnp.dot(perm_ref[...], h[j * TOK_BLOCK:(j + 1) * TOK_BLOCK], preferred_element_type=F32).astype(BF16)
         for j in range(h.shape[0] // TOK_BLOCK)], axis=0)
    width = kda_ref.shape[-1]
    for i, (o_ref, src) in enumerate(((kda_ref, hp), (ksb_ref, hp), (gda_ref, h), (gsb_ref, h))):
        o_ref[0] = jnp.dot(src, wt_ref[:, i * width:(i + 1) * width], preferred_element_type=F32).astype(BF16)
    for i, o_ref in enumerate((qda_ref, vda_ref, qsb_ref, vsb_ref)):
        r = lax.dot_general(wf_ref[i * width:(i + 1) * width, :], hp, (((1,), (1,)), ((), ())),
                            preferred_element_type=F32)
        if o_ref is vsb_ref:
            r = r * vscale_ref[...]
        r = r.astype(BF16)
        for t in range(o_ref.shape[1]):
            o_ref[0, t] = r[:, t * TOK_BLOCK:(t + 1) * TOK_BLOCK]


def _project(x, norm_g, scl, shift, w_tok, w_feat_t):
    B, S, D = x.shape
    n_blk = S // TOK_BLOCK
    blk_per_tile = ROW_TILE // TOK_BLOCK
    width = DA_WIDTH
    tok_spec = pl.BlockSpec((1, ROW_TILE, width), lambda b, i: (b, i, 0))
    feat_spec = pl.BlockSpec((1, blk_per_tile, width, TOK_BLOCK), lambda b, i: (b, i, 0, 0))
    tok_shape = jax.ShapeDtypeStruct((B, S, width), BF16)
    feat_shape = jax.ShapeDtypeStruct((B, n_blk, width, TOK_BLOCK), BF16)
    mod_spec = pl.BlockSpec((1, 1, D), lambda b, i: (b, 0, 0))
    return pl.pallas_call(
        _proj_kernel,
        grid=(B, S // ROW_TILE),
        in_specs=[pl.BlockSpec((1, ROW_TILE, D), lambda b, i: (b, i, 0)),
                  pl.BlockSpec((1, D), lambda b, i: (0, 0)),
                  mod_spec, mod_spec,
                  pl.BlockSpec(w_tok.shape, lambda b, i: (0, 0)),
                  pl.BlockSpec(w_feat_t.shape, lambda b, i: (0, 0)),
                  pl.BlockSpec((1, ROW_TILE), lambda b, i: (0, 0)),
                  pl.BlockSpec((TOK_BLOCK, TOK_BLOCK), lambda b, i: (0, 0))],
        out_specs=[tok_spec] * 4 + [feat_spec] * 4,
        out_shape=[tok_shape] * 4 + [feat_shape] * 4,
        compiler_params=_cparams(2),
        name="norm_in_proj",
    )(x, norm_g.reshape(1, D), scl, shift, w_tok, w_feat_t, jnp.asarray(_value_column_scale(ROW_TILE)),
      jnp.asarray(_position_to_token(), BF16))


def _store_row_groups(q_t, qz_ref, group):
    row = lax.broadcasted_iota(jnp.int32, q_t.shape, 0)
    for i in range(qz_ref.shape[0]):
        inside = (row >= i * group) & (row < (i + 1) * group)
        qz_ref[i] = jnp.where(inside, q_t, jnp.zeros_like(q_t))


def _key_block(k_ref, kj, rows=TOK_BLOCK):
    return k_ref[0, pl.ds(pl.multiple_of(kj * rows, rows), rows), :]


def _da_kernel(lam_ref, q_ref, k_ref, v_ref, bias_ref, g_ref, o_ref,
               acc_ref, m_ref, l_ref, qz_ref, s_ref, smax_ref, p_ref, a_ref, *, out_scale):
    qi = pl.program_id(2)
    n_map = acc_ref.shape[0]
    n_sub = q_ref.shape[1]
    _store_row_groups(jnp.concatenate([q_ref[0, j] for j in range(n_sub)], axis=-1), qz_ref, DA_HEAD_DIM)
    acc_ref[...] = jnp.zeros(acc_ref.shape, F32)
    m_ref[...] = jnp.full(m_ref.shape, NEG_INF, F32)
    l_ref[...] = jnp.zeros(l_ref.shape, F32)
    p_ref[...] = jnp.zeros(p_ref.shape, BF16)
    a_ref[...] = jnp.ones(a_ref.shape, F32)

    def scores(kj):
        kb = _key_block(k_ref, kj, DA_BLOCK)
        return [jnp.dot(kb, qz_ref[mp], preferred_element_type=F32) for mp in range(n_map)]

    def put_scores(ss, kind=None):
        for mp in range(n_map):
            s = ss[mp]
            if kind is not None:
                s = s + bias_ref[mp // 2, kind]
            s_ref[mp] = s
            smax_ref[mp] = jnp.max(s, axis=0, keepdims=True)

    def softmax_step():
        for mp in range(n_map):
            m_old = m_ref[mp]
            m_new = jnp.maximum(m_old, smax_ref[mp])
            alpha = jnp.exp2(m_old - m_new)
            p = jnp.exp2(s_ref[mp] - m_new)
            l_ref[mp] = alpha * l_ref[mp] + jnp.sum(p, axis=0, keepdims=True)
            m_ref[mp] = m_new
            a_ref[mp] = alpha
            p_ref[mp] = p.astype(BF16)

    def weighted_values(kj):
        out = []
        for mp in range(n_map):
            pv = None
            for j in range(n_sub):
                vh = v_ref[0, n_sub * kj + j, (mp // 2) * DA_V_DIM:(mp // 2 + 1) * DA_V_DIM, :]
                part = jnp.dot(vh, p_ref[mp, j * TOK_BLOCK:(j + 1) * TOK_BLOCK, :], preferred_element_type=F32)
                pv = part if pv is None else pv + part
            out.append(pv)
        return out

    def add_weighted(pv):
        for mp in range(n_map):
            acc_ref[mp] = a_ref[mp] * acc_ref[mp] + pv[mp]

    def accumulate(kj):
        add_weighted(weighted_values(kj))

    def step(t, next_kind=None, has_next=True):
        pv = weighted_values(jnp.maximum(t - 1, 0))
        if has_next:
            nxt = scores(t + 1)
        add_weighted(pv)
        softmax_step()
        if has_next:
            put_scores(nxt, next_kind)

    first = scores(0)
    pl.when(qi == 0)(lambda: put_scores(first, 0))
    pl.when(qi == 1)(lambda: put_scores(first, 1))
    pl.when(qi >= 2)(lambda: put_scores(first))

    def plain_body(t, carry):
        step(t)
        return carry

    lax.fori_loop(0, jnp.maximum(qi - 2, 0), plain_body, 0)
    pl.when(qi >= 2)(lambda: step(qi - 2, 1))
    pl.when(qi >= 1)(lambda: step(qi - 1, 0))
    step(qi, has_next=False)
    accumulate(qi)

    lam = lam_ref[0, 0]
    outs = []
    for hd in range(n_map // 2):
        o_t = acc_ref[2 * hd] / l_ref[2 * hd] - lam * (acc_ref[2 * hd + 1] / l_ref[2 * hd + 1])
        o = o_t.T
        o = o * lax.rsqrt(jnp.mean(o * o, axis=-1, keepdims=True) + EPS) * g_ref[...]
        outs.append((o * out_scale).astype(o_ref.dtype))
    o_ref[0] = jnp.concatenate(outs, axis=-1)


DA_GROUP = 2


def _diff_attention(q_t, k, v_t, bias, lam, subln_g, lambda_init):
    B, S, _ = k.shape
    n_blk = S // TOK_BLOCK
    n_sub = DA_BLOCK // TOK_BLOCK
    nhd = DA_GROUP
    n_map = 2 * nhd
    wide = nhd * DA_V_DIM
    return pl.pallas_call(
        functools.partial(_da_kernel, out_scale=1.0 - lambda_init),
        grid=(B, DA_HEADS // nhd, S // DA_BLOCK),
        in_specs=[pl.BlockSpec(memory_space=pltpu.SMEM),
                  pl.BlockSpec((1, n_sub, wide, TOK_BLOCK), lambda b, h, i: (b, i, h, 0)),
                  pl.BlockSpec((1, S, wide), lambda b, h, i: (b, 0, h)),
                  pl.BlockSpec((1, n_blk, wide, TOK_BLOCK), lambda b, h, i: (b, 0, h, 0)),
                  pl.BlockSpec((nhd, 2, DA_BLOCK, DA_BLOCK), lambda b, h, i: (h, 0, 0, 0)),
                  pl.BlockSpec((1, DA_V_DIM), lambda b, h, i: (0, 0))],
        out_specs=pl.BlockSpec((1, DA_BLOCK, wide), lambda b, h, i: (b, i, h)),
        out_shape=jax.ShapeDtypeStruct((B, S, DA_WIDTH), BF16),
        scratch_shapes=[pltpu.VMEM((n_map, DA_V_DIM, DA_BLOCK), F32),
                        pltpu.VMEM((n_map, 1, DA_BLOCK), F32),
                        pltpu.VMEM((n_map, 1, DA_BLOCK), F32),
                        pltpu.VMEM((n_map, wide, DA_BLOCK), BF16),
                        pltpu.VMEM((n_map, DA_BLOCK, DA_BLOCK), F32),
                        pltpu.VMEM((n_map, 1, DA_BLOCK), F32),
                        pltpu.VMEM((n_map, DA_BLOCK, DA_BLOCK), BF16),
                        pltpu.VMEM((n_map, 1, DA_BLOCK), F32)],
        compiler_params=_cparams(3),
        name="diff_attention",
    )(lam, q_t, k, v_t, bias, subln_g.reshape(1, DA_V_DIM))


def _sb_kernel(q_ref, k_ref, v_ref, mask_ref, o_ref, acc_ref, carry_ref, qz_ref, u_ref, tot_ref, w_ref):
    qi = pl.program_id(2)
    nh = acc_ref.shape[0]
    _store_row_groups(q_ref[0, 0], qz_ref, SB_HEAD_DIM)
    carry_ref[...] = jnp.ones(carry_ref.shape, F32)
    sub = lax.broadcasted_iota(jnp.int32, (SUBLANES, TOK_BLOCK), 0)
    rows = lambda x, g: x[g * SUBLANES:(g + 1) * SUBLANES]

    def logits(kj):
        kb = _key_block(k_ref, kj)
        return [jnp.dot(kb, qz_ref[hh], preferred_element_type=F32) for hh in range(nh)]

    def gates(zs, slot, mask=None):
        for hh in range(nh):
            t = jnp.tanh(zs[hh])
            if mask is not None:
                t = jnp.where(mask > 0.5, t, -1.0)
            u = 1.0 - t
            u_ref[slot, hh] = u
            tot = rows(u, CHUNK - 1)
            for g in reversed(range(CHUNK - 1)):
                tot = tot * rows(u, g)
            tot_ref[slot, hh] = tot * 2.0 ** -CHUNK

    def weights(slot, keep_carry=None):
        for hh in range(nh):
            suffix = tot_ref[slot, hh]
            for d in (1, 2, 4):
                shifted = pltpu.roll(suffix, SUBLANES - d, axis=0)
                suffix = jnp.where(sub + d < SUBLANES, suffix * shifted, suffix)
            above = jnp.where(sub + 1 < SUBLANES, pltpu.roll(suffix, SUBLANES - 1, axis=0), 1.0)
            carry = carry_ref[hh]
            new_carry = carry * jnp.broadcast_to(suffix[0:1], carry.shape)
            carry_ref[hh] = new_carry if keep_carry is None else new_carry * keep_carry
            run = above * carry
            w = [None] * CHUNK
            for g in reversed(range(CHUNK)):
                ug = u_ref[slot, hh, g * SUBLANES:(g + 1) * SUBLANES, :]
                w[g] = (2.0 - ug) * run
                run = run * ug
            w_ref[slot, hh] = jnp.concatenate(w, axis=0).astype(BF16)

    def weighted_values(kj, slot):
        vb = v_ref[0, kj]
        return [jnp.dot(vb[hh * SB_HEAD_DIM:(hh + 1) * SB_HEAD_DIM], w_ref[slot, hh], preferred_element_type=F32)
                for hh in range(nh)]

    def alive():
        return (jnp.max(carry_ref[...]) > 0.0).astype(jnp.int32)

    before = jnp.maximum(qi - 1, 0)
    z_diag = logits(qi)
    z_before = logits(before)
    gates(z_diag, 0, mask_ref[...])
    gates(z_before, 1)
    weights(0, keep_carry=(qi > 0).astype(F32))
    weights(1)
    pv_diag = weighted_values(qi, 0)
    pv_before = weighted_values(before, 1)
    for hh in range(nh):
        acc_ref[hh] = pv_diag[hh] + pv_before[hh]

    more = jnp.logical_and(qi >= 2, alive() > 0)
    pl.when(more)(lambda: gates(logits(jnp.maximum(qi - 2, 0)), 0))

    def left_cond(state):
        i, live = state
        return jnp.logical_and(i < qi - 1, live > 0)

    def left_body(state):
        i, _ = state
        kj = qi - 2 - i
        nxt = logits(jnp.maximum(kj - 1, 0))
        weights(0)
        pv = weighted_values(kj, 0)
        for hh in range(nh):
            acc_ref[hh] = acc_ref[hh] + pv[hh]
        gates(nxt, 0)
        return i + 1, alive()

    lax.while_loop(left_cond, left_body, (jnp.int32(0), more.astype(jnp.int32)))

    o_t =jnp.concatenate([acc_ref[hh] for hh in range(nh)], axis=0)
    o_ref[0] = o_t.T.astype(o_ref.dtype)


SB_GROUP = 4


def _sb_attention(q_t, k, v_t):
    B, S, _ = k.shape
    n_blk = S // TOK_BLOCK
    nh = SB_GROUP
    pair = nh * SB_HEAD_DIM
    return pl.pallas_call(
        _sb_kernel,
        grid=(B, SB_HEADS // nh, n_blk),
        in_specs=[pl.BlockSpec((1, 1, pair, TOK_BLOCK), lambda b, h, i: (b, i, h, 0)),
                  pl.BlockSpec((1, S, pair), lambda b, h, i: (b, 0, h)),
                  pl.BlockSpec((1, n_blk, pair, TOK_BLOCK), lambda b, h, i: (b, 0, h, 0)),
                  pl.BlockSpec((TOK_BLOCK, TOK_BLOCK), lambda b, h, i: (0, 0))],
        out_specs=pl.BlockSpec((1, TOK_BLOCK, pair), lambda b, h, i: (b, i, h)),
        out_shape=jax.ShapeDtypeStruct((B, S, SB_WIDTH), BF16),
        scratch_shapes=[pltpu.VMEM((nh, SB_HEAD_DIM, TOK_BLOCK), F32),
                        pltpu.VMEM((nh, SUBLANES, TOK_BLOCK), F32),
                        pltpu.VMEM((nh, pair, TOK_BLOCK), BF16),
                        pltpu.VMEM((2, nh, TOK_BLOCK, TOK_BLOCK), F32),
                        pltpu.VMEM((2, nh, SUBLANES, TOK_BLOCK), F32),
                        pltpu.VMEM((2, nh, TOK_BLOCK, TOK_BLOCK), BF16)],
        compiler_params=_cparams(3),
        name="stick_breaking_attention",
    )(q_t, k, v_t, jnp.asarray(_strict_causal_tile()))


def _out_kernel(oda_ref, osb_ref, gda_ref, gsb_ref, x_ref, gate_ref, w_ref, fg_ref, unperm_ref, out_ref, *,
                final_norm):
    def gated(o_ref, g_ref):
        o = jnp.concatenate(
            [jnp.dot(unperm_ref[...], o_ref[0, j * TOK_BLOCK:(j + 1) * TOK_BLOCK, :], preferred_element_type=F32)
             for j in range(o_ref.shape[1] // TOK_BLOCK)], axis=0)
        g = g_ref[0].astype(F32)
        return (o * (g / (1.0 + jnp.exp(-g)))).astype(BF16)

    split = oda_ref.shape[-1]
    y = jnp.dot(gated(oda_ref, gda_ref), w_ref[0:split, :], preferred_element_type=F32)
    y = y + jnp.dot(gated(osb_ref, gsb_ref), w_ref[split:, :], preferred_element_type=F32)
    xo = x_ref[0] + gate_ref[0] * y
    if final_norm:
        xo = xo * lax.rsqrt(jnp.mean(xo * xo, axis=-1, keepdims=True) + EPS) * fg_ref[...]
    out_ref[0] = xo


def _out_project(o_da, o_sb, g_da, g_sb, x, gate, w_out, final_g, final_norm):
    B, S, D = x.shape
    half_spec = pl.BlockSpec((1, ROW_TILE, DA_WIDTH), lambda b, i: (b, i, 0))
    row_spec = pl.BlockSpec((1, ROW_TILE, D), lambda b, i: (b, i, 0))
    return pl.pallas_call(
        functools.partial(_out_kernel, final_norm=final_norm),
        grid=(B, S // ROW_TILE),
        in_specs=[half_spec] * 4 + [row_spec,
                                    pl.BlockSpec((1, 1, D), lambda b, i: (b, 0, 0)),
                                    pl.BlockSpec(w_out.shape, lambda b, i: (0, 0)),
                                    pl.BlockSpec((1, D), lambda b, i: (0, 0)),
                                    pl.BlockSpec((TOK_BLOCK, TOK_BLOCK), lambda b, i: (0, 0))],
        out_specs=row_spec,
        out_shape=jax.ShapeDtypeStruct((B, S, D), F32),
        compiler_params=_cparams(2),
        name="gate_out_proj",
    )(o_da, o_sb, g_da, g_sb, x, gate, w_out, final_g.reshape(1, D), jnp.asarray(_position_to_token().T, BF16))


def _split_w_in(w):
    sizes = (DA_WIDTH, DA_WIDTH, DA_WIDTH, DA_WIDTH, SB_WIDTH, SB_WIDTH, SB_WIDTH, SB_WIDTH)
    q_da, k_da, v_da, g_da, q_sb, k_sb, v_sb, g_sb = jnp.split(w, np.cumsum(sizes)[:-1].tolist(), axis=1)
    w_tok = jnp.concatenate([k_da, k_sb, g_da, g_sb], axis=1).astype(BF16)
    q_scale = 1.0 / math.sqrt(DA_HEAD_DIM)
    w_feat_t = jnp.concatenate([q_da * (LOG2E * q_scale), v_da, q_sb * (0.5 * q_scale), v_sb], axis=1)
    return w_tok, w_feat_t.T.astype(BF16)


def kernel(x, c, norm_g, w_ada, b_ada, w_in, lambda_q1, lambda_k1, lambda_q2, lambda_k2, subln_g, w_out,
           rel_bias, final_g):
    B, S, D = x.shape
    depth = w_in.shape[0]
    assert S % ROW_TILE == 0 and ROW_TILE % TOK_BLOCK == 0 and S % DA_BLOCK == 0
    assert DA_HEAD_DIM == SB_HEAD_DIM and DA_WIDTH == SB_WIDTH and DA_WIDTH + SB_WIDTH == w_out.shape[1]
    bias = _bias_tiles(rel_bias)
    for l in range(depth):
        lambda_init = _lambda_init(l)
        mod, lam = _modulation(c, w_ada[l], b_ada[l], lambda_q1[l], lambda_k1[l], lambda_q2[l], lambda_k2[l],
                               lambda_init)
        shift, scl, gate = (m.reshape(B, 1, D) for m in jnp.split(mod, 3, axis=-1))
        w_tok, w_feat_t = _split_w_in(w_in[l])
        k_da, k_sb, g_da, g_sb, q_da, v_da, q_sb, v_sb = _project(x, norm_g[l], scl, shift, w_tok, w_feat_t)
        o_da = _diff_attention(q_da, k_da, v_da, bias, lam[:, :1], subln_g[l], lambda_init)
        o_sb = _sb_attention(q_sb, k_sb, v_sb)
        x = _out_project(o_da, o_sb, g_da, g_sb, x, gate, w_out[l].astype(BF16), final_g, l == depth - 1)
    return x
```

```python
import functools
import math

import numpy as np
import jax
import jax.numpy as jnp
from jax import lax
from jax.experimental import pallas as pl
from jax.experimental.pallas import tpu as pltpu

DA_HEADS = 4
DA_HEAD_DIM = 64
DA_V_DIM = 2 * DA_HEAD_DIM
DA_WIDTH = DA_HEADS * DA_V_DIM
SB_HEADS = 8
SB_HEAD_DIM = 64
SB_WIDTH = SB_HEADS * SB_HEAD_DIM
N_BUCKETS = 32
MAX_DISTANCE = 128
EPS = 1e-6
NEG_INF = -1e30

SUBLANES = 8
LANES = 128
TOK_BLOCK = 256
CHUNK = TOK_BLOCK // SUBLANES
DA_BLOCK = 2 * TOK_BLOCK
LOG2E = math.log2(math.e)
ROW_TILE = 512
OUT_ROW_TILE = 1024
VMEM_LIMIT = 48 * 1024 * 1024

F32 = jnp.float32
BF16 = jnp.bfloat16


def _lambda_init(layer_idx):
    return 0.8 - 0.6 * math.exp(-0.3 * layer_idx)


def _block_token(pos):
    return (pos % SUBLANES) * CHUNK + pos // SUBLANES


def _bucket_tiles():
    pos = np.arange(DA_BLOCK)
    tok = (pos // TOK_BLOCK) * TOK_BLOCK + _block_token(pos % TOK_BLOCK)
    rel0 = tok[None, :] - tok[:, None]
    rel = np.stack([np.maximum(rel0, 0), rel0 + DA_BLOCK]).astype(np.int32)
    max_exact = N_BUCKETS // 2
    nf = np.maximum(rel, 1).astype(np.float32)
    large = max_exact + (np.log(nf / np.float32(max_exact)) / np.float32(math.log(MAX_DISTANCE / max_exact))
                         * np.float32(N_BUCKETS - max_exact)).astype(np.int32)
    large = np.minimum(large, N_BUCKETS - 1)
    bucket = np.where(rel < max_exact, rel, large).astype(np.int32)
    bucket[0] = np.where(rel0 < 0, N_BUCKETS, bucket[0])
    return bucket


def _position_to_token():
    pos = np.arange(TOK_BLOCK)
    m = np.zeros((TOK_BLOCK, TOK_BLOCK), np.float32)
    m[pos, _block_token(pos)] = 1.0
    return m


def _strict_causal_tile():
    tok = _block_token(np.arange(TOK_BLOCK))
    return (tok[:, None] < tok[None, :]).astype(np.float32)


def _value_column_scale(n):
    g = (np.arange(n) % TOK_BLOCK) // SUBLANES
    return (2.0 ** -(CHUNK - g)).astype(np.float32).reshape(1, n)


def _cparams(n_axes):
    return pltpu.CompilerParams(dimension_semantics=("arbitrary",) * n_axes, vmem_limit_bytes=VMEM_LIMIT)


def _mod_kernel(c_ref, w_ref, b_ref, lq1_ref, lk1_ref, lq2_ref, lk2_ref, mod_ref, lam_ref, *, lambda_init):
    c = c_ref[...]
    silu_c = c / (1.0 + jnp.exp(-c))
    mod_ref[...] = jnp.dot(silu_c, w_ref[...], preferred_element_type=F32,
                           precision=lax.Precision.HIGHEST) + b_ref[...]
    s1 = jnp.sum(lq1_ref[...] * lk1_ref[...], axis=-1, keepdims=True)
    s2 = jnp.sum(lq2_ref[...] * lk2_ref[...], axis=-1, keepdims=True)
    lam_ref[...] = jnp.broadcast_to(jnp.exp(s1) - jnp.exp(s2) + lambda_init, lam_ref.shape)


def _modulation(c, w_ada, b_ada, lq1, lk1, lq2, lk2, lambda_init):
    B, D = c.shape
    n_out = w_ada.shape[1]
    col = 512
    vec = lambda: pl.BlockSpec((1, DA_HEAD_DIM), lambda j: (0, 0))
    return pl.pallas_call(
        functools.partial(_mod_kernel, lambda_init=lambda_init),
        grid=(n_out // col,),
        in_specs=[pl.BlockSpec((B, D), lambda j: (0, 0)),
                  pl.BlockSpec((D, col), lambda j: (0, j)),
                  pl.BlockSpec((1, col), lambda j: (0, j)),
                  vec(), vec(), vec(), vec()],
        out_specs=[pl.BlockSpec((B, col), lambda j: (0, j)),
                   pl.BlockSpec((1, LANES), lambda j: (0, 0))],
        out_shape=[jax.ShapeDtypeStruct((B, n_out), F32), jax.ShapeDtypeStruct((1, LANES), F32)],
        compiler_params=_cparams(1),
        name="adaln_mod",
    )(c, w_ada, b_ada.reshape(1, n_out), lq1.reshape(1, -1), lk1.reshape(1, -1), lq2.reshape(1, -1),
      lk2.reshape(1, -1))


def _bias_kernel(rb_ref, bucket_ref, out_ref):
    h = pl.program_id(0)
    far = rb_ref[N_BUCKETS - 1, h]
    for t in range(2):
        bucket = bucket_ref[t]
        tile = jnp.zeros(bucket.shape, F32)
        for b in range(N_BUCKETS - 1):
            tile = jnp.where(bucket == b, (rb_ref[b, h] - far) * LOG2E, tile)
        out_ref[0, t] = jnp.where(bucket == N_BUCKETS, NEG_INF, tile)


def _bias_tiles(rel_bias):
    buckets = jnp.asarray(_bucket_tiles())
    return pl.pallas_call(
        _bias_kernel,
        grid=(DA_HEADS,),
        in_specs=[pl.BlockSpec(memory_space=pltpu.SMEM),
                  pl.BlockSpec((2, DA_BLOCK, DA_BLOCK), lambda h: (0, 0, 0))],
        out_specs=pl.BlockSpec((1, 2, DA_BLOCK, DA_BLOCK), lambda h: (h, 0, 0, 0)),
        out_shape=jax.ShapeDtypeStruct((DA_HEADS, 2, DA_BLOCK, DA_BLOCK), F32),
        compiler_params=_cparams(1),
        name="rel_bias_tiles",
    )(rel_bias, buckets)


def _proj_kernel(x_ref, g_ref, scl_ref, shift_ref, wt_ref, wf_ref, vscale_ref, perm_ref,
                 kda_ref, ksb_ref, gda_ref, gsb_ref, qda_ref, vda_ref, qsb_ref, vsb_ref):
    x = x_ref[0]
    ms = jnp.mean(x * x, axis=-1, keepdims=True)
    xn = x * lax.rsqrt(ms + EPS) * g_ref[...]
    h = (xn * (1.0 + scl_ref[0]) + shift_ref[0]).astype(BF16)
    hp = jnp.concatenate(
        [jnp.dot(perm_ref[...], h[j * TOK_BLOCK:(j + 1) * TOK_BLOCK], preferred_element_type=F32).astype(BF16)
         for j in range(h.shape[0] // TOK_BLOCK)], axis=0)
    width = kda_ref.shape[-1]
    for i, (o_ref, src) in enumerate(((kda_ref, hp), (ksb_ref, hp), (gda_ref, h), (gsb_ref, h))):
        o_ref[0] = jnp.dot(src, wt_ref[:, i * width:(i + 1) * width], preferred_element_type=F32).astype(BF16)
    for i, o_ref in enumerate((qda_ref, vda_ref, qsb_ref, vsb_ref)):
        r = lax.dot_general(wf_ref[i * width:(i + 1) * width, :], hp, (((1,), (1,)), ((), ())),
                            preferred_element_type=F32)
        if o_ref is vsb_ref:
            r = r * vscale_ref[...]
        r = r.astype(BF16)
        for t in range(o_ref.shape[1]):
            o_ref[0, t] = r[:, t * TOK_BLOCK:(t + 1) * TOK_BLOCK]


def _project(x, norm_g, scl, shift, w_tok, w_feat_t):
    B, S, D = x.shape
    n_blk = S // TOK_BLOCK
    blk_per_tile = ROW_TILE // TOK_BLOCK
    width = DA_WIDTH
    tok_spec = pl.BlockSpec((1, ROW_TILE, width), lambda b, i: (b, i, 0))
    feat_spec = pl.BlockSpec((1, blk_per_tile, width, TOK_BLOCK), lambda b, i: (b, i, 0, 0))
    tok_shape = jax.ShapeDtypeStruct((B, S, width), BF16)
    feat_shape = jax.ShapeDtypeStruct((B, n_blk, width, TOK_BLOCK), BF16)
    mod_spec = pl.BlockSpec((1, 1, D), lambda b, i: (b, 0, 0))
    return pl.pallas_call(
        _proj_kernel,
        grid=(B, S // ROW_TILE),
        in_specs=[pl.BlockSpec((1, ROW_TILE, D), lambda b, i: (b, i, 0)),
                  pl.BlockSpec((1, D), lambda b, i: (0, 0)),
                  mod_spec, mod_spec,
                  pl.BlockSpec(w_tok.shape, lambda b, i: (0, 0)),
                  pl.BlockSpec(w_feat_t.shape, lambda b, i: (0, 0)),
                  pl.BlockSpec((1, ROW_TILE), lambda b, i: (0, 0)),
                  pl.BlockSpec((TOK_BLOCK, TOK_BLOCK), lambda b, i: (0, 0))],
        out_specs=[tok_spec] * 4 + [feat_spec] * 4,
        out_shape=[tok_shape] * 4 + [feat_shape] * 4,
        compiler_params=_cparams(2),
        name="norm_in_proj",
    )(x, norm_g.reshape(1, D), scl, shift, w_tok, w_feat_t, jnp.asarray(_value_column_scale(ROW_TILE)),
      jnp.asarray(_position_to_token(), BF16))


def _store_row_groups(q_t, qz_ref, group):
    row = lax.broadcasted_iota(jnp.int32, q_t.shape, 0)
    for i in range(qz_ref.shape[0]):
        inside = (row >= i * group) & (row < (i + 1) * group)
        qz_ref[i] = jnp.where(inside, q_t, jnp.zeros_like(q_t))


def _key_block(k_ref, kj, rows=TOK_BLOCK):
    return k_ref[0, pl.ds(pl.multiple_of(kj * rows, rows), rows), :]


def _da_kernel(lam_ref, q_ref, k_ref, v_ref, bias_ref, g_ref, o_ref,
               acc_ref, m_ref, l_ref, qz_ref, s_ref, smax_ref, p_ref, a_ref, *, out_scale):
    qi = pl.program_id(2)
    n_map = acc_ref.shape[0]
    n_sub = q_ref.shape[1]
    _store_row_groups(jnp.concatenate([q_ref[0, j] for j in range(n_sub)], axis=-1), qz_ref, DA_HEAD_DIM)
    acc_ref[...] = jnp.zeros(acc_ref.shape, F32)
    m_ref[...] = jnp.full(m_ref.shape, NEG_INF, F32)
    l_ref[...] = jnp.zeros(l_ref.shape, F32)
    p_ref[...] = jnp.zeros(p_ref.shape, BF16)
    a_ref[...] = jnp.ones(a_ref.shape, F32)

    def scores(kj):
        kb = _key_block(k_ref, kj, DA_BLOCK)
        return [jnp.dot(kb, qz_ref[mp], preferred_element_type=F32) for mp in range(n_map)]

    def put_scores(ss, kind=None):
        for mp in range(n_map):
            s = ss[mp]
            if kind is not None:
                s = s + bias_ref[mp // 2, kind]
            s_ref[mp] = s
            smax_ref[mp] = jnp.max(s, axis=0, keepdims=True)

    def softmax_step():
        for mp in range(n_map):
            m_old = m_ref[mp]
            m_new = jnp.maximum(m_old, smax_ref[mp])
            alpha = jnp.exp2(m_old - m_new)
            p = jnp.exp2(s_ref[mp] - m_new)
            l_ref[mp] = alpha * l_ref[mp] + jnp.sum(p, axis=0, keepdims=True)
            m_ref[mp] = m_new
            a_ref[mp] = alpha
            p_ref[mp] = p.astype(BF16)

    def weighted_values(kj):
        out = []
        for mp in range(n_map):
            pv = None
            for j in range(n_sub):
                vh = v_ref[0, n_sub * kj + j, (mp // 2) * DA_V_DIM:(mp // 2 + 1) * DA_V_DIM, :]
                part = jnp.dot(vh, p_ref[mp, j * TOK_BLOCK:(j + 1) * TOK_BLOCK, :], preferred_element_type=F32)
                pv = part if pv is None else pv + part
            out.append(pv)
        return out

    def add_weighted(pv):
        for mp in range(n_map):
            acc_ref[mp] = a_ref[mp] * acc_ref[mp] + pv[mp]

    def accumulate(kj):
        add_weighted(weighted_values(kj))

    def step(t, next_kind=None, has_next=True):
        pv = weighted_values(jnp.maximum(t - 1, 0))
        if has_next:
            nxt = scores(t + 1)
        add_weighted(pv)
        softmax_step()
        if has_next:
            put_scores(nxt, next_kind)

    first = scores(0)
    pl.when(qi == 0)(lambda: put_scores(first, 0))
    pl.when(qi == 1)(lambda: put_scores(first, 1))
    pl.when(qi >= 2)(lambda: put_scores(first))

    def plain_body(t, carry):
        step(t)
        return carry

    lax.fori_loop(0, jnp.maximum(qi - 2, 0), plain_body, 0)
    pl.when(qi >= 2)(lambda: step(qi - 2, 1))
    pl.when(qi >= 1)(lambda: step(qi - 1, 0))
    step(qi, has_next=False)
    accumulate(qi)

    lam = lam_ref[0, 0]
    outs = []
    for hd in range(n_map // 2):
        o_t = acc_ref[2 * hd] / l_ref[2 * hd] - lam * (acc_ref[2 * hd + 1] / l_ref[2 * hd + 1])
        o = o_t.T
        o = o * lax.rsqrt(jnp.mean(o * o, axis=-1, keepdims=True) + EPS) * g_ref[...]
        outs.append((o * out_scale).astype(o_ref.dtype))
    o_ref[0] = jnp.concatenate(outs, axis=-1)


DA_GROUP = 2


def _diff_attention(q_t, k, v_t, bias, lam, subln_g, lambda_init):
    B, S, _ = k.shape
    n_blk = S // TOK_BLOCK
    n_sub = DA_BLOCK // TOK_BLOCK
    nhd = DA_GROUP
    n_map = 2 * nhd
    wide = nhd * DA_V_DIM
    return pl.pallas_call(
        functools.partial(_da_kernel, out_scale=1.0 - lambda_init),
        grid=(B, DA_HEADS // nhd, S // DA_BLOCK),
        in_specs=[pl.BlockSpec(memory_space=pltpu.SMEM),
                  pl.BlockSpec((1, n_sub, wide, TOK_BLOCK), lambda b, h, i: (b, i, h, 0)),
                  pl.BlockSpec((1, S, wide), lambda b, h, i: (b, 0, h)),
                  pl.BlockSpec((1, n_blk, wide, TOK_BLOCK), lambda b, h, i: (b, 0, h, 0)),
                  pl.BlockSpec((nhd, 2, DA_BLOCK, DA_BLOCK), lambda b, h, i: (h, 0, 0, 0)),
                  pl.BlockSpec((1, DA_V_DIM), lambda b, h, i: (0, 0))],
        out_specs=pl.BlockSpec((1, DA_BLOCK, wide), lambda b, h, i: (b, i, h)),
        out_shape=jax.ShapeDtypeStruct((B, S, DA_WIDTH), BF16),
        scratch_shapes=[pltpu.VMEM((n_map, DA_V_DIM, DA_BLOCK), F32),
                        pltpu.VMEM((n_map, 1, DA_BLOCK), F32),
                        pltpu.VMEM((n_map, 1, DA_BLOCK), F32),
                        pltpu.VMEM((n_map, wide, DA_BLOCK), BF16),
                        pltpu.VMEM((n_map, DA_BLOCK, DA_BLOCK), F32),
                        pltpu.VMEM((n_map, 1, DA_BLOCK), F32),
                        pltpu.VMEM((n_map, DA_BLOCK, DA_BLOCK), BF16),
                        pltpu.VMEM((n_map, 1, DA_BLOCK), F32)],
        compiler_params=_cparams(3),
        name="diff_attention",
    )(lam, q_t, k, v_t, bias, subln_g.reshape(1, DA_V_DIM))


def _sb_kernel(q_ref, k_ref, v_ref, mask_ref, o_ref, acc_ref, carry_ref, qz_ref, u_ref, tot_ref, w_ref):
    qi = pl.program_id(2)
    nh = acc_ref.shape[0]
    slab = qz_ref.shape[1]
    slab_heads = slab // SB_HEAD_DIM
    for sl in range(nh // slab_heads):
        _store_row_groups(q_ref[0, 0, sl * slab:(sl + 1) * slab, :], qz_ref.at[pl.ds(sl * slab_heads, slab_heads)],
                          SB_HEAD_DIM)
    carry_ref[...] = jnp.ones(carry_ref.shape, F32)
    sub = lax.broadcasted_iota(jnp.int32, (SUBLANES, TOK_BLOCK), 0)
    rows = lambda x, g: x[g * SUBLANES:(g + 1) * SUBLANES]

    def logits(kj):
        kb = _key_block(k_ref, kj)
        return [jnp.dot(kb[:, (hh // slab_heads) * slab:(hh // slab_heads + 1) * slab], qz_ref[hh],
                        preferred_element_type=F32) for hh in range(nh)]

    def gates(zs, slot, mask=None):
        for hh in range(nh):
            t = jnp.tanh(zs[hh])
            if mask is not None:
                t = jnp.where(mask > 0.5, t, -1.0)
            u = 1.0 - t
            u_ref[slot, hh] = u
            tot = rows(u, CHUNK - 1)
            for g in reversed(range(CHUNK - 1)):
                tot = tot * rows(u, g)
            tot_ref[slot, hh] = tot * 2.0 ** -CHUNK

    def weights(slot, keep_carry=None):
        for hh in range(nh):
            suffix = tot_ref[slot, hh]
            for d in (1, 2, 4):
                shifted = pltpu.roll(suffix, SUBLANES - d, axis=0)
                suffix = jnp.where(sub + d < SUBLANES, suffix * shifted, suffix)
            above = jnp.where(sub + 1 < SUBLANES, pltpu.roll(suffix, SUBLANES - 1, axis=0), 1.0)
            carry = carry_ref[hh]
            new_carry = carry * jnp.broadcast_to(suffix[0:1], carry.shape)
            carry_ref[hh] = new_carry if keep_carry is None else new_carry * keep_carry
            run = above * carry
            w = [None] * CHUNK
            for g in reversed(range(CHUNK)):
                ug = u_ref[slot, hh, g * SUBLANES:(g + 1) * SUBLANES, :]
                w[g] = (2.0 - ug) * run
                run = run * ug
            w_ref[slot, hh] = jnp.concatenate(w, axis=0).astype(BF16)

    def weighted_values(kj, slot):
        vb = v_ref[0, kj]
        return [jnp.dot(vb[hh * SB_HEAD_DIM:(hh + 1) * SB_HEAD_DIM], w_ref[slot, hh], preferred_element_type=F32)
                for hh in range(nh)]

    def alive():
        return (jnp.max(carry_ref[...]) > 0.0).astype(jnp.int32)

    before = jnp.maximum(qi - 1, 0)
    z_diag = logits(qi)
    z_before = logits(before)
    gates(z_diag, 0, mask_ref[...])
    gates(z_before, 1)
    weights(0, keep_carry=(qi > 0).astype(F32))
    weights(1)
    pv_diag = weighted_values(qi, 0)
    pv_before = weighted_values(before, 1)
    for hh in range(nh):
        acc_ref[hh] = pv_diag[hh] + pv_before[hh]

    more = jnp.logical_and(qi >= 2, alive() > 0)
    pl.when(more)(lambda: gates(logits(jnp.maximum(qi - 2, 0)), 0))

    def left_cond(state):
        i, live = state
        return jnp.logical_and(i < qi - 1, live > 0)

    def left_body(state):
        i, _ = state
        kj = qi - 2 - i
        nxt = logits(jnp.maximum(kj - 1, 0))
        weights(0)
        pv = weighted_values(kj, 0)
        for hh in range(nh):
            acc_ref[hh] = acc_ref[hh] + pv[hh]
        gates(nxt, 0)
        return i + 1, alive()

    lax.while_loop(left_cond, left_body, (jnp.int32(0), more.astype(jnp.int32)))

    o_t =jnp.concatenate([acc_ref[hh] for hh in range(nh)], axis=0)
    o_ref[0] = o_t.T.astype(o_ref.dtype)


SB_GROUP = 8


def _sb_attention(q_t, k, v_t):
    B, S, _ = k.shape
    n_blk = S // TOK_BLOCK
    nh = SB_GROUP
    pair = nh * SB_HEAD_DIM
    return pl.pallas_call(
        _sb_kernel,
        grid=(B, SB_HEADS // nh, n_blk),
        in_specs=[pl.BlockSpec((1, 1, pair, TOK_BLOCK), lambda b, h, i: (b, i, h, 0)),
                  pl.BlockSpec((1, S, pair), lambda b, h, i: (b, 0, h)),
                  pl.BlockSpec((1, n_blk, pair, TOK_BLOCK), lambda b, h, i: (b, 0, h, 0)),
                  pl.BlockSpec((TOK_BLOCK, TOK_BLOCK), lambda b, h, i: (0, 0))],
        out_specs=pl.BlockSpec((1, TOK_BLOCK, pair), lambda b, h, i: (b, i, h)),
        out_shape=jax.ShapeDtypeStruct((B, S, SB_WIDTH), BF16),
        scratch_shapes=[pltpu.VMEM((nh, SB_HEAD_DIM, TOK_BLOCK), F32),
                        pltpu.VMEM((nh, SUBLANES, TOK_BLOCK), F32),
                        pltpu.VMEM((nh, TOK_BLOCK, TOK_BLOCK), BF16),
                        pltpu.VMEM((2, nh, TOK_BLOCK, TOK_BLOCK), F32),
                        pltpu.VMEM((2, nh, SUBLANES, TOK_BLOCK), F32),
                        pltpu.VMEM((2, nh, TOK_BLOCK, TOK_BLOCK), BF16)],
        compiler_params=_cparams(3),
        name="stick_breaking_attention",
    )(q_t, k, v_t, jnp.asarray(_strict_causal_tile()))


def _out_kernel(oda_ref, osb_ref, gda_ref, gsb_ref, x_ref, gate_ref, w_ref, fg_ref, unperm_ref, out_ref, *,
                final_norm):
    def gated(o_ref, g_ref):
        o = jnp.concatenate(
            [jnp.dot(unperm_ref[...], o_ref[0, j * TOK_BLOCK:(j + 1) * TOK_BLOCK, :], preferred_element_type=F32)
             for j in range(o_ref.shape[1] // TOK_BLOCK)], axis=0)
        g = g_ref[0].astype(F32)
        return (o * (g / (1.0 + jnp.exp(-g)))).astype(BF16)

    split = oda_ref.shape[-1]
    y = jnp.dot(gated(oda_ref, gda_ref), w_ref[0:split, :], preferred_element_type=F32)
    y = y + jnp.dot(gated(osb_ref, gsb_ref), w_ref[split:, :], preferred_element_type=F32)
    xo = x_ref[0] + gate_ref[0] * y
    if final_norm:
        xo = xo * lax.rsqrt(jnp.mean(xo * xo, axis=-1, keepdims=True) + EPS) * fg_ref[...]
    out_ref[0] = xo


def _out_project(o_da, o_sb, g_da, g_sb, x, gate, w_out, final_g, final_norm):
    B, S, D = x.shape
    half_spec = pl.BlockSpec((1, OUT_ROW_TILE, DA_WIDTH), lambda b, i: (b, i, 0))
    row_spec = pl.BlockSpec((1, OUT_ROW_TILE, D), lambda b, i: (b, i, 0))
    return pl.pallas_call(
        functools.partial(_out_kernel, final_norm=final_norm),
        grid=(B, S // OUT_ROW_TILE),
        in_specs=[half_spec] * 4 + [row_spec,
                                    pl.BlockSpec((1, 1, D), lambda b, i: (b, 0, 0)),
                                    pl.BlockSpec(w_out.shape, lambda b, i: (0, 0)),
                                    pl.BlockSpec((1, D), lambda b, i: (0, 0)),
                                    pl.BlockSpec((TOK_BLOCK, TOK_BLOCK), lambda b, i: (0, 0))],
        out_specs=row_spec,
        out_shape=jax.ShapeDtypeStruct((B, S, D), F32),
        compiler_params=_cparams(2),
        name="gate_out_proj",
    )(o_da, o_sb, g_da, g_sb, x, gate, w_out, final_g.reshape(1, D), jnp.asarray(_position_to_token().T, BF16))


def _split_w_in(w):
    sizes = (DA_WIDTH, DA_WIDTH, DA_WIDTH, DA_WIDTH, SB_WIDTH, SB_WIDTH, SB_WIDTH, SB_WIDTH)
    q_da, k_da, v_da, g_da, q_sb, k_sb, v_sb, g_sb = jnp.split(w, np.cumsum(sizes)[:-1].tolist(), axis=1)
    w_tok = jnp.concatenate([k_da, k_sb, g_da, g_sb], axis=1).astype(BF16)
    q_scale = 1.0 / math.sqrt(DA_HEAD_DIM)
    w_feat_t = jnp.concatenate([q_da * (LOG2E * q_scale), v_da, q_sb * (0.5 * q_scale), v_sb], axis=1)
    return w_tok, w_feat_t.T.astype(BF16)


def kernel(x, c, norm_g, w_ada, b_ada, w_in, lambda_q1, lambda_k1, lambda_q2, lambda_k2, subln_g, w_out,
           rel_bias, final_g):
    B, S, D = x.shape
    depth = w_in.shape[0]
    assert S % ROW_TILE == 0 and ROW_TILE % TOK_BLOCK == 0 and S % DA_BLOCK == 0
    assert S % OUT_ROW_TILE == 0 and OUT_ROW_TILE % TOK_BLOCK == 0
    assert DA_HEAD_DIM == SB_HEAD_DIM and DA_WIDTH == SB_WIDTH and DA_WIDTH + SB_WIDTH == w_out.shape[1]
    bias = _bias_tiles(rel_bias)
    for l in range(depth):
        lambda_init = _lambda_init(l)
        mod, lam = _modulation(c, w_ada[l], b_ada[l], lambda_q1[l], lambda_k1[l], lambda_q2[l], lambda_k2[l],
                               lambda_init)
        shift, scl, gate = (m.reshape(B, 1, D) for m in jnp.split(mod, 3, axis=-1))
        w_tok, w_feat_t = _split_w_in(w_in[l])
        k_da, k_sb, g_da, g_sb, q_da, v_da, q_sb, v_sb = _project(x, norm_g[l], scl, shift, w_tok, w_feat_t)
        o_da = _diff_attention(q_da, k_da, v_da, bias, lam[:, :1], subln_g[l], lambda_init)
        o_sb = _sb_attention(q_sb, k_sb, v_sb)
        x = _out_project(o_da, o_sb, g_da, g_sb, x, gate, w_out[l].astype(BF16), final_g, l == depth - 1)
    return x
```

```python
import functools
import math

import numpy as np
import jax
import jax.numpy as jnp
from jax import lax
from jax.experimental import pallas as pl
from jax.experimental.pallas import tpu as pltpu

DA_HEADS = 4
DA_HEAD_DIM = 64
DA_V_DIM = 2 * DA_HEAD_DIM
DA_WIDTH = DA_HEADS * DA_V_DIM
SB_HEADS = 8
SB_HEAD_DIM = 64
SB_WIDTH = SB_HEADS * SB_HEAD_DIM
N_BUCKETS = 32
MAX_DISTANCE = 128
EPS = 1e-6
NEG_INF = -1e30

SUBLANES = 8
LANES = 128
TOK_BLOCK = 256
CHUNK = TOK_BLOCK // SUBLANES
DA_BLOCK = 2 * TOK_BLOCK
LOG2E = math.log2(math.e)
DIAG_TILE, NEAR_TILE = 0, 1
ROW_TILE = 512
OUT_ROW_TILE = 1024
VMEM_LIMIT = 56 * 1024 * 1024

F32 = jnp.float32
BF16 = jnp.bfloat16


def _lambda_init(layer_idx):
    return 0.8 - 0.6 * math.exp(-0.3 * layer_idx)


def _block_token(pos):
    return (pos % SUBLANES) * CHUNK + pos // SUBLANES


def _bucket_tiles():
    pos = np.arange(DA_BLOCK)
    tok = (pos // TOK_BLOCK) * TOK_BLOCK + _block_token(pos % TOK_BLOCK)
    rel0 = tok[None, :] - tok[:, None]
    rel = np.stack([np.maximum(rel0, 0), rel0 + DA_BLOCK]).astype(np.int32)
    max_exact = N_BUCKETS // 2
    nf = np.maximum(rel, 1).astype(np.float32)
    large = max_exact + (np.log(nf / np.float32(max_exact)) / np.float32(math.log(MAX_DISTANCE / max_exact))
                         * np.float32(N_BUCKETS - max_exact)).astype(np.int32)
    large = np.minimum(large, N_BUCKETS - 1)
    bucket = np.where(rel < max_exact, rel, large).astype(np.int32)
    bucket[0] = np.where(rel0 < 0, N_BUCKETS, bucket[0])
    return bucket


def _position_to_token():
    pos = np.arange(TOK_BLOCK)
    m = np.zeros((TOK_BLOCK, TOK_BLOCK), np.float32)
    m[pos, _block_token(pos)] = 1.0
    return m


def _strict_causal_tile():
    tok = _block_token(np.arange(TOK_BLOCK))
    return (tok[:, None] < tok[None, :]).astype(np.float32)


def _value_column_scale(n):
    g = (np.arange(n) % TOK_BLOCK) // SUBLANES
    return (2.0 ** -(CHUNK - g)).astype(np.float32).reshape(1, n)


def _cparams(n_axes):
    return pltpu.CompilerParams(dimension_semantics=("arbitrary",) * n_axes, vmem_limit_bytes=VMEM_LIMIT)


def _mod_kernel(c_ref, w_ref, b_ref, lq1_ref, lk1_ref, lq2_ref, lk2_ref, mod_ref, lam_ref, *, lambda_init):
    c = c_ref[...]
    silu_c = c / (1.0 + jnp.exp(-c))
    mod_ref[...] = jnp.dot(silu_c, w_ref[...], preferred_element_type=F32,
                           precision=lax.Precision.HIGHEST) + b_ref[...]
    s1 = jnp.sum(lq1_ref[...] * lk1_ref[...], axis=-1, keepdims=True)
    s2 = jnp.sum(lq2_ref[...] * lk2_ref[...], axis=-1, keepdims=True)
    lam_ref[...] = jnp.broadcast_to(jnp.exp(s1) - jnp.exp(s2) + lambda_init, lam_ref.shape)


def _modulation(c, w_ada, b_ada, lq1, lk1, lq2, lk2, lambda_init):
    B, D = c.shape
    n_out = w_ada.shape[1]
    col = 512
    vec = lambda: pl.BlockSpec((1, DA_HEAD_DIM), lambda j: (0, 0))
    return pl.pallas_call(
        functools.partial(_mod_kernel, lambda_init=lambda_init),
        grid=(n_out // col,),
        in_specs=[pl.BlockSpec((B, D), lambda j: (0, 0)),
                  pl.BlockSpec((D, col), lambda j: (0, j)),
                  pl.BlockSpec((1, col), lambda j: (0, j)),
                  vec(), vec(), vec(), vec()],
        out_specs=[pl.BlockSpec((B, col), lambda j: (0, j)),
                   pl.BlockSpec((1, LANES), lambda j: (0, 0))],
        out_shape=[jax.ShapeDtypeStruct((B, n_out), F32), jax.ShapeDtypeStruct((1, LANES), F32)],
        compiler_params=_cparams(1),
        name="adaln_mod",
    )(c, w_ada, b_ada.reshape(1, n_out), lq1.reshape(1, -1), lk1.reshape(1, -1), lq2.reshape(1, -1),
      lk2.reshape(1, -1))


def _bias_kernel(rb_ref, bucket_ref, out_ref):
    h = pl.program_id(0)
    far = rb_ref[N_BUCKETS - 1, h]
    for t in range(2):
        bucket = bucket_ref[t]
        tile = jnp.zeros(bucket.shape, F32)
        for b in range(N_BUCKETS - 1):
            tile = jnp.where(bucket == b, (rb_ref[b, h] - far) * LOG2E, tile)
        out_ref[0, t] = jnp.where(bucket == N_BUCKETS, NEG_INF, tile)


def _bias_tiles(rel_bias):
    buckets = jnp.asarray(_bucket_tiles())
    return pl.pallas_call(
        _bias_kernel,
        grid=(DA_HEADS,),
        in_specs=[pl.BlockSpec(memory_space=pltpu.SMEM),
                  pl.BlockSpec((2, DA_BLOCK, DA_BLOCK), lambda h: (0, 0, 0))],
        out_specs=pl.BlockSpec((1, 2, DA_BLOCK, DA_BLOCK), lambda h: (h, 0, 0, 0)),
        out_shape=jax.ShapeDtypeStruct((DA_HEADS, 2, DA_BLOCK, DA_BLOCK), F32),
        compiler_params=_cparams(1),
        name="rel_bias_tiles",
    )(rel_bias, buckets)


def _proj_kernel(x_ref, g_ref, scl_ref, shift_ref, wt_ref, wf_ref, vscale_ref, perm_ref,
                 kda_ref, ksb_ref, gda_ref, gsb_ref, qda_ref, vda_ref, qsb_ref, vsb_ref):
    x = x_ref[0]
    ms = jnp.mean(x * x, axis=-1, keepdims=True)
    xn = x * lax.rsqrt(ms + EPS) * g_ref[...]
    h = (xn * (1.0 + scl_ref[0]) + shift_ref[0]).astype(BF16)
    hp = jnp.concatenate(
        [jnp.dot(perm_ref[...], h[j * TOK_BLOCK:(j + 1) * TOK_BLOCK], preferred_element_type=F32).astype(BF16)
         for j in range(h.shape[0] // TOK_BLOCK)], axis=0)
    width = kda_ref.shape[-1]
    for i, (o_ref, src) in enumerate(((kda_ref, hp), (ksb_ref, hp), (gda_ref, h), (gsb_ref, h))):
        o_ref[0] = jnp.dot(src, wt_ref[:, i * width:(i + 1) * width], preferred_element_type=F32).astype(BF16)
    for i, o_ref in enumerate((qda_ref, vda_ref, qsb_ref, vsb_ref)):
        r = lax.dot_general(wf_ref[i * width:(i + 1) * width, :], hp, (((1,), (1,)), ((), ())),
                            preferred_element_type=F32)
        if o_ref is vsb_ref:
            r = r * vscale_ref[...]
        r = r.astype(BF16)
        for t in range(o_ref.shape[1]):
            o_ref[0, t] = r[:, t * TOK_BLOCK:(t + 1) * TOK_BLOCK]


def _project(x, norm_g, scl, shift, w_tok, w_feat_t):
    B, S, D = x.shape
    n_blk = S // TOK_BLOCK
    blk_per_tile = ROW_TILE // TOK_BLOCK
    width = DA_WIDTH
    tok_spec = pl.BlockSpec((1, ROW_TILE, width), lambda b, i: (b, i, 0))
    feat_spec = pl.BlockSpec((1, blk_per_tile, width, TOK_BLOCK), lambda b, i: (b, i, 0, 0))
    tok_shape = jax.ShapeDtypeStruct((B, S, width), BF16)
    feat_shape = jax.ShapeDtypeStruct((B, n_blk, width, TOK_BLOCK), BF16)
    mod_spec = pl.BlockSpec((1, 1, D), lambda b, i: (b, 0, 0))
    return pl.pallas_call(
        _proj_kernel,
        grid=(B, S // ROW_TILE),
        in_specs=[pl.BlockSpec((1, ROW_TILE, D), lambda b, i: (b, i, 0)),
                  pl.BlockSpec((1, D), lambda b, i: (0, 0)),
                  mod_spec, mod_spec,
                  pl.BlockSpec(w_tok.shape, lambda b, i: (0, 0)),
                  pl.BlockSpec(w_feat_t.shape, lambda b, i: (0, 0)),
                  pl.BlockSpec((1, ROW_TILE), lambda b, i: (0, 0)),
                  pl.BlockSpec((TOK_BLOCK, TOK_BLOCK), lambda b, i: (0, 0))],
        out_specs=[tok_spec] * 4 + [feat_spec] * 4,
        out_shape=[tok_shape] * 4 + [feat_shape] * 4,
        compiler_params=_cparams(2),
        name="norm_in_proj",
    )(x, norm_g.reshape(1, D), scl, shift, w_tok, w_feat_t, jnp.asarray(_value_column_scale(ROW_TILE)),
      jnp.asarray(_position_to_token(), BF16))


def _store_row_groups(q_t, qz_ref, group):
    row = lax.broadcasted_iota(jnp.int32, q_t.shape, 0)
    for i in range(qz_ref.shape[0]):
        inside = (row >= i * group) & (row < (i + 1) * group)
        qz_ref[i] = jnp.where(inside, q_t, jnp.zeros_like(q_t))


def _key_block(k_ref, kj, rows=TOK_BLOCK):
    return k_ref[0, pl.ds(pl.multiple_of(kj * rows, rows), rows), :]


def _da_tile_passes(n_q):
    plain = [(qi, kj) for qi in range(n_q) for kj in range(qi - 1)]
    near = [(qi, qi - 1) for qi in range(1, n_q)]
    diag = [(qi, qi) for qi in range(n_q)]
    passes, start = [], 0
    for tiles, kind in ((plain, None), (near, NEAR_TILE), (diag, DIAG_TILE)):
        passes.append((start, len(tiles), kind))
        start += len(tiles)
    order = np.asarray(plain + near + diag, np.int32)
    return order[:, 0], order[:, 1], tuple(passes)


def _da_kernel(lam_ref, qs_ref, ks_ref, q_ref, k_ref, v_ref, bias_ref, g_ref, o_ref,
               acc_ref, m_ref, l_ref, qz_ref, s_ref, smax_ref, p_ref, a_ref, *, out_scale, passes):
    n_q, n_map = acc_ref.shape[0], acc_ref.shape[1]
    n_sub = DA_BLOCK // TOK_BLOCK
    acc_ref[...] = jnp.zeros(acc_ref.shape, F32)
    m_ref[...] = jnp.full(m_ref.shape, NEG_INF, F32)
    l_ref[...] = jnp.zeros(l_ref.shape, F32)

    for qi in range(n_q):
        q_t = jnp.concatenate([q_ref[0, n_sub * qi + j] for j in range(n_sub)], axis=-1)
        _store_row_groups(q_t, qz_ref.at[qi], DA_HEAD_DIM)

    def scores(qi, kj):
        kb = _key_block(k_ref, kj, DA_BLOCK)
        return [jnp.dot(kb, qz_ref[qi, mp], preferred_element_type=F32) for mp in range(n_map)]

    def put_scores(ss, kind):
        for mp in range(n_map):
            s = ss[mp]
            if kind is not None:
                s = s + bias_ref[mp // 2, kind]
            s_ref[mp] = s
            smax_ref[mp] = jnp.max(s, axis=0, keepdims=True)

    def softmax_step(qi, slot):
        for mp in range(n_map):
            m_old = m_ref[qi, mp]
            m_new = jnp.maximum(m_old, smax_ref[mp])
            alpha = jnp.exp2(m_old - m_new)
            p = jnp.exp2(s_ref[mp] - m_new)
            l_ref[qi, mp] = alpha * l_ref[qi, mp] + jnp.sum(p, axis=0, keepdims=True)
            m_ref[qi, mp] = m_new
            a_ref[mp] = alpha
            p_ref[slot, mp] = p.astype(BF16)

    def weighted_values(kj, slot):
        out = []
        for mp in range(n_map):
            pv = None
            for j in range(n_sub):
                vh = v_ref[0, n_sub * kj + j, (mp // 2) * DA_V_DIM:(mp // 2 + 1) * DA_V_DIM, :]
                part = jnp.dot(vh, p_ref[slot, mp, j * TOK_BLOCK:(j + 1) * TOK_BLOCK, :], preferred_element_type=F32)
                pv = part if pv is None else pv + part
            out.append(pv)
        return out

    def add_weighted(pv, qi):
        for mp in range(n_map):
            acc_ref[qi, mp] = a_ref[mp] * acc_ref[qi, mp] + pv[mp]

    for slot, (start, count, kind) in enumerate(passes):
        slot = slot % 2
        if count == 0:
            continue
        last = start + count - 1
        p_ref[slot] = jnp.zeros(p_ref.shape[1:], BF16)
        a_ref[...] = jnp.ones(a_ref.shape, F32)
        put_scores(scores(qs_ref[start], ks_ref[start]), kind)

        def step(t, carry, start=start, last=last, kind=kind, slot=slot):
            cur = start + t
            prev = jnp.maximum(cur - 1, start)
            nxt = jnp.minimum(cur + 1, last)
            pv = weighted_values(ks_ref[prev], slot)
            next_scores = scores(qs_ref[nxt], ks_ref[nxt])
            add_weighted(pv, qs_ref[prev])
            softmax_step(qs_ref[cur], slot)
            put_scores(next_scores, kind)
            return carry

        lax.fori_loop(0, count, step, 0)
        add_weighted(weighted_values(ks_ref[last], slot), qs_ref[last])

    lam = lam_ref[0, 0]

    def finish(qi, carry):
        outs = []
        for hd in range(n_map // 2):
            o_t = (acc_ref[qi, 2 * hd] / l_ref[qi, 2 * hd]
                   - lam * (acc_ref[qi, 2 * hd + 1] / l_ref[qi, 2 * hd + 1]))
            o = o_t.T
            o = o * lax.rsqrt(jnp.mean(o * o, axis=-1, keepdims=True) + EPS) * g_ref[...]
            outs.append((o * out_scale).astype(o_ref.dtype))
        o_ref[0, pl.ds(pl.multiple_of(qi * DA_BLOCK, DA_BLOCK), DA_BLOCK), :] = jnp.concatenate(outs, axis=-1)
        return carry

    lax.fori_loop(0, n_q, finish, 0)


DA_GROUP = 2


def _diff_attention(q_t, k, v_t, bias, lam, subln_g, lambda_init):
    B, S, _ = k.shape
    n_blk = S // TOK_BLOCK
    n_q = S // DA_BLOCK
    nhd = DA_GROUP
    n_map = 2 * nhd
    wide = nhd * DA_V_DIM
    qs, ks, passes = _da_tile_passes(n_q)
    smem = pl.BlockSpec(memory_space=pltpu.SMEM)
    return pl.pallas_call(
        functools.partial(_da_kernel, out_scale=1.0 - lambda_init, passes=passes),
        grid=(B, DA_HEADS // nhd),
        in_specs=[smem, smem, smem,
                  pl.BlockSpec((1, n_blk, wide, TOK_BLOCK), lambda b, h: (b, 0, h, 0)),
                  pl.BlockSpec((1, S, wide), lambda b, h: (b, 0, h)),
                  pl.BlockSpec((1, n_blk, wide, TOK_BLOCK), lambda b, h: (b, 0, h, 0)),
                  pl.BlockSpec((nhd, 2, DA_BLOCK, DA_BLOCK), lambda b, h: (h, 0, 0, 0)),
                  pl.BlockSpec((1, DA_V_DIM), lambda b, h: (0, 0))],
        out_specs=pl.BlockSpec((1, S, wide), lambda b, h: (b, 0, h)),
        out_shape=jax.ShapeDtypeStruct((B, S, DA_WIDTH), BF16),
        scratch_shapes=[pltpu.VMEM((n_q, n_map, DA_V_DIM, DA_BLOCK), F32),
                        pltpu.VMEM((n_q, n_map, 1, DA_BLOCK), F32),
                        pltpu.VMEM((n_q, n_map, 1, DA_BLOCK), F32),
                        pltpu.VMEM((n_q, n_map, wide, DA_BLOCK), BF16),
                        pltpu.VMEM((n_map, DA_BLOCK, DA_BLOCK), F32),
                        pltpu.VMEM((n_map, 1, DA_BLOCK), F32),
                        pltpu.VMEM((2, n_map, DA_BLOCK, DA_BLOCK), BF16),
                        pltpu.VMEM((n_map, 1, DA_BLOCK), F32)],
        compiler_params=_cparams(2),
        name="diff_attention",
    )(lam, jnp.asarray(qs), jnp.asarray(ks), q_t, k, v_t, bias, subln_g.reshape(1, DA_V_DIM))


def _sb_kernel(q_ref, k_ref, v_ref, mask_ref, o_ref, acc_ref, carry_ref, qz_ref, u_ref, tot_ref, w_ref):
    qi = pl.program_id(2)
    nh = acc_ref.shape[0]
    slab = qz_ref.shape[1]
    slab_heads = slab // SB_HEAD_DIM
    for sl in range(nh // slab_heads):
        _store_row_groups(q_ref[0, 0, sl * slab:(sl + 1) * slab, :], qz_ref.at[pl.ds(sl * slab_heads, slab_heads)],
                          SB_HEAD_DIM)
    carry_ref[...] = jnp.ones(carry_ref.shape, F32)
    sub = lax.broadcasted_iota(jnp.int32, (SUBLANES, TOK_BLOCK), 0)
    rows = lambda x, g: x[g * SUBLANES:(g + 1) * SUBLANES]

    def logits(kj):
        kb = _key_block(k_ref, kj)
        return [jnp.dot(kb[:, (hh // slab_heads) * slab:(hh // slab_heads + 1) * slab], qz_ref[hh],
                        preferred_element_type=F32) for hh in range(nh)]

    def gates(zs, slot, mask=None):
        for hh in range(nh):
            t = jnp.tanh(zs[hh])
            if mask is not None:
                t = jnp.where(mask > 0.5, t, -1.0)
            u = 1.0 - t
            u_ref[slot, hh] = u
            tot = rows(u, CHUNK - 1)
            for g in reversed(range(CHUNK - 1)):
                tot = tot * rows(u, g)
            tot_ref[slot, hh] = tot * 2.0 ** -CHUNK

    def weights(slot, keep_carry=None):
        for hh in range(nh):
            suffix = tot_ref[slot, hh]
            for d in (1, 2, 4):
                shifted = pltpu.roll(suffix, SUBLANES - d, axis=0)
                suffix = jnp.where(sub + d < SUBLANES, suffix * shifted, suffix)
            above = jnp.where(sub + 1 < SUBLANES, pltpu.roll(suffix, SUBLANES - 1, axis=0), 1.0)
            carry = carry_ref[hh]
            new_carry = carry * jnp.broadcast_to(suffix[0:1], carry.shape)
            carry_ref[hh] = new_carry if keep_carry is None else new_carry * keep_carry
            run = above * carry
            w = [None] * CHUNK
            for g in reversed(range(CHUNK)):
                ug = u_ref[slot, hh, g * SUBLANES:(g + 1) * SUBLANES, :]
                w[g] = (2.0 - ug) * run
                run = run * ug
            w_ref[slot, hh] = jnp.concatenate(w, axis=0).astype(BF16)

    def weighted_values(kj, slot):
        vb = v_ref[0, kj]
        return [jnp.dot(vb[hh * SB_HEAD_DIM:(hh + 1) * SB_HEAD_DIM], w_ref[slot, hh], preferred_element_type=F32)
                for hh in range(nh)]

    def alive():
        return (jnp.max(carry_ref[...]) > 0.0).astype(jnp.int32)

    before = jnp.maximum(qi - 1, 0)
    z_diag = logits(qi)
    z_before = logits(before)
    gates(z_diag, 0, mask_ref[...])
    gates(z_before, 1)
    weights(0, keep_carry=(qi > 0).astype(F32))
    weights(1)
    pv_diag = weighted_values(qi, 0)
    pv_before = weighted_values(before, 1)
    for hh in range(nh):
        acc_ref[hh] = pv_diag[hh] + pv_before[hh]

    more = jnp.logical_and(qi >= 2, alive() > 0)
    pl.when(more)(lambda: gates(logits(jnp.maximum(qi - 2, 0)), 0))

    def left_cond(state):
        i, live = state
        return jnp.logical_and(i < qi - 1, live > 0)

    def left_body(state):
        i, _ = state
        kj = qi - 2 - i
        nxt = logits(jnp.maximum(kj - 1, 0))
        weights(0)
        pv = weighted_values(kj, 0)
        for hh in range(nh):
            acc_ref[hh] = acc_ref[hh] + pv[hh]
        gates(nxt, 0)
        return i + 1, alive()

    lax.while_loop(left_cond, left_body, (jnp.int32(0), more.astype(jnp.int32)))

    o_t =jnp.concatenate([acc_ref[hh] for hh in range(nh)], axis=0)
    o_ref[0] = o_t.T.astype(o_ref.dtype)


SB_GROUP = 8


def _sb_attention(q_t, k, v_t):
    B, S, _ = k.shape
    n_blk = S // TOK_BLOCK
    nh = SB_GROUP
    pair = nh * SB_HEAD_DIM
    return pl.pallas_call(
        _sb_kernel,
        grid=(B, SB_HEADS // nh, n_blk),
        in_specs=[pl.BlockSpec((1, 1, pair, TOK_BLOCK), lambda b, h, i: (b, i, h, 0)),
                  pl.BlockSpec((1, S, pair), lambda b, h, i: (b, 0, h)),
                  pl.BlockSpec((1, n_blk, pair, TOK_BLOCK), lambda b, h, i: (b, 0, h, 0)),
                  pl.BlockSpec((TOK_BLOCK, TOK_BLOCK), lambda b, h, i: (0, 0))],
        out_specs=pl.BlockSpec((1, TOK_BLOCK, pair), lambda b, h, i: (b, i, h)),
        out_shape=jax.ShapeDtypeStruct((B, S, SB_WIDTH), BF16),
        scratch_shapes=[pltpu.VMEM((nh, SB_HEAD_DIM, TOK_BLOCK), F32),
                        pltpu.VMEM((nh, SUBLANES, TOK_BLOCK), F32),
                        pltpu.VMEM((nh, TOK_BLOCK, TOK_BLOCK), BF16),
                        pltpu.VMEM((2, nh, TOK_BLOCK, TOK_BLOCK), F32),
                        pltpu.VMEM((2, nh, SUBLANES, TOK_BLOCK), F32),
                        pltpu.VMEM((2, nh, TOK_BLOCK, TOK_BLOCK), BF16)],
        compiler_params=_cparams(3),
        name="stick_breaking_attention",
    )(q_t, k, v_t, jnp.asarray(_strict_causal_tile()))


def _out_kernel(oda_ref, osb_ref, gda_ref, gsb_ref, x_ref, gate_ref, w_ref, fg_ref, unperm_ref, out_ref, *,
                final_norm):
    def gated(o_ref, g_ref):
        o = jnp.concatenate(
            [jnp.dot(unperm_ref[...], o_ref[0, j * TOK_BLOCK:(j + 1) * TOK_BLOCK, :], preferred_element_type=F32)
             for j in range(o_ref.shape[1] // TOK_BLOCK)], axis=0)
        g = g_ref[0].astype(F32)
        return (o * (g / (1.0 + jnp.exp(-g)))).astype(BF16)

    split = oda_ref.shape[-1]
    y = jnp.dot(gated(oda_ref, gda_ref), w_ref[0:split, :], preferred_element_type=F32)
    y = y + jnp.dot(gated(osb_ref, gsb_ref), w_ref[split:, :], preferred_element_type=F32)
    xo = x_ref[0] + gate_ref[0] * y
    if final_norm:
        xo = xo * lax.rsqrt(jnp.mean(xo * xo, axis=-1, keepdims=True) + EPS) * fg_ref[...]
    out_ref[0] = xo


def _out_project(o_da, o_sb, g_da, g_sb, x, gate, w_out, final_g, final_norm):
    B, S, D = x.shape
    half_spec = pl.BlockSpec((1, OUT_ROW_TILE, DA_WIDTH), lambda b, i: (b, i, 0))
    row_spec = pl.BlockSpec((1, OUT_ROW_TILE, D), lambda b, i: (b, i, 0))
    return pl.pallas_call(
        functools.partial(_out_kernel, final_norm=final_norm),
        grid=(B, S // OUT_ROW_TILE),
        in_specs=[half_spec] * 4 + [row_spec,
                                    pl.BlockSpec((1, 1, D), lambda b, i: (b, 0, 0)),
                                    pl.BlockSpec(w_out.shape, lambda b, i: (0, 0)),
                                    pl.BlockSpec((1, D), lambda b, i: (0, 0)),
                                    pl.BlockSpec((TOK_BLOCK, TOK_BLOCK), lambda b, i: (0, 0))],
        out_specs=row_spec,
        out_shape=jax.ShapeDtypeStruct((B, S, D), F32),
        compiler_params=_cparams(2),
        name="gate_out_proj",
    )(o_da, o_sb, g_da, g_sb, x, gate, w_out, final_g.reshape(1, D), jnp.asarray(_position_to_token().T, BF16))


def _split_w_in(w):
    sizes = (DA_WIDTH, DA_WIDTH, DA_WIDTH, DA_WIDTH, SB_WIDTH, SB_WIDTH, SB_WIDTH, SB_WIDTH)
    q_da, k_da, v_da, g_da, q_sb, k_sb, v_sb, g_sb = jnp.split(w, np.cumsum(sizes)[:-1].tolist(), axis=1)
    w_tok = jnp.concatenate([k_da, k_sb, g_da, g_sb], axis=1).astype(BF16)
    q_scale = 1.0 / math.sqrt(DA_HEAD_DIM)
    w_feat_t = jnp.concatenate([q_da * (LOG2E * q_scale), v_da, q_sb * (0.5 * q_scale), v_sb], axis=1)
    return w_tok, w_feat_t.T.astype(BF16)


def kernel(x, c, norm_g, w_ada, b_ada, w_in, lambda_q1, lambda_k1, lambda_q2, lambda_k2, subln_g, w_out,
           rel_bias, final_g):
    B, S, D = x.shape
    depth = w_in.shape[0]
    assert S % ROW_TILE == 0 and ROW_TILE % TOK_BLOCK == 0 and S % DA_BLOCK == 0
    assert S % OUT_ROW_TILE == 0 and OUT_ROW_TILE % TOK_BLOCK == 0
    assert DA_HEAD_DIM == SB_HEAD_DIM and DA_WIDTH == SB_WIDTH and DA_WIDTH + SB_WIDTH == w_out.shape[1]
    bias = _bias_tiles(rel_bias)
    for l in range(depth):
        lambda_init = _lambda_init(l)
        mod, lam = _modulation(c, w_ada[l], b_ada[l], lambda_q1[l], lambda_k1[l], lambda_q2[l], lambda_k2[l],
                               lambda_init)
        shift, scl, gate = (m.reshape(B, 1, D) for m in jnp.split(mod, 3, axis=-1))
        w_tok, w_feat_t = _split_w_in(w_in[l])
        k_da, k_sb, g_da, g_sb, q_da, v_da, q_sb, v_sb = _project(x, norm_g[l], scl, shift, w_tok, w_feat_t)
        o_da = _diff_attention(q_da, k_da, v_da, bias, lam[:, :1], subln_g[l], lambda_init)
        o_sb = _sb_attention(q_sb, k_sb, v_sb)
        x = _out_project(o_da, o_sb, g_da, g_sb, x, gate, w_out[l].astype(BF16), final_g, l == depth - 1)
    return x
```

```python
import functools
import math

import numpy as np
import jax
import jax.numpy as jnp
from jax import lax
from jax.experimental import pallas as pl
from jax.experimental.pallas import tpu as pltpu

DA_HEADS = 4
DA_HEAD_DIM = 64
DA_V_DIM = 2 * DA_HEAD_DIM
DA_WIDTH = DA_HEADS * DA_V_DIM
SB_HEADS = 8
SB_HEAD_DIM = 64
SB_WIDTH = SB_HEADS * SB_HEAD_DIM
N_BUCKETS = 32
MAX_DISTANCE = 128
EPS = 1e-6
NEG_INF = -1e30

SUBLANES = 8
LANES = 128
TOK_BLOCK = 256
CHUNK = TOK_BLOCK // SUBLANES
DA_BLOCK = 2 * TOK_BLOCK
LOG2E = math.log2(math.e)
DIAG_TILE, NEAR_TILE = 0, 1
ROW_TILE = 512
OUT_ROW_TILE = 1024
VMEM_LIMIT = 56 * 1024 * 1024

F32 = jnp.float32
BF16 = jnp.bfloat16


def _lambda_init(layer_idx):
    return 0.8 - 0.6 * math.exp(-0.3 * layer_idx)


def _block_token(pos):
    return (pos % SUBLANES) * CHUNK + pos // SUBLANES


def _bucket_tiles():
    pos = np.arange(DA_BLOCK)
    tok = (pos // TOK_BLOCK) * TOK_BLOCK + _block_token(pos % TOK_BLOCK)
    rel0 = tok[None, :] - tok[:, None]
    rel = np.stack([np.maximum(rel0, 0), rel0 + DA_BLOCK]).astype(np.int32)
    max_exact = N_BUCKETS // 2
    nf = np.maximum(rel, 1).astype(np.float32)
    large = max_exact + (np.log(nf / np.float32(max_exact)) / np.float32(math.log(MAX_DISTANCE / max_exact))
                         * np.float32(N_BUCKETS - max_exact)).astype(np.int32)
    large = np.minimum(large, N_BUCKETS - 1)
    bucket = np.where(rel < max_exact, rel, large).astype(np.int32)
    bucket[0] = np.where(rel0 < 0, N_BUCKETS, bucket[0])
    return bucket


def _position_to_token():
    pos = np.arange(TOK_BLOCK)
    m = np.zeros((TOK_BLOCK, TOK_BLOCK), np.float32)
    m[pos, _block_token(pos)] = 1.0
    return m


def _strict_causal_tile():
    tok = _block_token(np.arange(TOK_BLOCK))
    return (tok[:, None] < tok[None, :]).astype(np.float32)


def _value_column_scale(n):
    g = (np.arange(n) % TOK_BLOCK) // SUBLANES
    return (2.0 ** -(CHUNK - g)).astype(np.float32).reshape(1, n)


def _cparams(n_axes):
    return pltpu.CompilerParams(dimension_semantics=("arbitrary",) * n_axes, vmem_limit_bytes=VMEM_LIMIT)


def _mod_kernel(c_ref, w_ref, b_ref, lq1_ref, lk1_ref, lq2_ref, lk2_ref, mod_ref, lam_ref, *, lambda_init):
    c = c_ref[...]
    silu_c = c / (1.0 + jnp.exp(-c))
    mod_ref[...] = jnp.dot(silu_c, w_ref[...], preferred_element_type=F32,
                           precision=lax.Precision.HIGHEST) + b_ref[...]
    s1 = jnp.sum(lq1_ref[...] * lk1_ref[...], axis=-1, keepdims=True)
    s2 = jnp.sum(lq2_ref[...] * lk2_ref[...], axis=-1, keepdims=True)
    lam_ref[...] = jnp.broadcast_to(jnp.exp(s1) - jnp.exp(s2) + lambda_init, lam_ref.shape)


def _modulation(c, w_ada, b_ada, lq1, lk1, lq2, lk2, lambda_init):
    B, D = c.shape
    n_out = w_ada.shape[1]
    col = 512
    vec = lambda: pl.BlockSpec((1, DA_HEAD_DIM), lambda j: (0, 0))
    return pl.pallas_call(
        functools.partial(_mod_kernel, lambda_init=lambda_init),
        grid=(n_out // col,),
        in_specs=[pl.BlockSpec((B, D), lambda j: (0, 0)),
                  pl.BlockSpec((D, col), lambda j: (0, j)),
                  pl.BlockSpec((1, col), lambda j: (0, j)),
                  vec(), vec(), vec(), vec()],
        out_specs=[pl.BlockSpec((B, col), lambda j: (0, j)),
                   pl.BlockSpec((1, LANES), lambda j: (0, 0))],
        out_shape=[jax.ShapeDtypeStruct((B, n_out), F32), jax.ShapeDtypeStruct((1, LANES), F32)],
        compiler_params=_cparams(1),
        name="adaln_mod",
    )(c, w_ada, b_ada.reshape(1, n_out), lq1.reshape(1, -1), lk1.reshape(1, -1), lq2.reshape(1, -1),
      lk2.reshape(1, -1))


def _bias_kernel(rb_ref, bucket_ref, out_ref):
    h = pl.program_id(0)
    far = rb_ref[N_BUCKETS - 1, h]
    for t in range(2):
        bucket = bucket_ref[t]
        tile = jnp.zeros(bucket.shape, F32)
        for b in range(N_BUCKETS - 1):
            tile = jnp.where(bucket == b, (rb_ref[b, h] - far) * LOG2E, tile)
        out_ref[0, t] = jnp.where(bucket == N_BUCKETS, NEG_INF, tile)


def _bias_tiles(rel_bias):
    buckets = jnp.asarray(_bucket_tiles())
    return pl.pallas_call(
        _bias_kernel,
        grid=(DA_HEADS,),
        in_specs=[pl.BlockSpec(memory_space=pltpu.SMEM),
                  pl.BlockSpec((2, DA_BLOCK, DA_BLOCK), lambda h: (0, 0, 0))],
        out_specs=pl.BlockSpec((1, 2, DA_BLOCK, DA_BLOCK), lambda h: (h, 0, 0, 0)),
        out_shape=jax.ShapeDtypeStruct((DA_HEADS, 2, DA_BLOCK, DA_BLOCK), F32),
        compiler_params=_cparams(1),
        name="rel_bias_tiles",
    )(rel_bias, buckets)


def _proj_kernel(x_ref, g_ref, scl_ref, shift_ref, wt_ref, wf_ref, vscale_ref, perm_ref,
                 kda_ref, ksb_ref, gda_ref, gsb_ref, qda_ref, vda_ref, qsb_ref, vsb_ref):
    x = x_ref[0]
    ms = jnp.mean(x * x, axis=-1, keepdims=True)
    xn = x * lax.rsqrt(ms + EPS) * g_ref[...]
    h = (xn * (1.0 + scl_ref[0]) + shift_ref[0]).astype(BF16)
    hp = jnp.concatenate(
        [jnp.dot(perm_ref[...], h[j * TOK_BLOCK:(j + 1) * TOK_BLOCK], preferred_element_type=F32).astype(BF16)
         for j in range(h.shape[0] // TOK_BLOCK)], axis=0)
    width = kda_ref.shape[-1]
    for i, (o_ref, src) in enumerate(((kda_ref, hp), (ksb_ref, hp), (gda_ref, h), (gsb_ref, h))):
        o_ref[0] = jnp.dot(src, wt_ref[:, i * width:(i + 1) * width], preferred_element_type=F32).astype(BF16)
    for i, o_ref in enumerate((qda_ref, vda_ref, qsb_ref, vsb_ref)):
        r = lax.dot_general(wf_ref[i * width:(i + 1) * width, :], hp, (((1,), (1,)), ((), ())),
                            preferred_element_type=F32)
        if o_ref is vsb_ref:
            r = r * vscale_ref[...]
        r = r.astype(BF16)
        for t in range(o_ref.shape[1]):
            o_ref[0, t] = r[:, t * TOK_BLOCK:(t + 1) * TOK_BLOCK]


def _project(x, norm_g, scl, shift, w_tok, w_feat_t):
    B, S, D = x.shape
    n_blk = S // TOK_BLOCK
    blk_per_tile = ROW_TILE // TOK_BLOCK
    width = DA_WIDTH
    tok_spec = pl.BlockSpec((1, ROW_TILE, width), lambda b, i: (b, i, 0))
    feat_spec = pl.BlockSpec((1, blk_per_tile, width, TOK_BLOCK), lambda b, i: (b, i, 0, 0))
    tok_shape = jax.ShapeDtypeStruct((B, S, width), BF16)
    feat_shape = jax.ShapeDtypeStruct((B, n_blk, width, TOK_BLOCK), BF16)
    mod_spec = pl.BlockSpec((1, 1, D), lambda b, i: (b, 0, 0))
    return pl.pallas_call(
        _proj_kernel,
        grid=(B, S // ROW_TILE),
        in_specs=[pl.BlockSpec((1, ROW_TILE, D), lambda b, i: (b, i, 0)),
                  pl.BlockSpec((1, D), lambda b, i: (0, 0)),
                  mod_spec, mod_spec,
                  pl.BlockSpec(w_tok.shape, lambda b, i: (0, 0)),
                  pl.BlockSpec(w_feat_t.shape, lambda b, i: (0, 0)),
                  pl.BlockSpec((1, ROW_TILE), lambda b, i: (0, 0)),
                  pl.BlockSpec((TOK_BLOCK, TOK_BLOCK), lambda b, i: (0, 0))],
        out_specs=[tok_spec] * 4 + [feat_spec] * 4,
        out_shape=[tok_shape] * 4 + [feat_shape] * 4,
        compiler_params=_cparams(2),
        name="norm_in_proj",
    )(x, norm_g.reshape(1, D), scl, shift, w_tok, w_feat_t, jnp.asarray(_value_column_scale(ROW_TILE)),
      jnp.asarray(_position_to_token(), BF16))


def _store_row_groups(q_t, qz_ref, group):
    row = lax.broadcasted_iota(jnp.int32, q_t.shape, 0)
    for i in range(qz_ref.shape[0]):
        inside = (row >= i * group) & (row < (i + 1) * group)
        qz_ref[i] = jnp.where(inside, q_t, jnp.zeros_like(q_t))


def _key_block(k_ref, kj, rows=TOK_BLOCK):
    return k_ref[0, pl.ds(pl.multiple_of(kj * rows, rows), rows), :]


def _da_tile_passes(n_q):
    plain = [(qi, kj) for qi in range(n_q) for kj in range(qi - 1)]
    near = [(qi, qi - 1) for qi in range(1, n_q)]
    diag = [(qi, qi) for qi in range(n_q)]
    passes, start = [], 0
    for tiles, kind in ((plain, None), (near, NEAR_TILE), (diag, DIAG_TILE)):
        passes.append((start, len(tiles), kind))
        start += len(tiles)
    order = np.asarray(plain + near + diag, np.int32)
    return order[:, 0], order[:, 1], tuple(passes)


def _da_kernel(lam_ref, qs_ref, ks_ref, q_ref, k_ref, v_ref, bias_ref, g_ref, o_ref,
               acc_ref, m_ref, l_ref, qz_ref, s_ref, smax_ref, p_ref, a_ref, *, out_scale, passes):
    n_q, n_map = acc_ref.shape[0], acc_ref.shape[1]
    n_sub = DA_BLOCK // TOK_BLOCK
    acc_ref[...] = jnp.zeros(acc_ref.shape, F32)
    m_ref[...] = jnp.full(m_ref.shape, NEG_INF, F32)
    l_ref[...] = jnp.zeros(l_ref.shape, F32)

    for qi in range(n_q):
        q_t = jnp.concatenate([q_ref[0, n_sub * qi + j] for j in range(n_sub)], axis=-1)
        _store_row_groups(q_t, qz_ref.at[qi], DA_HEAD_DIM)

    def scores(qi, kj, maps=None):
        kb = _key_block(k_ref, kj, DA_BLOCK)
        return [jnp.dot(kb, qz_ref[qi, mp], preferred_element_type=F32)
                for mp in (range(n_map) if maps is None else maps)]

    def put_scores(ss, kind):
        for mp in range(n_map):
            s = ss[mp]
            if kind is not None:
                s = s + bias_ref[mp // 2, kind]
            s_ref[mp] = s
            smax_ref[mp] = jnp.max(s, axis=0, keepdims=True)

    def softmax_step(qi, slot):
        for mp in range(n_map):
            m_old = m_ref[qi, mp]
            m_new = jnp.maximum(m_old, smax_ref[mp])
            alpha = jnp.exp2(m_old - m_new)
            p = jnp.exp2(s_ref[mp] - m_new)
            l_ref[qi, mp] = alpha * l_ref[qi, mp] + jnp.sum(p, axis=0, keepdims=True)
            m_ref[qi, mp] = m_new
            a_ref[mp] = alpha
            p_ref[slot, mp] = p.astype(BF16)

    def weighted_values(kj, slot, maps=None):
        out = []
        for mp in (range(n_map) if maps is None else maps):
            pv = None
            for j in range(n_sub):
                vh = v_ref[0, n_sub * kj + j, (mp // 2) * DA_V_DIM:(mp // 2 + 1) * DA_V_DIM, :]
                part = jnp.dot(vh, p_ref[slot, mp, j * TOK_BLOCK:(j + 1) * TOK_BLOCK, :], preferred_element_type=F32)
                pv = part if pv is None else pv + part
            out.append(pv)
        return out

    def add_weighted(pv, qi):
        for mp in range(n_map):
            acc_ref[qi, mp] = a_ref[mp] * acc_ref[qi, mp] + pv[mp]

    for start, count, kind in passes:
        if count == 0:
            continue
        last = start + count - 1
        put_scores(scores(qs_ref[start], ks_ref[start]), kind)
        second = min(start + 1, last)
        next_scores = scores(qs_ref[second], ks_ref[second])
        softmax_step(qs_ref[start], start % 2)
        put_scores(next_scores, kind)

        def one_step(cur, cur_slot, last=last, kind=kind):
            prev = cur - 1
            nxt = jnp.minimum(cur + 1, last)
            pv, next_scores = [], []
            for mp in range(n_map):
                pv += weighted_values(ks_ref[prev], 1 - cur_slot, (mp,))
                next_scores += scores(qs_ref[nxt], ks_ref[nxt], (mp,))
            add_weighted(pv, qs_ref[prev])
            softmax_step(qs_ref[cur], cur_slot)
            put_scores(next_scores, kind)

        rest = count - 1
        first_slot = (start + 1) % 2

        def two_steps(j, carry, start=start, first_slot=first_slot, one_step=one_step):
            cur = start + 1 + 2 * j
            one_step(cur, first_slot)
            one_step(cur + 1, 1 - first_slot)
            return carry

        lax.fori_loop(0, rest // 2, two_steps, 0)
        if rest % 2:
            one_step(last, last % 2)
        add_weighted(weighted_values(ks_ref[last], last % 2), qs_ref[last])

    lam = lam_ref[0, 0]

    def finish(qi, carry):
        outs = []
        for hd in range(n_map // 2):
            o_t = (acc_ref[qi, 2 * hd] / l_ref[qi, 2 * hd]
                   - lam * (acc_ref[qi, 2 * hd + 1] / l_ref[qi, 2 * hd + 1]))
            o = o_t.T
            o = o * lax.rsqrt(jnp.mean(o * o, axis=-1, keepdims=True) + EPS) * g_ref[...]
            outs.append((o * out_scale).astype(o_ref.dtype))
        o_ref[0, pl.ds(pl.multiple_of(qi * DA_BLOCK, DA_BLOCK), DA_BLOCK), :] = jnp.concatenate(outs, axis=-1)
        return carry

    lax.fori_loop(0, n_q, finish, 0)


DA_GROUP = 2


def _diff_attention(q_t, k, v_t, bias, lam, subln_g, lambda_init):
    B, S, _ = k.shape
    n_blk = S // TOK_BLOCK
    n_q = S // DA_BLOCK
    nhd = DA_GROUP
    n_map = 2 * nhd
    wide = nhd * DA_V_DIM
    qs, ks, passes = _da_tile_passes(n_q)
    smem = pl.BlockSpec(memory_space=pltpu.SMEM)
    return pl.pallas_call(
        functools.partial(_da_kernel, out_scale=1.0 - lambda_init, passes=passes),
        grid=(B, DA_HEADS // nhd),
        in_specs=[smem, smem, smem,
                  pl.BlockSpec((1, n_blk, wide, TOK_BLOCK), lambda b, h: (b, 0, h, 0)),
                  pl.BlockSpec((1, S, wide), lambda b, h: (b, 0, h)),
                  pl.BlockSpec((1, n_blk, wide, TOK_BLOCK), lambda b, h: (b, 0, h, 0)),
                  pl.BlockSpec((nhd, 2, DA_BLOCK, DA_BLOCK), lambda b, h: (h, 0, 0, 0)),
                  pl.BlockSpec((1, DA_V_DIM), lambda b, h: (0, 0))],
        out_specs=pl.BlockSpec((1, S, wide), lambda b, h: (b, 0, h)),
        out_shape=jax.ShapeDtypeStruct((B, S, DA_WIDTH), BF16),
        scratch_shapes=[pltpu.VMEM((n_q, n_map, DA_V_DIM, DA_BLOCK), F32),
                        pltpu.VMEM((n_q, n_map, 1, DA_BLOCK), F32),
                        pltpu.VMEM((n_q, n_map, 1, DA_BLOCK), F32),
                        pltpu.VMEM((n_q, n_map, wide, DA_BLOCK), BF16),
                        pltpu.VMEM((n_map, DA_BLOCK, DA_BLOCK), F32),
                        pltpu.VMEM((n_map, 1, DA_BLOCK), F32),
                        pltpu.VMEM((2, n_map, DA_BLOCK, DA_BLOCK), BF16),
                        pltpu.VMEM((n_map, 1, DA_BLOCK), F32)],
        compiler_params=_cparams(2),
        name="diff_attention",
    )(lam, jnp.asarray(qs), jnp.asarray(ks), q_t, k, v_t, bias, subln_g.reshape(1, DA_V_DIM))


def _sb_kernel(q_ref, k_ref, v_ref, mask_ref, o_ref, acc_ref, carry_ref, qz_ref, u_ref, tot_ref, w_ref):
    qi = pl.program_id(2)
    nh = acc_ref.shape[0]
    slab = qz_ref.shape[1]
    slab_heads = slab // SB_HEAD_DIM
    for sl in range(nh // slab_heads):
        _store_row_groups(q_ref[0, 0, sl * slab:(sl + 1) * slab, :], qz_ref.at[pl.ds(sl * slab_heads, slab_heads)],
                          SB_HEAD_DIM)
    carry_ref[...] = jnp.ones(carry_ref.shape, F32)
    sub = lax.broadcasted_iota(jnp.int32, (SUBLANES, TOK_BLOCK), 0)
    rows = lambda x, g: x[g * SUBLANES:(g + 1) * SUBLANES]

    def logits(kj):
        kb = _key_block(k_ref, kj)
        return [jnp.dot(kb[:, (hh // slab_heads) * slab:(hh // slab_heads + 1) * slab], qz_ref[hh],
                        preferred_element_type=F32) for hh in range(nh)]

    def gates(zs, slot, mask=None):
        for hh in range(nh):
            t = jnp.tanh(zs[hh])
            if mask is not None:
                t = jnp.where(mask > 0.5, t, -1.0)
            u = 1.0 - t
            u_ref[slot, hh] = u
            tot = rows(u, CHUNK - 1)
            for g in reversed(range(CHUNK - 1)):
                tot = tot * rows(u, g)
            tot_ref[slot, hh] = tot * 2.0 ** -CHUNK

    def weights(slot, keep_carry=None):
        for hh in range(nh):
            suffix = tot_ref[slot, hh]
            for d in (1, 2, 4):
                shifted = pltpu.roll(suffix, SUBLANES - d, axis=0)
                suffix = jnp.where(sub + d < SUBLANES, suffix * shifted, suffix)
            above = jnp.where(sub + 1 < SUBLANES, pltpu.roll(suffix, SUBLANES - 1, axis=0), 1.0)
            carry = carry_ref[hh]
            new_carry = carry * jnp.broadcast_to(suffix[0:1], carry.shape)
            carry_ref[hh] = new_carry if keep_carry is None else new_carry * keep_carry
            run = above * carry
            w = [None] * CHUNK
            for g in reversed(range(CHUNK)):
                ug = u_ref[slot, hh, g * SUBLANES:(g + 1) * SUBLANES, :]
                w[g] = (2.0 - ug) * run
                run = run * ug
            w_ref[slot, hh] = jnp.concatenate(w, axis=0).astype(BF16)

    def weighted_values(kj, slot):
        vb = v_ref[0, kj]
        return [jnp.dot(vb[hh * SB_HEAD_DIM:(hh + 1) * SB_HEAD_DIM], w_ref[slot, hh], preferred_element_type=F32)
                for hh in range(nh)]

    def alive():
        return (jnp.max(carry_ref[...]) > 0.0).astype(jnp.int32)

    before = jnp.maximum(qi - 1, 0)
    z_diag = logits(qi)
    z_before = logits(before)
    gates(z_diag, 0, mask_ref[...])
    gates(z_before, 1)
    weights(0, keep_carry=(qi > 0).astype(F32))
    weights(1)
    pv_diag = weighted_values(qi, 0)
    pv_before = weighted_values(before, 1)
    for hh in range(nh):
        acc_ref[hh] = pv_diag[hh] + pv_before[hh]

    more = jnp.logical_and(qi >= 2, alive() > 0)
    pl.when(more)(lambda: gates(logits(jnp.maximum(qi - 2, 0)), 0))

    def left_cond(state):
        i, live = state
        return jnp.logical_and(i < qi - 1, live > 0)

    def left_body(state):
        i, _ = state
        kj = qi - 2 - i
        nxt = logits(jnp.maximum(kj - 1, 0))
        weights(0)
        pv = weighted_values(kj, 0)
        for hh in range(nh):
            acc_ref[hh] = acc_ref[hh] + pv[hh]
        gates(nxt, 0)
        return i + 1, alive()

    lax.while_loop(left_cond, left_body, (jnp.int32(0), more.astype(jnp.int32)))

    o_t =jnp.concatenate([acc_ref[hh] for hh in range(nh)], axis=0)
    o_ref[0] = o_t.T.astype(o_ref.dtype)


SB_GROUP = 8


def _sb_attention(q_t, k, v_t):
    B, S, _ = k.shape
    n_blk = S // TOK_BLOCK
    nh = SB_GROUP
    pair = nh * SB_HEAD_DIM
    return pl.pallas_call(
        _sb_kernel,
        grid=(B, SB_HEADS // nh, n_blk),
        in_specs=[pl.BlockSpec((1, 1, pair, TOK_BLOCK), lambda b, h, i: (b, i, h, 0)),
                  pl.BlockSpec((1, S, pair), lambda b, h, i: (b, 0, h)),
                  pl.BlockSpec((1, n_blk, pair, TOK_BLOCK), lambda b, h, i: (b, 0, h, 0)),
                  pl.BlockSpec((TOK_BLOCK, TOK_BLOCK), lambda b, h, i: (0, 0))],
        out_specs=pl.BlockSpec((1, TOK_BLOCK, pair), lambda b, h, i: (b, i, h)),
        out_shape=jax.ShapeDtypeStruct((B, S, SB_WIDTH), BF16),
        scratch_shapes=[pltpu.VMEM((nh, SB_HEAD_DIM, TOK_BLOCK), F32),
                        pltpu.VMEM((nh, SUBLANES, TOK_BLOCK), F32),
                        pltpu.VMEM((nh, TOK_BLOCK, TOK_BLOCK), BF16),
                        pltpu.VMEM((2, nh, TOK_BLOCK, TOK_BLOCK), F32),
                        pltpu.VMEM((2, nh, SUBLANES, TOK_BLOCK), F32),
                        pltpu.VMEM((2, nh, TOK_BLOCK, TOK_BLOCK), BF16)],
        compiler_params=_cparams(3),
        name="stick_breaking_attention",
    )(q_t, k, v_t, jnp.asarray(_strict_causal_tile()))


def _out_kernel(oda_ref, osb_ref, gda_ref, gsb_ref, x_ref, gate_ref, w_ref, fg_ref, unperm_ref, out_ref, *,
                final_norm):
    def gated(o_ref, g_ref):
        o = jnp.concatenate(
            [jnp.dot(unperm_ref[...], o_ref[0, j * TOK_BLOCK:(j + 1) * TOK_BLOCK, :], preferred_element_type=F32)
             for j in range(o_ref.shape[1] // TOK_BLOCK)], axis=0)
        g = g_ref[0].astype(F32)
        return (o * (g / (1.0 + jnp.exp(-g)))).astype(BF16)

    split = oda_ref.shape[-1]
    y = jnp.dot(gated(oda_ref, gda_ref), w_ref[0:split, :], preferred_element_type=F32)
    y = y + jnp.dot(gated(osb_ref, gsb_ref), w_ref[split:, :], preferred_element_type=F32)
    xo = x_ref[0] + gate_ref[0] * y
    if final_norm:
        xo = xo * lax.rsqrt(jnp.mean(xo * xo, axis=-1, keepdims=True) + EPS) * fg_ref[...]
    out_ref[0] = xo


def _out_project(o_da, o_sb, g_da, g_sb, x, gate, w_out, final_g, final_norm):
    B, S, D = x.shape
    half_spec = pl.BlockSpec((1, OUT_ROW_TILE, DA_WIDTH), lambda b, i: (b, i, 0))
    row_spec = pl.BlockSpec((1, OUT_ROW_TILE, D), lambda b, i: (b, i, 0))
    return pl.pallas_call(
        functools.partial(_out_kernel, final_norm=final_norm),
        grid=(B, S // OUT_ROW_TILE),
        in_specs=[half_spec] * 4 + [row_spec,
                                    pl.BlockSpec((1, 1, D), lambda b, i: (b, 0, 0)),
                                    pl.BlockSpec(w_out.shape, lambda b, i: (0, 0)),
                                    pl.BlockSpec((1, D), lambda b, i: (0, 0)),
                                    pl.BlockSpec((TOK_BLOCK, TOK_BLOCK), lambda b, i: (0, 0))],
        out_specs=row_spec,
        out_shape=jax.ShapeDtypeStruct((B, S, D), F32),
        compiler_params=_cparams(2),
        name="gate_out_proj",
    )(o_da, o_sb, g_da, g_sb, x, gate, w_out, final_g.reshape(1, D), jnp.asarray(_position_to_token().T, BF16))


def _split_w_in(w):
    sizes = (DA_WIDTH, DA_WIDTH, DA_WIDTH, DA_WIDTH, SB_WIDTH, SB_WIDTH, SB_WIDTH, SB_WIDTH)
    q_da, k_da, v_da, g_da, q_sb, k_sb, v_sb, g_sb = jnp.split(w, np.cumsum(sizes)[:-1].tolist(), axis=1)
    w_tok = jnp.concatenate([k_da, k_sb, g_da, g_sb], axis=1).astype(BF16)
    q_scale = 1.0 / math.sqrt(DA_HEAD_DIM)
    w_feat_t = jnp.concatenate([q_da * (LOG2E * q_scale), v_da, q_sb * (0.5 * q_scale), v_sb], axis=1)
    return w_tok, w_feat_t.T.astype(BF16)


def kernel(x, c, norm_g, w_ada, b_ada, w_in, lambda_q1, lambda_k1, lambda_q2, lambda_k2, subln_g, w_out,
           rel_bias, final_g):
    B, S, D = x.shape
    depth = w_in.shape[0]
    assert S % ROW_TILE == 0 and ROW_TILE % TOK_BLOCK == 0 and S % DA_BLOCK == 0
    assert S % OUT_ROW_TILE == 0 and OUT_ROW_TILE % TOK_BLOCK == 0
    assert DA_HEAD_DIM == SB_HEAD_DIM and DA_WIDTH == SB_WIDTH and DA_WIDTH + SB_WIDTH == w_out.shape[1]
    bias = _bias_tiles(rel_bias)
    for l in range(depth):
        lambda_init = _lambda_init(l)
        mod, lam = _modulation(c, w_ada[l], b_ada[l], lambda_q1[l], lambda_k1[l], lambda_q2[l], lambda_k2[l],
                               lambda_init)
        shift, scl, gate = (m.reshape(B, 1, D) for m in jnp.split(mod, 3, axis=-1))
        w_tok, w_feat_t = _split_w_in(w_in[l])
        k_da, k_sb, g_da, g_sb, q_da, v_da, q_sb, v_sb = _project(x, norm_g[l], scl, shift, w_tok, w_feat_t)
        o_da = _diff_attention(q_da, k_da, v_da, bias, lam[:, :1], subln_g[l], lambda_init)
        o_sb = _sb_attention(q_sb, k_sb, v_sb)
        x = _out_project(o_da, o_sb, g_da, g_sb, x, gate, w_out[l].astype(BF16), final_g, l == depth - 1)
    return x
```

```python
import functools
import math

import numpy as np
import jax
import jax.numpy as jnp
from jax import lax
from jax.experimental import pallas as pl
from jax.experimental.pallas import tpu as pltpu

DA_HEADS = 4
DA_HEAD_DIM = 64
DA_V_DIM = 2 * DA_HEAD_DIM
DA_WIDTH = DA_HEADS * DA_V_DIM
SB_HEADS = 8
SB_HEAD_DIM = 64
SB_WIDTH = SB_HEADS * SB_HEAD_DIM
N_BUCKETS = 32
MAX_DISTANCE = 128
EPS = 1e-6
NEG_INF = -1e30

SUBLANES = 8
LANES = 128
TOK_BLOCK = 256
CHUNK = TOK_BLOCK // SUBLANES
DA_BLOCK = 2 * TOK_BLOCK
LOG2E = math.log2(math.e)
DIAG_TILE, NEAR_TILE = 0, 1
ROW_TILE = 512
OUT_ROW_TILE = 1024
VMEM_LIMIT = 56 * 1024 * 1024

F32 = jnp.float32
BF16 = jnp.bfloat16


def _lambda_init(layer_idx):
    return 0.8 - 0.6 * math.exp(-0.3 * layer_idx)


def _block_token(pos):
    return (pos % SUBLANES) * CHUNK + pos // SUBLANES


def _bucket_tiles():
    pos = np.arange(DA_BLOCK)
    key_tok = (pos // TOK_BLOCK) * TOK_BLOCK + _block_token(pos % TOK_BLOCK)
    rel0 = pos[None, :] - key_tok[:, None]
    rel = np.stack([np.maximum(rel0, 0), rel0 + DA_BLOCK]).astype(np.int32)
    max_exact = N_BUCKETS // 2
    nf = np.maximum(rel, 1).astype(np.float32)
    large = max_exact + (np.log(nf / np.float32(max_exact)) / np.float32(math.log(MAX_DISTANCE / max_exact))
                         * np.float32(N_BUCKETS - max_exact)).astype(np.int32)
    large = np.minimum(large, N_BUCKETS - 1)
    bucket = np.where(rel < max_exact, rel, large).astype(np.int32)
    bucket[0] = np.where(rel0 < 0, N_BUCKETS, bucket[0])
    return bucket


def _position_to_token():
    pos = np.arange(TOK_BLOCK)
    m = np.zeros((TOK_BLOCK, TOK_BLOCK), np.float32)
    m[pos, _block_token(pos)] = 1.0
    return m


def _strict_causal_tile():
    pos = np.arange(TOK_BLOCK)
    return (_block_token(pos)[:, None] < pos[None, :]).astype(np.float32)


def _value_column_scale(n):
    g = (np.arange(n) % TOK_BLOCK) // SUBLANES
    return (2.0 ** -(CHUNK - g)).astype(np.float32).reshape(1, n)


def _cparams(n_axes):
    return pltpu.CompilerParams(dimension_semantics=("arbitrary",) * n_axes, vmem_limit_bytes=VMEM_LIMIT)


def _mod_kernel(c_ref, w_ref, b_ref, lq1_ref, lk1_ref, lq2_ref, lk2_ref, mod_ref, lam_ref, *, lambda_init):
    c = c_ref[...]
    silu_c = c / (1.0 + jnp.exp(-c))
    mod_ref[...] = jnp.dot(silu_c, w_ref[...], preferred_element_type=F32,
                           precision=lax.Precision.HIGHEST) + b_ref[...]
    s1 = jnp.sum(lq1_ref[...] * lk1_ref[...], axis=-1, keepdims=True)
    s2 = jnp.sum(lq2_ref[...] * lk2_ref[...], axis=-1, keepdims=True)
    lam_ref[...] = jnp.broadcast_to(jnp.exp(s1) - jnp.exp(s2) + lambda_init, lam_ref.shape)


def _modulation(c, w_ada, b_ada, lq1, lk1, lq2, lk2, lambda_init):
    B, D = c.shape
    n_out = w_ada.shape[1]
    col = 512
    vec = lambda: pl.BlockSpec((1, DA_HEAD_DIM), lambda j: (0, 0))
    return pl.pallas_call(
        functools.partial(_mod_kernel, lambda_init=lambda_init),
        grid=(n_out // col,),
        in_specs=[pl.BlockSpec((B, D), lambda j: (0, 0)),
                  pl.BlockSpec((D, col), lambda j: (0, j)),
                  pl.BlockSpec((1, col), lambda j: (0, j)),
                  vec(), vec(), vec(), vec()],
        out_specs=[pl.BlockSpec((B, col), lambda j: (0, j)),
                   pl.BlockSpec((1, LANES), lambda j: (0, 0))],
        out_shape=[jax.ShapeDtypeStruct((B, n_out), F32), jax.ShapeDtypeStruct((1, LANES), F32)],
        compiler_params=_cparams(1),
        name="adaln_mod",
    )(c, w_ada, b_ada.reshape(1, n_out), lq1.reshape(1, -1), lk1.reshape(1, -1), lq2.reshape(1, -1),
      lk2.reshape(1, -1))


def _bias_kernel(rb_ref, bucket_ref, out_ref):
    h = pl.program_id(0)
    far = rb_ref[N_BUCKETS - 1, h]
    for t in range(2):
        bucket = bucket_ref[t]
        tile = jnp.zeros(bucket.shape, F32)
        for b in range(N_BUCKETS - 1):
            tile = jnp.where(bucket == b, (rb_ref[b, h] - far) * LOG2E, tile)
        out_ref[0, t] = jnp.where(bucket == N_BUCKETS, NEG_INF, tile)


def _bias_tiles(rel_bias):
    buckets = jnp.asarray(_bucket_tiles())
    return pl.pallas_call(
        _bias_kernel,
        grid=(DA_HEADS,),
        in_specs=[pl.BlockSpec(memory_space=pltpu.SMEM),
                  pl.BlockSpec((2, DA_BLOCK, DA_BLOCK), lambda h: (0, 0, 0))],
        out_specs=pl.BlockSpec((1, 2, DA_BLOCK, DA_BLOCK), lambda h: (h, 0, 0, 0)),
        out_shape=jax.ShapeDtypeStruct((DA_HEADS, 2, DA_BLOCK, DA_BLOCK), F32),
        compiler_params=_cparams(1),
        name="rel_bias_tiles",
    )(rel_bias, buckets)


def _proj_kernel(x_ref, g_ref, scl_ref, shift_ref, wt_ref, wf_ref, vscale_ref, perm_ref,
                 kda_ref, ksb_ref, gda_ref, gsb_ref, qda_ref, vda_ref, qsb_ref, vsb_ref):
    x = x_ref[0]
    ms = jnp.mean(x * x, axis=-1, keepdims=True)
    xn = x * lax.rsqrt(ms + EPS) * g_ref[...]
    h = (xn * (1.0 + scl_ref[0]) + shift_ref[0]).astype(BF16)
    hp = jnp.concatenate(
        [jnp.dot(perm_ref[...], h[j * TOK_BLOCK:(j + 1) * TOK_BLOCK], preferred_element_type=F32).astype(BF16)
         for j in range(h.shape[0] // TOK_BLOCK)], axis=0)
    width = kda_ref.shape[-1]
    for i, (o_ref, src) in enumerate(((kda_ref, hp), (ksb_ref, hp), (gda_ref, h), (gsb_ref, h))):
        o_ref[0] = jnp.dot(src, wt_ref[:, i * width:(i + 1) * width], preferred_element_type=F32).astype(BF16)
    for i, (o_ref, src) in enumerate(((qda_ref, h), (vda_ref, hp), (qsb_ref, h), (vsb_ref, hp))):
        r = lax.dot_general(wf_ref[i * width:(i + 1) * width, :], src, (((1,), (1,)), ((), ())),
                            preferred_element_type=F32)
        if o_ref is vsb_ref:
            r = r * vscale_ref[...]
        r = r.astype(BF16)
        for t in range(o_ref.shape[1]):
            o_ref[0, t] = r[:, t * TOK_BLOCK:(t + 1) * TOK_BLOCK]


def _project(x, norm_g, scl, shift, w_tok, w_feat_t):
    B, S, D = x.shape
    n_blk = S // TOK_BLOCK
    blk_per_tile = ROW_TILE // TOK_BLOCK
    width = DA_WIDTH
    tok_spec = pl.BlockSpec((1, ROW_TILE, width), lambda b, i: (b, i, 0))
    feat_spec = pl.BlockSpec((1, blk_per_tile, width, TOK_BLOCK), lambda b, i: (b, i, 0, 0))
    tok_shape = jax.ShapeDtypeStruct((B, S, width), BF16)
    feat_shape = jax.ShapeDtypeStruct((B, n_blk, width, TOK_BLOCK), BF16)
    mod_spec = pl.BlockSpec((1, 1, D), lambda b, i: (b, 0, 0))
    return pl.pallas_call(
        _proj_kernel,
        grid=(B, S // ROW_TILE),
        in_specs=[pl.BlockSpec((1, ROW_TILE, D), lambda b, i: (b, i, 0)),
                  pl.BlockSpec((1, D), lambda b, i: (0, 0)),
                  mod_spec, mod_spec,
                  pl.BlockSpec(w_tok.shape, lambda b, i: (0, 0)),
                  pl.BlockSpec(w_feat_t.shape, lambda b, i: (0, 0)),
                  pl.BlockSpec((1, ROW_TILE), lambda b, i: (0, 0)),
                  pl.BlockSpec((TOK_BLOCK, TOK_BLOCK), lambda b, i: (0, 0))],
        out_specs=[tok_spec] * 4 + [feat_spec] * 4,
        out_shape=[tok_shape] * 4 + [feat_shape] * 4,
        compiler_params=_cparams(2),
        name="norm_in_proj",
    )(x, norm_g.reshape(1, D), scl, shift, w_tok, w_feat_t, jnp.asarray(_value_column_scale(ROW_TILE)),
      jnp.asarray(_position_to_token(), BF16))


def _store_row_groups(q_t, qz_ref, group):
    row = lax.broadcasted_iota(jnp.int32, q_t.shape, 0)
    for i in range(qz_ref.shape[0]):
        inside = (row >= i * group) & (row < (i + 1) * group)
        qz_ref[i] = jnp.where(inside, q_t, jnp.zeros_like(q_t))


def _key_block(k_ref, kj, rows=TOK_BLOCK):
    return k_ref[0, pl.ds(pl.multiple_of(kj * rows, rows), rows), :]


def _da_tile_passes(n_q):
    plain = [(qi, kj) for qi in range(n_q) for kj in range(qi - 1)]
    near = [(qi, qi - 1) for qi in range(1, n_q)]
    diag = [(qi, qi) for qi in range(n_q)]
    passes, start = [], 0
    for tiles, kind in ((plain, None), (near, NEAR_TILE), (diag, DIAG_TILE)):
        passes.append((start, len(tiles), kind))
        start += len(tiles)
    order = np.asarray(plain + near + diag, np.int32)
    return order[:, 0], order[:, 1], tuple(passes)


def _da_kernel(lam_ref, qs_ref, ks_ref, q_ref, k_ref, v_ref, bias_ref, g_ref, o_ref,
               acc_ref, m_ref, l_ref, qz_ref, s_ref, smax_ref, p_ref, a_ref, *, out_scale, passes):
    n_q, n_map = acc_ref.shape[0], acc_ref.shape[1]
    n_sub = DA_BLOCK // TOK_BLOCK
    acc_ref[...] = jnp.zeros(acc_ref.shape, F32)
    m_ref[...] = jnp.full(m_ref.shape, NEG_INF, F32)
    l_ref[...] = jnp.zeros(l_ref.shape, F32)

    for qi in range(n_q):
        q_t = jnp.concatenate([q_ref[0, n_sub * qi + j] for j in range(n_sub)], axis=-1)
        _store_row_groups(q_t, qz_ref.at[qi], DA_HEAD_DIM)

    def scores(qi, kj, maps=None):
        kb = _key_block(k_ref, kj, DA_BLOCK)
        return [jnp.dot(kb, qz_ref[qi, mp], preferred_element_type=F32)
                for mp in (range(n_map) if maps is None else maps)]

    def put_scores(ss, kind):
        for mp in range(n_map):
            s = ss[mp]
            if kind is not None:
                s = s + bias_ref[mp // 2, kind]
            s_ref[mp] = s
            smax_ref[mp] = jnp.max(s, axis=0, keepdims=True)

    def softmax_step(qi, slot):
        for mp in range(n_map):
            m_old = m_ref[qi, mp]
            m_new = jnp.maximum(m_old, smax_ref[mp])
            alpha = jnp.exp2(m_old - m_new)
            p = jnp.exp2(s_ref[mp] - m_new)
            l_ref[qi, mp] = alpha * l_ref[qi, mp] + jnp.sum(p, axis=0, keepdims=True)
            m_ref[qi, mp] = m_new
            a_ref[mp] = alpha
            p_ref[slot, mp] = p.astype(BF16)

    def weighted_values(kj, slot, maps=None):
        out = []
        for mp in (range(n_map) if maps is None else maps):
            pv = None
            for j in range(n_sub):
                vh = v_ref[0, n_sub * kj + j, (mp // 2) * DA_V_DIM:(mp // 2 + 1) * DA_V_DIM, :]
                part = jnp.dot(vh, p_ref[slot, mp, j * TOK_BLOCK:(j + 1) * TOK_BLOCK, :], preferred_element_type=F32)
                pv = part if pv is None else pv + part
            out.append(pv)
        return out

    def add_weighted(pv, qi):
        for mp in range(n_map):
            acc_ref[qi, mp] = a_ref[mp] * acc_ref[qi, mp] + pv[mp]

    for start, count, kind in passes:
        if count == 0:
            continue
        last = start + count - 1
        put_scores(scores(qs_ref[start], ks_ref[start]), kind)
        second = min(start + 1, last)
        next_scores = scores(qs_ref[second], ks_ref[second])
        softmax_step(qs_ref[start], start % 2)
        put_scores(next_scores, kind)

        def one_step(cur, cur_slot, last=last, kind=kind):
            prev = cur - 1
            nxt = jnp.minimum(cur + 1, last)
            pv, next_scores = [], []
            for mp in range(n_map):
                pv += weighted_values(ks_ref[prev], 1 - cur_slot, (mp,))
                next_scores += scores(qs_ref[nxt], ks_ref[nxt], (mp,))
            add_weighted(pv, qs_ref[prev])
            softmax_step(qs_ref[cur], cur_slot)
            put_scores(next_scores, kind)

        rest = count - 1
        first_slot = (start + 1) % 2

        def two_steps(j, carry, start=start, first_slot=first_slot, one_step=one_step):
            cur = start + 1 + 2 * j
            one_step(cur, first_slot)
            one_step(cur + 1, 1 - first_slot)
            return carry

        lax.fori_loop(0, rest // 2, two_steps, 0)
        if rest % 2:
            one_step(last, last % 2)
        add_weighted(weighted_values(ks_ref[last], last % 2), qs_ref[last])

    lam = lam_ref[0, 0]

    def finish(qi, carry):
        outs = []
        for hd in range(n_map // 2):
            o_t = (acc_ref[qi, 2 * hd] / l_ref[qi, 2 * hd]
                   - lam * (acc_ref[qi, 2 * hd + 1] / l_ref[qi, 2 * hd + 1]))
            o = o_t.T
            o = o * lax.rsqrt(jnp.mean(o * o, axis=-1, keepdims=True) + EPS) * g_ref[...]
            outs.append((o * out_scale).astype(o_ref.dtype))
        o_ref[0, pl.ds(pl.multiple_of(qi * DA_BLOCK, DA_BLOCK), DA_BLOCK), :] = jnp.concatenate(outs, axis=-1)
        return carry

    lax.fori_loop(0, n_q, finish, 0)


DA_GROUP = 2


def _diff_attention(q_t, k, v_t, bias, lam, subln_g, lambda_init):
    B, S, _ = k.shape
    n_blk = S // TOK_BLOCK
    n_q = S // DA_BLOCK
    nhd = DA_GROUP
    n_map = 2 * nhd
    wide = nhd * DA_V_DIM
    qs, ks, passes = _da_tile_passes(n_q)
    smem = pl.BlockSpec(memory_space=pltpu.SMEM)
    return pl.pallas_call(
        functools.partial(_da_kernel, out_scale=1.0 - lambda_init, passes=passes),
        grid=(B, DA_HEADS // nhd),
        in_specs=[smem, smem, smem,
                  pl.BlockSpec((1, n_blk, wide, TOK_BLOCK), lambda b, h: (b, 0, h, 0)),
                  pl.BlockSpec((1, S, wide), lambda b, h: (b, 0, h)),
                  pl.BlockSpec((1, n_blk, wide, TOK_BLOCK), lambda b, h: (b, 0, h, 0)),
                  pl.BlockSpec((nhd, 2, DA_BLOCK, DA_BLOCK), lambda b, h: (h, 0, 0, 0)),
                  pl.BlockSpec((1, DA_V_DIM), lambda b, h: (0, 0))],
        out_specs=pl.BlockSpec((1, S, wide), lambda b, h: (b, 0, h)),
        out_shape=jax.ShapeDtypeStruct((B, S, DA_WIDTH), BF16),
        scratch_shapes=[pltpu.VMEM((n_q, n_map, DA_V_DIM, DA_BLOCK), F32),
                        pltpu.VMEM((n_q, n_map, 1, DA_BLOCK), F32),
                        pltpu.VMEM((n_q, n_map, 1, DA_BLOCK), F32),
                        pltpu.VMEM((n_q, n_map, wide, DA_BLOCK), BF16),
                        pltpu.VMEM((n_map, DA_BLOCK, DA_BLOCK), F32),
                        pltpu.VMEM((n_map, 1, DA_BLOCK), F32),
                        pltpu.VMEM((2, n_map, DA_BLOCK, DA_BLOCK), BF16),
                        pltpu.VMEM((n_map, 1, DA_BLOCK), F32)],
        compiler_params=_cparams(2),
        name="diff_attention",
    )(lam, jnp.asarray(qs), jnp.asarray(ks), q_t, k, v_t, bias, subln_g.reshape(1, DA_V_DIM))


def _sb_kernel(q_ref, k_ref, v_ref, mask_ref, o_ref, acc_ref, carry_ref, qz_ref, u_ref, tot_ref, w_ref):
    qi = pl.program_id(2)
    nh = acc_ref.shape[0]
    slab = qz_ref.shape[1]
    slab_heads = slab // SB_HEAD_DIM
    for sl in range(nh // slab_heads):
        _store_row_groups(q_ref[0, 0, sl * slab:(sl + 1) * slab, :], qz_ref.at[pl.ds(sl * slab_heads, slab_heads)],
                          SB_HEAD_DIM)
    carry_ref[...] = jnp.ones(carry_ref.shape, F32)
    sub = lax.broadcasted_iota(jnp.int32, (SUBLANES, TOK_BLOCK), 0)
    rows = lambda x, g: x[g * SUBLANES:(g + 1) * SUBLANES]

    def logits(kj):
        kb = _key_block(k_ref, kj)
        return [jnp.dot(kb[:, (hh // slab_heads) * slab:(hh // slab_heads + 1) * slab], qz_ref[hh],
                        preferred_element_type=F32) for hh in range(nh)]

    def gates(zs, slot, mask=None):
        for hh in range(nh):
            t = jnp.tanh(zs[hh])
            if mask is not None:
                t = jnp.where(mask > 0.5, t, -1.0)
            u = 1.0 - t
            u_ref[slot, hh] = u
            tot = rows(u, CHUNK - 1)
            for g in reversed(range(CHUNK - 1)):
                tot = tot * rows(u, g)
            tot_ref[slot, hh] = tot * 2.0 ** -CHUNK

    def weights(slot, keep_carry=None):
        for hh in range(nh):
            suffix = tot_ref[slot, hh]
            for d in (1, 2, 4):
                shifted = pltpu.roll(suffix, SUBLANES - d, axis=0)
                suffix = jnp.where(sub + d < SUBLANES, suffix * shifted, suffix)
            above = jnp.where(sub + 1 < SUBLANES, pltpu.roll(suffix, SUBLANES - 1, axis=0), 1.0)
            carry = carry_ref[hh]
            new_carry = carry * jnp.broadcast_to(suffix[0:1], carry.shape)
            carry_ref[hh] = new_carry if keep_carry is None else new_carry * keep_carry
            run = above * carry
            w = [None] * CHUNK
            for g in reversed(range(CHUNK)):
                ug = u_ref[slot, hh, g * SUBLANES:(g + 1) * SUBLANES, :]
                w[g] = (2.0 - ug) * run
                run = run * ug
            w_ref[slot, hh] = jnp.concatenate(w, axis=0).astype(BF16)

    def weighted_values(kj, slot):
        vb = v_ref[0, kj]
        return [jnp.dot(vb[hh * SB_HEAD_DIM:(hh + 1) * SB_HEAD_DIM], w_ref[slot, hh], preferred_element_type=F32)
                for hh in range(nh)]

    def alive():
        return (jnp.max(carry_ref[...]) > 0.0).astype(jnp.int32)

    before = jnp.maximum(qi - 1, 0)
    z_diag = logits(qi)
    z_before = logits(before)
    gates(z_diag, 0, mask_ref[...])
    gates(z_before, 1)
    weights(0, keep_carry=(qi > 0).astype(F32))
    weights(1)
    pv_diag = weighted_values(qi, 0)
    pv_before = weighted_values(before, 1)
    for hh in range(nh):
        acc_ref[hh] = pv_diag[hh] + pv_before[hh]

    more = jnp.logical_and(qi >= 2, alive() > 0)
    pl.when(more)(lambda: gates(logits(jnp.maximum(qi - 2, 0)), 0))

    def left_cond(state):
        i, live = state
        return jnp.logical_and(i < qi - 1, live > 0)

    def left_body(state):
        i, _ = state
        kj = qi - 2 - i
        nxt = logits(jnp.maximum(kj - 1, 0))
        weights(0)
        pv = weighted_values(kj, 0)
        for hh in range(nh):
            acc_ref[hh] = acc_ref[hh] + pv[hh]
        gates(nxt, 0)
        return i + 1, alive()

    lax.while_loop(left_cond, left_body, (jnp.int32(0), more.astype(jnp.int32)))

    o_t =jnp.concatenate([acc_ref[hh] for hh in range(nh)], axis=0)
    o_ref[0] = o_t.T.astype(o_ref.dtype)


SB_GROUP = 8


def _sb_attention(q_t, k, v_t):
    B, S, _ = k.shape
    n_blk = S // TOK_BLOCK
    nh = SB_GROUP
    pair = nh * SB_HEAD_DIM
    return pl.pallas_call(
        _sb_kernel,
        grid=(B, SB_HEADS // nh, n_blk),
        in_specs=[pl.BlockSpec((1, 1, pair, TOK_BLOCK), lambda b, h, i: (b, i, h, 0)),
                  pl.BlockSpec((1, S, pair), lambda b, h, i: (b, 0, h)),
                  pl.BlockSpec((1, n_blk, pair, TOK_BLOCK), lambda b, h, i: (b, 0, h, 0)),
                  pl.BlockSpec((TOK_BLOCK, TOK_BLOCK), lambda b, h, i: (0, 0))],
        out_specs=pl.BlockSpec((1, TOK_BLOCK, pair), lambda b, h, i: (b, i, h)),
        out_shape=jax.ShapeDtypeStruct((B, S, SB_WIDTH), BF16),
        scratch_shapes=[pltpu.VMEM((nh, SB_HEAD_DIM, TOK_BLOCK), F32),
                        pltpu.VMEM((nh, SUBLANES, TOK_BLOCK), F32),
                        pltpu.VMEM((nh, TOK_BLOCK, TOK_BLOCK), BF16),
                        pltpu.VMEM((2, nh, TOK_BLOCK, TOK_BLOCK), F32),
                        pltpu.VMEM((2, nh, SUBLANES, TOK_BLOCK), F32),
                        pltpu.VMEM((2, nh, TOK_BLOCK, TOK_BLOCK), BF16)],
        compiler_params=_cparams(3),
        name="stick_breaking_attention",
    )(q_t, k, v_t, jnp.asarray(_strict_causal_tile()))


def _out_kernel(oda_ref, osb_ref, gda_ref, gsb_ref, x_ref, gate_ref, w_ref, fg_ref, out_ref, *, final_norm):
    def gated(o_ref, g_ref):
        g = g_ref[0].astype(F32)
        return (o_ref[0].astype(F32) * (g / (1.0 + jnp.exp(-g)))).astype(BF16)

    split = oda_ref.shape[-1]
    y = jnp.dot(gated(oda_ref, gda_ref), w_ref[0:split, :], preferred_element_type=F32)
    y = y + jnp.dot(gated(osb_ref, gsb_ref), w_ref[split:, :], preferred_element_type=F32)
    xo = x_ref[0] + gate_ref[0] * y
    if final_norm:
        xo = xo * lax.rsqrt(jnp.mean(xo * xo, axis=-1, keepdims=True) + EPS) * fg_ref[...]
    out_ref[0] = xo


def _out_project(o_da, o_sb, g_da, g_sb, x, gate, w_out, final_g, final_norm):
    B, S, D = x.shape
    half_spec = pl.BlockSpec((1, OUT_ROW_TILE, DA_WIDTH), lambda b, i: (b, i, 0))
    row_spec = pl.BlockSpec((1, OUT_ROW_TILE, D), lambda b, i: (b, i, 0))
    return pl.pallas_call(
        functools.partial(_out_kernel, final_norm=final_norm),
        grid=(B, S // OUT_ROW_TILE),
        in_specs=[half_spec] * 4 + [row_spec,
                                    pl.BlockSpec((1, 1, D), lambda b, i: (b, 0, 0)),
                                    pl.BlockSpec(w_out.shape, lambda b, i: (0, 0)),
                                    pl.BlockSpec((1, D), lambda b, i: (0, 0))],
        out_specs=row_spec,
        out_shape=jax.ShapeDtypeStruct((B, S, D), F32),
        compiler_params=_cparams(2),
        name="gate_out_proj",
    )(o_da, o_sb, g_da, g_sb, x, gate, w_out, final_g.reshape(1, D))


def _split_w_in(w):
    sizes = (DA_WIDTH, DA_WIDTH, DA_WIDTH, DA_WIDTH, SB_WIDTH, SB_WIDTH, SB_WIDTH, SB_WIDTH)
    q_da, k_da, v_da, g_da, q_sb, k_sb, v_sb, g_sb = jnp.split(w, np.cumsum(sizes)[:-1].tolist(), axis=1)
    w_tok = jnp.concatenate([k_da, k_sb, g_da, g_sb], axis=1).astype(BF16)
    q_scale = 1.0 / math.sqrt(DA_HEAD_DIM)
    w_feat_t = jnp.concatenate([q_da * (LOG2E * q_scale), v_da, q_sb * (0.5 * q_scale), v_sb], axis=1)
    return w_tok, w_feat_t.T.astype(BF16)


def kernel(x, c, norm_g, w_ada, b_ada, w_in, lambda_q1, lambda_k1, lambda_q2, lambda_k2, subln_g, w_out,
           rel_bias, final_g):
    B, S, D = x.shape
    depth = w_in.shape[0]
    assert S % ROW_TILE == 0 and ROW_TILE % TOK_BLOCK == 0 and S % DA_BLOCK == 0
    assert S % OUT_ROW_TILE == 0 and OUT_ROW_TILE % TOK_BLOCK == 0
    assert DA_HEAD_DIM == SB_HEAD_DIM and DA_WIDTH == SB_WIDTH and DA_WIDTH + SB_WIDTH == w_out.shape[1]
    bias = _bias_tiles(rel_bias)
    for l in range(depth):
        lambda_init = _lambda_init(l)
        mod, lam = _modulation(c, w_ada[l], b_ada[l], lambda_q1[l], lambda_k1[l], lambda_q2[l], lambda_k2[l],
                               lambda_init)
        shift, scl, gate = (m.reshape(B, 1, D) for m in jnp.split(mod, 3, axis=-1))
        w_tok, w_feat_t = _split_w_in(w_in[l])
        k_da, k_sb, g_da, g_sb, q_da, v_da, q_sb, v_sb = _project(x, norm_g[l], scl, shift, w_tok, w_feat_t)
        o_da = _diff_attention(q_da, k_da, v_da, bias, lam[:, :1], subln_g[l], lambda_init)
        o_sb = _sb_attention(q_sb, k_sb, v_sb)
        x = _out_project(o_da, o_sb, g_da, g_sb, x, gate, w_out[l].astype(BF16), final_g, l == depth - 1)
    return x
```

```python
import functools
import math

import numpy as np
import jax
import jax.numpy as jnp
from jax import lax
from jax.experimental import pallas as pl
from jax.experimental.pallas import tpu as pltpu

DA_HEADS = 4
DA_HEAD_DIM = 64
DA_V_DIM = 2 * DA_HEAD_DIM
DA_WIDTH = DA_HEADS * DA_V_DIM
SB_HEADS = 8
SB_HEAD_DIM = 64
SB_WIDTH = SB_HEADS * SB_HEAD_DIM
N_BUCKETS = 32
MAX_DISTANCE = 128
EPS = 1e-6
NEG_INF = -1e30

SUBLANES = 8
LANES = 128
TOK_BLOCK = 256
CHUNK = TOK_BLOCK // SUBLANES
DA_BLOCK = 2 * TOK_BLOCK
LOG2E = math.log2(math.e)
DIAG_TILE, NEAR_TILE = 0, 1
ROW_TILE = 512
OUT_ROW_TILE = 1024
VMEM_LIMIT = 56 * 1024 * 1024

F32 = jnp.float32
BF16 = jnp.bfloat16


def _lambda_init(layer_idx):
    return 0.8 - 0.6 * math.exp(-0.3 * layer_idx)


def _block_token(pos):
    return (pos % SUBLANES) * CHUNK + pos // SUBLANES


def _bucket_tiles():
    pos = np.arange(DA_BLOCK)
    key_tok = (pos // TOK_BLOCK) * TOK_BLOCK + _block_token(pos % TOK_BLOCK)
    rel0 = pos[None, :] - key_tok[:, None]
    rel = np.stack([np.maximum(rel0, 0), rel0 + DA_BLOCK]).astype(np.int32)
    max_exact = N_BUCKETS // 2
    nf = np.maximum(rel, 1).astype(np.float32)
    large = max_exact + (np.log(nf / np.float32(max_exact)) / np.float32(math.log(MAX_DISTANCE / max_exact))
                         * np.float32(N_BUCKETS - max_exact)).astype(np.int32)
    large = np.minimum(large, N_BUCKETS - 1)
    bucket = np.where(rel < max_exact, rel, large).astype(np.int32)
    bucket[0] = np.where(rel0 < 0, N_BUCKETS, bucket[0])
    return bucket


def _position_to_token():
    pos = np.arange(TOK_BLOCK)
    m = np.zeros((TOK_BLOCK, TOK_BLOCK), np.float32)
    m[pos, _block_token(pos)] = 1.0
    return m


def _strict_causal_tile():
    pos = np.arange(TOK_BLOCK)
    return (_block_token(pos)[:, None] < pos[None, :]).astype(np.float32)


def _value_column_scale(n):
    g = (np.arange(n) % TOK_BLOCK) // SUBLANES
    return (2.0 ** -(CHUNK - g)).astype(np.float32).reshape(1, n)


def _cparams(n_axes):
    return pltpu.CompilerParams(dimension_semantics=("arbitrary",) * n_axes, vmem_limit_bytes=VMEM_LIMIT)


def _mod_kernel(c_ref, w_ref, b_ref, lq1_ref, lk1_ref, lq2_ref, lk2_ref, mod_ref, lam_ref, *, lambda_init):
    c = c_ref[...]
    silu_c = c / (1.0 + jnp.exp(-c))
    mod_ref[...] = jnp.dot(silu_c, w_ref[...], preferred_element_type=F32,
                           precision=lax.Precision.HIGHEST) + b_ref[...]
    s1 = jnp.sum(lq1_ref[...] * lk1_ref[...], axis=-1, keepdims=True)
    s2 = jnp.sum(lq2_ref[...] * lk2_ref[...], axis=-1, keepdims=True)
    lam_ref[...] = jnp.broadcast_to(jnp.exp(s1) - jnp.exp(s2) + lambda_init, lam_ref.shape)


def _modulation(c, w_ada, b_ada, lq1, lk1, lq2, lk2, lambda_init):
    B, D = c.shape
    n_out = w_ada.shape[1]
    col = 512
    vec = lambda: pl.BlockSpec((1, DA_HEAD_DIM), lambda j: (0, 0))
    return pl.pallas_call(
        functools.partial(_mod_kernel, lambda_init=lambda_init),
        grid=(n_out // col,),
        in_specs=[pl.BlockSpec((B, D), lambda j: (0, 0)),
                  pl.BlockSpec((D, col), lambda j: (0, j)),
                  pl.BlockSpec((1, col), lambda j: (0, j)),
                  vec(), vec(), vec(), vec()],
        out_specs=[pl.BlockSpec((B, col), lambda j: (0, j)),
                   pl.BlockSpec((1, LANES), lambda j: (0, 0))],
        out_shape=[jax.ShapeDtypeStruct((B, n_out), F32), jax.ShapeDtypeStruct((1, LANES), F32)],
        compiler_params=_cparams(1),
        name="adaln_mod",
    )(c, w_ada, b_ada.reshape(1, n_out), lq1.reshape(1, -1), lk1.reshape(1, -1), lq2.reshape(1, -1),
      lk2.reshape(1, -1))


def _bias_kernel(rb_ref, bucket_ref, out_ref):
    h = pl.program_id(0)
    far = rb_ref[N_BUCKETS - 1, h]
    for t in range(2):
        bucket = bucket_ref[t]
        tile = jnp.zeros(bucket.shape, F32)
        for b in range(N_BUCKETS - 1):
            tile = jnp.where(bucket == b, (rb_ref[b, h] - far) * LOG2E, tile)
        out_ref[0, t] = jnp.where(bucket == N_BUCKETS, NEG_INF, tile)


def _bias_tiles(rel_bias):
    buckets = jnp.asarray(_bucket_tiles())
    return pl.pallas_call(
        _bias_kernel,
        grid=(DA_HEADS,),
        in_specs=[pl.BlockSpec(memory_space=pltpu.SMEM),
                  pl.BlockSpec((2, DA_BLOCK, DA_BLOCK), lambda h: (0, 0, 0))],
        out_specs=pl.BlockSpec((1, 2, DA_BLOCK, DA_BLOCK), lambda h: (h, 0, 0, 0)),
        out_shape=jax.ShapeDtypeStruct((DA_HEADS, 2, DA_BLOCK, DA_BLOCK), F32),
        compiler_params=_cparams(1),
        name="rel_bias_tiles",
    )(rel_bias, buckets)


def _modulated_norm(x, g, scl, shift):
    ms = jnp.mean(x * x, axis=-1, keepdims=True)
    xn = x * lax.rsqrt(ms + EPS) * g
    return (xn * (1.0 + scl) + shift).astype(BF16)


def _proj_kernel(x_ref, g_ref, scl_ref, shift_ref, wt_ref, wf_ref, vscale_ref, perm_ref,
                 kda_ref, ksb_ref, qda_ref, vda_ref, qsb_ref, vsb_ref):
    h = _modulated_norm(x_ref[0], g_ref[...], scl_ref[0], shift_ref[0])
    hp = jnp.concatenate(
        [jnp.dot(perm_ref[...], h[j * TOK_BLOCK:(j + 1) * TOK_BLOCK], preferred_element_type=F32).astype(BF16)
         for j in range(h.shape[0] // TOK_BLOCK)], axis=0)
    width = kda_ref.shape[-1]
    for i, o_ref in enumerate((kda_ref, ksb_ref)):
        o_ref[0] = jnp.dot(hp, wt_ref[:, i * width:(i + 1) * width], preferred_element_type=F32).astype(BF16)
    for i, (o_ref, src) in enumerate(((qda_ref, h), (vda_ref, hp), (qsb_ref, h), (vsb_ref, hp))):
        r = lax.dot_general(wf_ref[i * width:(i + 1) * width, :], src, (((1,), (1,)), ((), ())),
                            preferred_element_type=F32)
        if o_ref is vsb_ref:
            r = r * vscale_ref[...]
        r = r.astype(BF16)
        for t in range(o_ref.shape[1]):
            o_ref[0, t] = r[:, t * TOK_BLOCK:(t + 1) * TOK_BLOCK]


def _project(x, norm_g, scl, shift, w_tok, w_feat_t):
    B, S, D = x.shape
    n_blk = S // TOK_BLOCK
    blk_per_tile = ROW_TILE // TOK_BLOCK
    width = DA_WIDTH
    tok_spec = pl.BlockSpec((1, ROW_TILE, width), lambda b, i: (b, i, 0))
    feat_spec = pl.BlockSpec((1, blk_per_tile, width, TOK_BLOCK), lambda b, i: (b, i, 0, 0))
    tok_shape = jax.ShapeDtypeStruct((B, S, width), BF16)
    feat_shape = jax.ShapeDtypeStruct((B, n_blk, width, TOK_BLOCK), BF16)
    mod_spec = pl.BlockSpec((1, 1, D), lambda b, i: (b, 0, 0))
    return pl.pallas_call(
        _proj_kernel,
        grid=(B, S // ROW_TILE),
        in_specs=[pl.BlockSpec((1, ROW_TILE, D), lambda b, i: (b, i, 0)),
                  pl.BlockSpec((1, D), lambda b, i: (0, 0)),
                  mod_spec, mod_spec,
                  pl.BlockSpec(w_tok.shape, lambda b, i: (0, 0)),
                  pl.BlockSpec(w_feat_t.shape, lambda b, i: (0, 0)),
                  pl.BlockSpec((1, ROW_TILE), lambda b, i: (0, 0)),
                  pl.BlockSpec((TOK_BLOCK, TOK_BLOCK), lambda b, i: (0, 0))],
        out_specs=[tok_spec] * 2 + [feat_spec] * 4,
        out_shape=[tok_shape] * 2 + [feat_shape] * 4,
        compiler_params=_cparams(2),
        name="norm_in_proj",
    )(x, norm_g.reshape(1, D), scl, shift, w_tok, w_feat_t, jnp.asarray(_value_column_scale(ROW_TILE)),
      jnp.asarray(_position_to_token(), BF16))


def _store_row_groups(q_t, qz_ref, group):
    row = lax.broadcasted_iota(jnp.int32, q_t.shape, 0)
    for i in range(qz_ref.shape[0]):
        inside = (row >= i * group) & (row < (i + 1) * group)
        qz_ref[i] = jnp.where(inside, q_t, jnp.zeros_like(q_t))


def _key_block(k_ref, kj, rows=TOK_BLOCK):
    return k_ref[0, pl.ds(pl.multiple_of(kj * rows, rows), rows), :]


def _da_tile_passes(n_q):
    plain = [(qi, kj) for qi in range(n_q) for kj in range(qi - 1)]
    near = [(qi, qi - 1) for qi in range(1, n_q)]
    diag = [(qi, qi) for qi in range(n_q)]
    passes, start = [], 0
    for tiles, kind in ((plain, None), (near, NEAR_TILE), (diag, DIAG_TILE)):
        passes.append((start, len(tiles), kind))
        start += len(tiles)
    order = np.asarray(plain + near + diag, np.int32)
    return order[:, 0], order[:, 1], tuple(passes)


def _da_kernel(lam_ref, qs_ref, ks_ref, q_ref, k_ref, v_ref, bias_ref, g_ref, o_ref,
               acc_ref, m_ref, l_ref, qz_ref, s_ref, smax_ref, p_ref, a_ref, *, out_scale, passes):
    n_q, n_map = acc_ref.shape[0], acc_ref.shape[1]
    n_sub = DA_BLOCK // TOK_BLOCK
    acc_ref[...] = jnp.zeros(acc_ref.shape, F32)
    m_ref[...] = jnp.full(m_ref.shape, NEG_INF, F32)
    l_ref[...] = jnp.zeros(l_ref.shape, F32)

    for qi in range(n_q):
        q_t = jnp.concatenate([q_ref[0, n_sub * qi + j] for j in range(n_sub)], axis=-1)
        _store_row_groups(q_t, qz_ref.at[qi], DA_HEAD_DIM)

    def scores(qi, kj, maps=None):
        kb = _key_block(k_ref, kj, DA_BLOCK)
        return [jnp.dot(kb, qz_ref[qi, mp], preferred_element_type=F32)
                for mp in (range(n_map) if maps is None else maps)]

    def put_scores(ss, kind):
        for mp in range(n_map):
            s = ss[mp]
            if kind is not None:
                s = s + bias_ref[mp // 2, kind]
            s_ref[mp] = s
            smax_ref[mp] = jnp.max(s, axis=0, keepdims=True)

    def softmax_step(qi, slot):
        for mp in range(n_map):
            m_old = m_ref[qi, mp]
            m_new = jnp.maximum(m_old, smax_ref[mp])
            alpha = jnp.exp2(m_old - m_new)
            p = jnp.exp2(s_ref[mp] - m_new)
            l_ref[qi, mp] = alpha * l_ref[qi, mp] + jnp.sum(p, axis=0, keepdims=True)
            m_ref[qi, mp] = m_new
            a_ref[mp] = alpha
            p_ref[slot, mp] = p.astype(BF16)

    def weighted_values(kj, slot, maps=None):
        out = []
        for mp in (range(n_map) if maps is None else maps):
            pv = None
            for j in range(n_sub):
                vh = v_ref[0, n_sub * kj + j, (mp // 2) * DA_V_DIM:(mp // 2 + 1) * DA_V_DIM, :]
                part = jnp.dot(vh, p_ref[slot, mp, j * TOK_BLOCK:(j + 1) * TOK_BLOCK, :], preferred_element_type=F32)
                pv = part if pv is None else pv + part
            out.append(pv)
        return out

    def add_weighted(pv, qi):
        for mp in range(n_map):
            acc_ref[qi, mp] = a_ref[mp] * acc_ref[qi, mp] + pv[mp]

    for start, count, kind in passes:
        if count == 0:
            continue
        last = start + count - 1
        put_scores(scores(qs_ref[start], ks_ref[start]), kind)
        second = min(start + 1, last)
        next_scores = scores(qs_ref[second], ks_ref[second])
        softmax_step(qs_ref[start], start % 2)
        put_scores(next_scores, kind)

        def one_step(cur, cur_slot, last=last, kind=kind):
            prev = cur - 1
            nxt = jnp.minimum(cur + 1, last)
            pv, next_scores = [], []
            for mp in range(n_map):
                pv += weighted_values(ks_ref[prev], 1 - cur_slot, (mp,))
                next_scores += scores(qs_ref[nxt], ks_ref[nxt], (mp,))
            add_weighted(pv, qs_ref[prev])
            softmax_step(qs_ref[cur], cur_slot)
            put_scores(next_scores, kind)

        rest = count - 1
        first_slot = (start + 1) % 2

        def two_steps(j, carry, start=start, first_slot=first_slot, one_step=one_step):
            cur = start + 1 + 2 * j
            one_step(cur, first_slot)
            one_step(cur + 1, 1 - first_slot)
            return carry

        lax.fori_loop(0, rest // 2, two_steps, 0)
        if rest % 2:
            one_step(last, last % 2)
        add_weighted(weighted_values(ks_ref[last], last % 2), qs_ref[last])

    lam = lam_ref[0, 0]

    def finish(qi, carry):
        outs = []
        for hd in range(n_map // 2):
            o_t = (acc_ref[qi, 2 * hd] / l_ref[qi, 2 * hd]
                   - lam * (acc_ref[qi, 2 * hd + 1] / l_ref[qi, 2 * hd + 1]))
            o = o_t.T
            o = o * lax.rsqrt(jnp.mean(o * o, axis=-1, keepdims=True) + EPS) * g_ref[...]
            outs.append((o * out_scale).astype(o_ref.dtype))
        o_ref[0, pl.ds(pl.multiple_of(qi * DA_BLOCK, DA_BLOCK), DA_BLOCK), :] = jnp.concatenate(outs, axis=-1)
        return carry

    lax.fori_loop(0, n_q, finish, 0)


DA_GROUP = 2


def _diff_attention(q_t, k, v_t, bias, lam, subln_g, lambda_init):
    B, S, _ = k.shape
    n_blk = S // TOK_BLOCK
    n_q = S // DA_BLOCK
    nhd = DA_GROUP
    n_map = 2 * nhd
    wide = nhd * DA_V_DIM
    qs, ks, passes = _da_tile_passes(n_q)
    smem = pl.BlockSpec(memory_space=pltpu.SMEM)
    return pl.pallas_call(
        functools.partial(_da_kernel, out_scale=1.0 - lambda_init, passes=passes),
        grid=(B, DA_HEADS // nhd),
        in_specs=[smem, smem, smem,
                  pl.BlockSpec((1, n_blk, wide, TOK_BLOCK), lambda b, h: (b, 0, h, 0)),
                  pl.BlockSpec((1, S, wide), lambda b, h: (b, 0, h)),
                  pl.BlockSpec((1, n_blk, wide, TOK_BLOCK), lambda b, h: (b, 0, h, 0)),
                  pl.BlockSpec((nhd, 2, DA_BLOCK, DA_BLOCK), lambda b, h: (h, 0, 0, 0)),
                  pl.BlockSpec((1, DA_V_DIM), lambda b, h: (0, 0))],
        out_specs=pl.BlockSpec((1, S, wide), lambda b, h: (b, 0, h)),
        out_shape=jax.ShapeDtypeStruct((B, S, DA_WIDTH), BF16),
        scratch_shapes=[pltpu.VMEM((n_q, n_map, DA_V_DIM, DA_BLOCK), F32),
                        pltpu.VMEM((n_q, n_map, 1, DA_BLOCK), F32),
                        pltpu.VMEM((n_q, n_map, 1, DA_BLOCK), F32),
                        pltpu.VMEM((n_q, n_map, wide, DA_BLOCK), BF16),
                        pltpu.VMEM((n_map, DA_BLOCK, DA_BLOCK), F32),
                        pltpu.VMEM((n_map, 1, DA_BLOCK), F32),
                        pltpu.VMEM((2, n_map, DA_BLOCK, DA_BLOCK), BF16),
                        pltpu.VMEM((n_map, 1, DA_BLOCK), F32)],
        compiler_params=_cparams(2),
        name="diff_attention",
    )(lam, jnp.asarray(qs), jnp.asarray(ks), q_t, k, v_t, bias, subln_g.reshape(1, DA_V_DIM))


def _sb_kernel(q_ref, k_ref, v_ref, mask_ref, o_ref, acc_ref, carry_ref, qz_ref, u_ref, tot_ref, w_ref):
    qi = pl.program_id(2)
    nh = acc_ref.shape[0]
    slab = qz_ref.shape[1]
    slab_heads = slab // SB_HEAD_DIM
    for sl in range(nh // slab_heads):
        _store_row_groups(q_ref[0, 0, sl * slab:(sl + 1) * slab, :], qz_ref.at[pl.ds(sl * slab_heads, slab_heads)],
                          SB_HEAD_DIM)
    carry_ref[...] = jnp.ones(carry_ref.shape, F32)
    sub = lax.broadcasted_iota(jnp.int32, (SUBLANES, TOK_BLOCK), 0)
    rows = lambda x, g: x[g * SUBLANES:(g + 1) * SUBLANES]

    def logits(kj):
        kb = _key_block(k_ref, kj)
        return [jnp.dot(kb[:, (hh // slab_heads) * slab:(hh // slab_heads + 1) * slab], qz_ref[hh],
                        preferred_element_type=F32) for hh in range(nh)]

    def gates(zs, slot, mask=None):
        for hh in range(nh):
            t = jnp.tanh(zs[hh])
            if mask is not None:
                t = jnp.where(mask > 0.5, t, -1.0)
            u = 1.0 - t
            u_ref[slot, hh] = u
            tot = rows(u, CHUNK - 1)
            for g in reversed(range(CHUNK - 1)):
                tot = tot * rows(u, g)
            tot_ref[slot, hh] = tot * 2.0 ** -CHUNK

    def weights(slot, keep_carry=None):
        for hh in range(nh):
            suffix = tot_ref[slot, hh]
            for d in (1, 2, 4):
                shifted = pltpu.roll(suffix, SUBLANES - d, axis=0)
                suffix = jnp.where(sub + d < SUBLANES, suffix * shifted, suffix)
            above = jnp.where(sub + 1 < SUBLANES, pltpu.roll(suffix, SUBLANES - 1, axis=0), 1.0)
            carry = carry_ref[hh]
            new_carry = carry * jnp.broadcast_to(suffix[0:1], carry.shape)
            carry_ref[hh] = new_carry if keep_carry is None else new_carry * keep_carry
            run = above * carry
            w = [None] * CHUNK
            for g in reversed(range(CHUNK)):
                ug = u_ref[slot, hh, g * SUBLANES:(g + 1) * SUBLANES, :]
                w[g] = (2.0 - ug) * run
                run = run * ug
            w_ref[slot, hh] = jnp.concatenate(w, axis=0).astype(BF16)

    def weighted_values(kj, slot):
        vb = v_ref[0, kj]
        return [jnp.dot(vb[hh * SB_HEAD_DIM:(hh + 1) * SB_HEAD_DIM], w_ref[slot, hh], preferred_element_type=F32)
                for hh in range(nh)]

    def alive():
        return (jnp.max(carry_ref[...]) > 0.0).astype(jnp.int32)

    before = jnp.maximum(qi - 1, 0)
    z_diag = logits(qi)
    z_before = logits(before)
    gates(z_diag, 0, mask_ref[...])
    gates(z_before, 1)
    weights(0, keep_carry=(qi > 0).astype(F32))
    weights(1)
    pv_diag = weighted_values(qi, 0)
    pv_before = weighted_values(before, 1)
    for hh in range(nh):
        acc_ref[hh] = pv_diag[hh] + pv_before[hh]

    more = jnp.logical_and(qi >= 2, alive() > 0)
    pl.when(more)(lambda: gates(logits(jnp.maximum(qi - 2, 0)), 0))

    def left_cond(state):
        i, live = state
        return jnp.logical_and(i < qi - 1, live > 0)

    def left_body(state):
        i, _ = state
        kj = qi - 2 - i
        nxt = logits(jnp.maximum(kj - 1, 0))
        weights(0)
        pv = weighted_values(kj, 0)
        for hh in range(nh):
            acc_ref[hh] = acc_ref[hh] + pv[hh]
        gates(nxt, 0)
        return i + 1, alive()

    lax.while_loop(left_cond, left_body, (jnp.int32(0), more.astype(jnp.int32)))

    o_t =jnp.concatenate([acc_ref[hh] for hh in range(nh)], axis=0)
    o_ref[0] = o_t.T.astype(o_ref.dtype)


SB_GROUP = 8


def _sb_attention(q_t, k, v_t):
    B, S, _ = k.shape
    n_blk = S // TOK_BLOCK
    nh = SB_GROUP
    pair = nh * SB_HEAD_DIM
    return pl.pallas_call(
        _sb_kernel,
        grid=(B, SB_HEADS // nh, n_blk),
        in_specs=[pl.BlockSpec((1, 1, pair, TOK_BLOCK), lambda b, h, i: (b, i, h, 0)),
                  pl.BlockSpec((1, S, pair), lambda b, h, i: (b, 0, h)),
                  pl.BlockSpec((1, n_blk, pair, TOK_BLOCK), lambda b, h, i: (b, 0, h, 0)),
                  pl.BlockSpec((TOK_BLOCK, TOK_BLOCK), lambda b, h, i: (0, 0))],
        out_specs=pl.BlockSpec((1, TOK_BLOCK, pair), lambda b, h, i: (b, i, h)),
        out_shape=jax.ShapeDtypeStruct((B, S, SB_WIDTH), BF16),
        scratch_shapes=[pltpu.VMEM((nh, SB_HEAD_DIM, TOK_BLOCK), F32),
                        pltpu.VMEM((nh, SUBLANES, TOK_BLOCK), F32),
                        pltpu.VMEM((nh, TOK_BLOCK, TOK_BLOCK), BF16),
                        pltpu.VMEM((2, nh, TOK_BLOCK, TOK_BLOCK), F32),
                        pltpu.VMEM((2, nh, SUBLANES, TOK_BLOCK), F32),
                        pltpu.VMEM((2, nh, TOK_BLOCK, TOK_BLOCK), BF16)],
        compiler_params=_cparams(3),
        name="stick_breaking_attention",
    )(q_t, k, v_t, jnp.asarray(_strict_causal_tile()))


def _out_kernel(oda_ref, osb_ref, x_ref, g_ref, scl_ref, shift_ref, gate_ref, wg_ref, w_ref, fg_ref, out_ref, *,
                final_norm):
    x = x_ref[0]
    h = _modulated_norm(x, g_ref[...], scl_ref[0], shift_ref[0])
    split = oda_ref.shape[-1]

    def gated(o_ref, cols):
        g = jnp.dot(h, wg_ref[:, cols], preferred_element_type=F32)
        return (o_ref[0].astype(F32) * (g / (1.0 + jnp.exp(-g)))).astype(BF16)

    y = jnp.dot(gated(oda_ref, slice(0, split)), w_ref[0:split, :], preferred_element_type=F32)
    y = y + jnp.dot(gated(osb_ref, slice(split, 2 * split)), w_ref[split:, :], preferred_element_type=F32)
    xo = x + gate_ref[0] * y
    if final_norm:
        xo = xo * lax.rsqrt(jnp.mean(xo * xo, axis=-1, keepdims=True) + EPS) * fg_ref[...]
    out_ref[0] = xo


def _out_project(o_da, o_sb, x, norm_g, scl, shift, gate, w_gate, w_out, final_g, final_norm):
    B, S, D = x.shape
    half_spec = pl.BlockSpec((1, OUT_ROW_TILE, DA_WIDTH), lambda b, i: (b, i, 0))
    row_spec = pl.BlockSpec((1, OUT_ROW_TILE, D), lambda b, i: (b, i, 0))
    vec_spec = pl.BlockSpec((1, D), lambda b, i: (0, 0))
    mod_spec = pl.BlockSpec((1, 1, D), lambda b, i: (b, 0, 0))
    return pl.pallas_call(
        functools.partial(_out_kernel, final_norm=final_norm),
        grid=(B, S // OUT_ROW_TILE),
        in_specs=[half_spec, half_spec, row_spec, vec_spec, mod_spec, mod_spec, mod_spec,
                  pl.BlockSpec(w_gate.shape, lambda b, i: (0, 0)),
                  pl.BlockSpec(w_out.shape, lambda b, i: (0, 0)),
                  vec_spec],
        out_specs=row_spec,
        out_shape=jax.ShapeDtypeStruct((B, S, D), F32),
        compiler_params=_cparams(2),
        name="gate_out_proj",
    )(o_da, o_sb, x, norm_g.reshape(1, D), scl, shift, gate, w_gate, w_out, final_g.reshape(1, D))


def _split_w_in(w):
    sizes = (DA_WIDTH, DA_WIDTH, DA_WIDTH, DA_WIDTH, SB_WIDTH, SB_WIDTH, SB_WIDTH, SB_WIDTH)
    q_da, k_da, v_da, g_da, q_sb, k_sb, v_sb, g_sb = jnp.split(w, np.cumsum(sizes)[:-1].tolist(), axis=1)
    w_tok = jnp.concatenate([k_da, k_sb], axis=1).astype(BF16)
    w_gate = jnp.concatenate([g_da, g_sb], axis=1).astype(BF16)
    q_scale = 1.0 / math.sqrt(DA_HEAD_DIM)
    w_feat_t = jnp.concatenate([q_da * (LOG2E * q_scale), v_da, q_sb * (0.5 * q_scale), v_sb], axis=1)
    return w_tok, w_feat_t.T.astype(BF16), w_gate


def kernel(x, c, norm_g, w_ada, b_ada, w_in, lambda_q1, lambda_k1, lambda_q2, lambda_k2, subln_g, w_out,
           rel_bias, final_g):
    B, S, D = x.shape
    depth = w_in.shape[0]
    assert S % ROW_TILE == 0 and ROW_TILE % TOK_BLOCK == 0 and S % DA_BLOCK == 0
    assert S % OUT_ROW_TILE == 0 and OUT_ROW_TILE % TOK_BLOCK == 0
    assert DA_HEAD_DIM == SB_HEAD_DIM and DA_WIDTH == SB_WIDTH and DA_WIDTH + SB_WIDTH == w_out.shape[1]
    bias = _bias_tiles(rel_bias)
    for l in range(depth):
        lambda_init = _lambda_init(l)
        mod, lam = _modulation(c, w_ada[l], b_ada[l], lambda_q1[l], lambda_k1[l], lambda_q2[l], lambda_k2[l],
                               lambda_init)
        shift, scl, gate = (m.reshape(B, 1, D) for m in jnp.split(mod, 3, axis=-1))
        w_tok, w_feat_t, w_gate = _split_w_in(w_in[l])
        k_da, k_sb, q_da, v_da, q_sb, v_sb = _project(x, norm_g[l], scl, shift, w_tok, w_feat_t)
        o_da = _diff_attention(q_da, k_da, v_da, bias, lam[:, :1], subln_g[l], lambda_init)
        o_sb = _sb_attention(q_sb, k_sb, v_sb)
        x = _out_project(o_da, o_sb, x, norm_g[l], scl, shift, gate, w_gate, w_out[l].astype(BF16), final_g,
                         l == depth - 1)
    return x
```

```python
import functools
import math

import numpy as np
import jax
import jax.numpy as jnp
from jax import lax
from jax.experimental import pallas as pl
from jax.experimental.pallas import tpu as pltpu

DA_HEADS = 4
DA_HEAD_DIM = 64
DA_V_DIM = 2 * DA_HEAD_DIM
DA_WIDTH = DA_HEADS * DA_V_DIM
SB_HEADS = 8
SB_HEAD_DIM = 64
SB_WIDTH = SB_HEADS * SB_HEAD_DIM
N_BUCKETS = 32
MAX_DISTANCE = 128
EPS = 1e-6
NEG_INF = -1e30

SUBLANES = 8
LANES = 128
TOK_BLOCK = 256
CHUNK = TOK_BLOCK // SUBLANES
DA_BLOCK = 2 * TOK_BLOCK
LOG2E = math.log2(math.e)
DIAG_TILE, NEAR_TILE = 0, 1
ROW_TILE = 1024
OUT_ROW_TILE = 1024
VMEM_LIMIT = 56 * 1024 * 1024

F32 = jnp.float32
BF16 = jnp.bfloat16


def _lambda_init(layer_idx):
    return 0.8 - 0.6 * math.exp(-0.3 * layer_idx)


def _block_token(pos):
    return (pos % SUBLANES) * CHUNK + pos // SUBLANES


def _bucket_tiles():
    pos = np.arange(DA_BLOCK)
    key_tok = (pos // TOK_BLOCK) * TOK_BLOCK + _block_token(pos % TOK_BLOCK)
    rel0 = pos[None, :] - key_tok[:, None]
    rel = np.stack([np.maximum(rel0, 0), rel0 + DA_BLOCK]).astype(np.int32)
    max_exact = N_BUCKETS // 2
    nf = np.maximum(rel, 1).astype(np.float32)
    large = max_exact + (np.log(nf / np.float32(max_exact)) / np.float32(math.log(MAX_DISTANCE / max_exact))
                         * np.float32(N_BUCKETS - max_exact)).astype(np.int32)
    large = np.minimum(large, N_BUCKETS - 1)
    bucket = np.where(rel < max_exact, rel, large).astype(np.int32)
    bucket[0] = np.where(rel0 < 0, N_BUCKETS, bucket[0])
    return bucket


def _position_to_token():
    pos = np.arange(TOK_BLOCK)
    m = np.zeros((TOK_BLOCK, TOK_BLOCK), np.float32)
    m[pos, _block_token(pos)] = 1.0
    return m


def _strict_causal_tile():
    pos = np.arange(TOK_BLOCK)
    return (_block_token(pos)[:, None] < pos[None, :]).astype(np.float32)


def _value_column_scale(n):
    g = (np.arange(n) % TOK_BLOCK) // SUBLANES
    return (2.0 ** -(CHUNK - g)).astype(np.float32).reshape(1, n)


def _cparams(n_axes):
    return pltpu.CompilerParams(dimension_semantics=("arbitrary",) * n_axes, vmem_limit_bytes=VMEM_LIMIT)


def _mod_kernel(c_ref, w_ref, b_ref, lq1_ref, lk1_ref, lq2_ref, lk2_ref, mod_ref, lam_ref, *, lambda_init):
    c = c_ref[...]
    silu_c = c / (1.0 + jnp.exp(-c))
    mod_ref[...] = jnp.dot(silu_c, w_ref[...], preferred_element_type=F32,
                           precision=lax.Precision.HIGHEST) + b_ref[...]
    s1 = jnp.sum(lq1_ref[...] * lk1_ref[...], axis=-1, keepdims=True)
    s2 = jnp.sum(lq2_ref[...] * lk2_ref[...], axis=-1, keepdims=True)
    lam_ref[...] = jnp.broadcast_to(jnp.exp(s1) - jnp.exp(s2) + lambda_init, lam_ref.shape)


def _modulation(c, w_ada, b_ada, lq1, lk1, lq2, lk2, lambda_init):
    B, D = c.shape
    n_out = w_ada.shape[1]
    col = 512
    vec = lambda: pl.BlockSpec((1, DA_HEAD_DIM), lambda j: (0, 0))
    return pl.pallas_call(
        functools.partial(_mod_kernel, lambda_init=lambda_init),
        grid=(n_out // col,),
        in_specs=[pl.BlockSpec((B, D), lambda j: (0, 0)),
                  pl.BlockSpec((D, col), lambda j: (0, j)),
                  pl.BlockSpec((1, col), lambda j: (0, j)),
                  vec(), vec(), vec(), vec()],
        out_specs=[pl.BlockSpec((B, col), lambda j: (0, j)),
                   pl.BlockSpec((1, LANES), lambda j: (0, 0))],
        out_shape=[jax.ShapeDtypeStruct((B, n_out), F32), jax.ShapeDtypeStruct((1, LANES), F32)],
        compiler_params=_cparams(1),
        name="adaln_mod",
    )(c, w_ada, b_ada.reshape(1, n_out), lq1.reshape(1, -1), lk1.reshape(1, -1), lq2.reshape(1, -1),
      lk2.reshape(1, -1))


def _bias_kernel(rb_ref, bucket_ref, out_ref):
    h = pl.program_id(0)
    far = rb_ref[N_BUCKETS - 1, h]
    for t in range(2):
        bucket = bucket_ref[t]
        tile = jnp.zeros(bucket.shape, F32)
        for b in range(N_BUCKETS - 1):
            tile = jnp.where(bucket == b, (rb_ref[b, h] - far) * LOG2E, tile)
        out_ref[0, t] = jnp.where(bucket == N_BUCKETS, NEG_INF, tile)


def _bias_tiles(rel_bias):
    buckets = _bucket_tiles()
    near_rest = buckets[NEAR_TILE].copy()
    near_rest[TOK_BLOCK:, :LANES] = N_BUCKETS - 1
    assert (near_rest == N_BUCKETS - 1).all() and (buckets[DIAG_TILE][TOK_BLOCK:, :TOK_BLOCK] == N_BUCKETS).all()
    buckets = jnp.asarray(buckets)
    return pl.pallas_call(
        _bias_kernel,
        grid=(DA_HEADS,),
        in_specs=[pl.BlockSpec(memory_space=pltpu.SMEM),
                  pl.BlockSpec((2, DA_BLOCK, DA_BLOCK), lambda h: (0, 0, 0))],
        out_specs=pl.BlockSpec((1, 2, DA_BLOCK, DA_BLOCK), lambda h: (h, 0, 0, 0)),
        out_shape=jax.ShapeDtypeStruct((DA_HEADS, 2, DA_BLOCK, DA_BLOCK), F32),
        compiler_params=_cparams(1),
        name="rel_bias_tiles",
    )(rel_bias, buckets)


def _modulated_norm(x, g, scl, shift):
    ms = jnp.mean(x * x, axis=-1, keepdims=True)
    xn = x * lax.rsqrt(ms + EPS) * g
    return (xn * (1.0 + scl) + shift).astype(BF16)


def _proj_kernel(x_ref, g_ref, scl_ref, shift_ref, wt_ref, wf_ref, vscale_ref, perm_ref,
                 kda_ref, ksb_ref, qda_ref, vda_ref, qsb_ref, vsb_ref):
    h = _modulated_norm(x_ref[0], g_ref[...], scl_ref[0], shift_ref[0])
    hp = jnp.concatenate(
        [jnp.dot(perm_ref[...], h[j * TOK_BLOCK:(j + 1) * TOK_BLOCK], preferred_element_type=F32).astype(BF16)
         for j in range(h.shape[0] // TOK_BLOCK)], axis=0)
    width = kda_ref.shape[-1]
    for i, o_ref in enumerate((kda_ref, ksb_ref)):
        o_ref[0] = jnp.dot(hp, wt_ref[:, i * width:(i + 1) * width], preferred_element_type=F32).astype(BF16)
    for i, (o_ref, src) in enumerate(((qda_ref, h), (vda_ref, hp), (qsb_ref, h), (vsb_ref, hp))):
        r = lax.dot_general(wf_ref[i * width:(i + 1) * width, :], src, (((1,), (1,)), ((), ())),
                            preferred_element_type=F32)
        if o_ref is vsb_ref:
            r = r * vscale_ref[...]
        r = r.astype(BF16)
        for t in range(o_ref.shape[1]):
            o_ref[0, t] = r[:, t * TOK_BLOCK:(t + 1) * TOK_BLOCK]


def _project(x, norm_g, scl, shift, w_tok, w_feat_t):
    B, S, D = x.shape
    n_blk = S // TOK_BLOCK
    blk_per_tile = ROW_TILE // TOK_BLOCK
    width = DA_WIDTH
    tok_spec = pl.BlockSpec((1, ROW_TILE, width), lambda b, i: (b, i, 0))
    feat_spec = pl.BlockSpec((1, blk_per_tile, width, TOK_BLOCK), lambda b, i: (b, i, 0, 0))
    tok_shape = jax.ShapeDtypeStruct((B, S, width), BF16)
    feat_shape = jax.ShapeDtypeStruct((B, n_blk, width, TOK_BLOCK), BF16)
    mod_spec = pl.BlockSpec((1, 1, D), lambda b, i: (b, 0, 0))
    return pl.pallas_call(
        _proj_kernel,
        grid=(B, S // ROW_TILE),
        in_specs=[pl.BlockSpec((1, ROW_TILE, D), lambda b, i: (b, i, 0)),
                  pl.BlockSpec((1, D), lambda b, i: (0, 0)),
                  mod_spec, mod_spec,
                  pl.BlockSpec(w_tok.shape, lambda b, i: (0, 0)),
                  pl.BlockSpec(w_feat_t.shape, lambda b, i: (0, 0)),
                  pl.BlockSpec((1, ROW_TILE), lambda b, i: (0, 0)),
                  pl.BlockSpec((TOK_BLOCK, TOK_BLOCK), lambda b, i: (0, 0))],
        out_specs=[tok_spec] * 2 + [feat_spec] * 4,
        out_shape=[tok_shape] * 2 + [feat_shape] * 4,
        compiler_params=_cparams(2),
        name="norm_in_proj",
    )(x, norm_g.reshape(1, D), scl, shift, w_tok, w_feat_t, jnp.asarray(_value_column_scale(ROW_TILE)),
      jnp.asarray(_position_to_token(), BF16))


def _store_row_groups(q_t, qz_ref, group):
    row = lax.broadcasted_iota(jnp.int32, q_t.shape, 0)
    for i in range(qz_ref.shape[0]):
        inside = (row >= i * group) & (row < (i + 1) * group)
        qz_ref[i] = jnp.where(inside, q_t, jnp.zeros_like(q_t))


def _key_block(k_ref, kj, rows=TOK_BLOCK):
    return k_ref[0, pl.ds(pl.multiple_of(kj * rows, rows), rows), :]


def _da_tile_passes(n_q):
    plain = [(qi, kj) for qi in range(n_q) for kj in range(qi - 1)]
    near = [(qi, qi - 1) for qi in range(1, n_q)]
    diag = [(qi, qi) for qi in range(n_q)]
    passes, start = [], 0
    for tiles, kind in ((plain, None), (near, NEAR_TILE), (diag, DIAG_TILE)):
        passes.append((start, len(tiles), kind))
        start += len(tiles)
    order = np.asarray(plain + near + diag, np.int32)
    return order[:, 0], order[:, 1], tuple(passes)


def _da_kernel(lam_ref, qs_ref, ks_ref, q_ref, k_ref, v_ref, bias_ref, g_ref, o_ref,
               acc_ref, m_ref, l_ref, qz_ref, s_ref, smax_ref, p_ref, a_ref, *, out_scale, passes):
    n_q, n_map = acc_ref.shape[0], acc_ref.shape[1]
    n_sub = DA_BLOCK // TOK_BLOCK
    acc_ref[...] = jnp.zeros(acc_ref.shape, F32)
    m_ref[...] = jnp.full(m_ref.shape, NEG_INF, F32)
    l_ref[...] = jnp.zeros(l_ref.shape, F32)

    for qi in range(n_q):
        q_t = jnp.concatenate([q_ref[0, n_sub * qi + j] for j in range(n_sub)], axis=-1)
        _store_row_groups(q_t, qz_ref.at[qi], DA_HEAD_DIM)

    def scores(qi, kj, maps=None):
        kb = _key_block(k_ref, kj, DA_BLOCK)
        return [jnp.dot(kb, qz_ref[qi, mp], preferred_element_type=F32)
                for mp in (range(n_map) if maps is None else maps)]

    half, lane0 = TOK_BLOCK, LANES

    def put_scores(ss, kind):
        for mp in range(n_map):
            s = ss[mp]
            hd = mp // 2
            if kind == NEAR_TILE:
                corner = s[half:, :lane0] + bias_ref[hd, kind, half:, :lane0]
                s = jnp.concatenate([s[:half], jnp.concatenate([corner, s[half:, lane0:]], axis=1)], axis=0)
                s_ref[mp] = s
                smax_ref[mp] = jnp.max(s, axis=0, keepdims=True)
            elif kind == DIAG_TILE:
                top = s[:half] + bias_ref[hd, kind, :half, :]
                right = s[half:, half:] + bias_ref[hd, kind, half:, half:]
                s_ref[mp, :half, :] = top
                s_ref[mp, half:, half:] = right
                top_max = jnp.max(top, axis=0, keepdims=True)
                smax_ref[mp] = jnp.concatenate(
                    [top_max[:, :half], jnp.maximum(top_max[:, half:], jnp.max(right, axis=0, keepdims=True))], axis=1)
            else:
                s_ref[mp] = s
                smax_ref[mp] = jnp.max(s, axis=0, keepdims=True)

    def softmax_step(qi, slot, kind):
        for mp in range(n_map):
            m_old = m_ref[qi, mp]
            m_new = jnp.maximum(m_old, smax_ref[mp])
            alpha = jnp.exp2(m_old - m_new)
            if kind == DIAG_TILE:
                p_top = jnp.exp2(s_ref[mp, :half, :] - m_new)
                p_right = jnp.exp2(s_ref[mp, half:, half:] - m_new[:, half:])
                top_sum = jnp.sum(p_top, axis=0, keepdims=True)
                p_sum = jnp.concatenate(
                    [top_sum[:, :half], top_sum[:, half:] + jnp.sum(p_right, axis=0, keepdims=True)], axis=1)
                p_ref[slot, mp, :half, :] = p_top.astype(BF16)
                p_ref[slot, mp, half:, :half] = jnp.zeros((half, half), BF16)
                p_ref[slot, mp, half:, half:] = p_right.astype(BF16)
            else:
                p = jnp.exp2(s_ref[mp] - m_new)
                p_sum = jnp.sum(p, axis=0, keepdims=True)
                p_ref[slot, mp] = p.astype(BF16)
            l_ref[qi, mp] = alpha * l_ref[qi, mp] + p_sum
            m_ref[qi, mp] = m_new
            a_ref[mp] = alpha

    def weighted_values(kj, slot, maps=None):
        out = []
        for mp in (range(n_map) if maps is None else maps):
            pv = None
            for j in range(n_sub):
                vh = v_ref[0, n_sub * kj + j, (mp // 2) * DA_V_DIM:(mp // 2 + 1) * DA_V_DIM, :]
                part = jnp.dot(vh, p_ref[slot, mp, j * TOK_BLOCK:(j + 1) * TOK_BLOCK, :], preferred_element_type=F32)
                pv = part if pv is None else pv + part
            out.append(pv)
        return out

    def add_weighted(pv, qi):
        for mp in range(n_map):
            acc_ref[qi, mp] = a_ref[mp] * acc_ref[qi, mp] + pv[mp]

    for start, count, kind in passes:
        if count == 0:
            continue
        last = start + count - 1
        put_scores(scores(qs_ref[start], ks_ref[start]), kind)
        second = min(start + 1, last)
        next_scores = scores(qs_ref[second], ks_ref[second])
        softmax_step(qs_ref[start], start % 2, kind)
        put_scores(next_scores, kind)

        def one_step(cur, cur_slot, last=last, kind=kind):
            prev = cur - 1
            nxt = jnp.minimum(cur + 1, last)
            pv, next_scores = [], []
            for mp in range(n_map):
                pv += weighted_values(ks_ref[prev], 1 - cur_slot, (mp,))
                next_scores += scores(qs_ref[nxt], ks_ref[nxt], (mp,))
            add_weighted(pv, qs_ref[prev])
            softmax_step(qs_ref[cur], cur_slot, kind)
            put_scores(next_scores, kind)

        rest = count - 1
        first_slot = (start + 1) % 2

        def two_steps(j, carry, start=start, first_slot=first_slot, one_step=one_step):
            cur = start + 1 + 2 * j
            one_step(cur, first_slot)
            one_step(cur + 1, 1 - first_slot)
            return carry

        lax.fori_loop(0, rest // 2, two_steps, 0)
        if rest % 2:
            one_step(last, last % 2)
        add_weighted(weighted_values(ks_ref[last], last % 2), qs_ref[last])

    lam = lam_ref[0, 0]

    def finish(qi, carry):
        outs = []
        for hd in range(n_map // 2):
            o_t = (acc_ref[qi, 2 * hd] / l_ref[qi, 2 * hd]
                   - lam * (acc_ref[qi, 2 * hd + 1] / l_ref[qi, 2 * hd + 1]))
            o = o_t.T
            o = o * lax.rsqrt(jnp.mean(o * o, axis=-1, keepdims=True) + EPS) * g_ref[...]
            outs.append((o * out_scale).astype(o_ref.dtype))
        o_ref[0, pl.ds(pl.multiple_of(qi * DA_BLOCK, DA_BLOCK), DA_BLOCK), :] = jnp.concatenate(outs, axis=-1)
        return carry

    lax.fori_loop(0, n_q, finish, 0)


DA_GROUP = 2


def _diff_attention(q_t, k, v_t, bias, lam, subln_g, lambda_init):
    B, S, _ = k.shape
    n_blk = S // TOK_BLOCK
    n_q = S // DA_BLOCK
    nhd = DA_GROUP
    n_map = 2 * nhd
    wide = nhd * DA_V_DIM
    qs, ks, passes = _da_tile_passes(n_q)
    smem = pl.BlockSpec(memory_space=pltpu.SMEM)
    return pl.pallas_call(
        functools.partial(_da_kernel, out_scale=1.0 - lambda_init, passes=passes),
        grid=(B, DA_HEADS // nhd),
        in_specs=[smem, smem, smem,
                  pl.BlockSpec((1, n_blk, wide, TOK_BLOCK), lambda b, h: (b, 0, h, 0)),
                  pl.BlockSpec((1, S, wide), lambda b, h: (b, 0, h)),
                  pl.BlockSpec((1, n_blk, wide, TOK_BLOCK), lambda b, h: (b, 0, h, 0)),
                  pl.BlockSpec((nhd, 2, DA_BLOCK, DA_BLOCK), lambda b, h: (h, 0, 0, 0)),
                  pl.BlockSpec((1, DA_V_DIM), lambda b, h: (0, 0))],
        out_specs=pl.BlockSpec((1, S, wide), lambda b, h: (b, 0, h)),
        out_shape=jax.ShapeDtypeStruct((B, S, DA_WIDTH), BF16),
        scratch_shapes=[pltpu.VMEM((n_q, n_map, DA_V_DIM, DA_BLOCK), F32),
                        pltpu.VMEM((n_q, n_map, 1, DA_BLOCK), F32),
                        pltpu.VMEM((n_q, n_map, 1, DA_BLOCK), F32),
                        pltpu.VMEM((n_q, n_map, wide, DA_BLOCK), BF16),
                        pltpu.VMEM((n_map, DA_BLOCK, DA_BLOCK), F32),
                        pltpu.VMEM((n_map, 1, DA_BLOCK), F32),
                        pltpu.VMEM((2, n_map, DA_BLOCK, DA_BLOCK), BF16),
                        pltpu.VMEM((n_map, 1, DA_BLOCK), F32)],
        compiler_params=_cparams(2),
        name="diff_attention",
    )(lam, jnp.asarray(qs), jnp.asarray(ks), q_t, k, v_t, bias, subln_g.reshape(1, DA_V_DIM))


def _sb_kernel(q_ref, k_ref, v_ref, mask_ref, o_ref, acc_ref, carry_ref, qz_ref, u_ref, tot_ref, w_ref):
    qi = pl.program_id(2)
    nh = acc_ref.shape[0]
    slab = qz_ref.shape[1]
    slab_heads = slab // SB_HEAD_DIM
    for sl in range(nh // slab_heads):
        _store_row_groups(q_ref[0, 0, sl * slab:(sl + 1) * slab, :], qz_ref.at[pl.ds(sl * slab_heads, slab_heads)],
                          SB_HEAD_DIM)
    carry_ref[...] = jnp.ones(carry_ref.shape, F32)
    sub = lax.broadcasted_iota(jnp.int32, (SUBLANES, TOK_BLOCK), 0)
    rows = lambda x, g: x[g * SUBLANES:(g + 1) * SUBLANES]

    def logits(kj):
        kb = _key_block(k_ref, kj)
        return [jnp.dot(kb[:, (hh // slab_heads) * slab:(hh // slab_heads + 1) * slab], qz_ref[hh],
                        preferred_element_type=F32) for hh in range(nh)]

    def gates(zs, slot, mask=None):
        for hh in range(nh):
            t = jnp.tanh(zs[hh])
            if mask is not None:
                t = jnp.where(mask > 0.5, t, -1.0)
            u = 1.0 - t
            u_ref[slot, hh] = u
            tot = rows(u, CHUNK - 1)
            for g in reversed(range(CHUNK - 1)):
                tot = tot * rows(u, g)
            tot_ref[slot, hh] = tot * 2.0 ** -CHUNK

    def weights(slot, keep_carry=None):
        for hh in range(nh):
            suffix = tot_ref[slot, hh]
            for d in (1, 2, 4):
                shifted = pltpu.roll(suffix, SUBLANES - d, axis=0)
                suffix = jnp.where(sub + d < SUBLANES, suffix * shifted, suffix)
            above = jnp.where(sub + 1 < SUBLANES, pltpu.roll(suffix, SUBLANES - 1, axis=0), 1.0)
            carry = carry_ref[hh]
            new_carry = carry * jnp.broadcast_to(suffix[0:1], carry.shape)
            carry_ref[hh] = new_carry if keep_carry is None else new_carry * keep_carry
            run = above * carry
            w = [None] * CHUNK
            for g in reversed(range(CHUNK)):
                ug = u_ref[slot, hh, g * SUBLANES:(g + 1) * SUBLANES, :]
                w[g] = (2.0 - ug) * run
                run = run * ug
            w_ref[slot, hh] = jnp.concatenate(w, axis=0).astype(BF16)

    def weighted_values(kj, slot):
        vb = v_ref[0, kj]
        return [jnp.dot(vb[hh * SB_HEAD_DIM:(hh + 1) * SB_HEAD_DIM], w_ref[slot, hh], preferred_element_type=F32)
                for hh in range(nh)]

    def alive():
        return (jnp.max(carry_ref[...]) > 0.0).astype(jnp.int32)

    before = jnp.maximum(qi - 1, 0)
    z_diag = logits(qi)
    z_before = logits(before)
    gates(z_diag, 0, mask_ref[...])
    gates(z_before, 1)
    weights(0, keep_carry=(qi > 0).astype(F32))
    weights(1)
    pv_diag = weighted_values(qi, 0)
    pv_before = weighted_values(before, 1)
    for hh in range(nh):
        acc_ref[hh] = pv_diag[hh] + pv_before[hh]

    more = jnp.logical_and(qi >= 2, alive() > 0)
    pl.when(more)(lambda: gates(logits(jnp.maximum(qi - 2, 0)), 0))

    def left_cond(state):
        i, live = state
        return jnp.logical_and(i < qi - 1, live > 0)

    def left_body(state):
        i, _ = state
        kj = qi - 2 - i
        nxt = logits(jnp.maximum(kj - 1, 0))
        weights(0)
        pv = weighted_values(kj, 0)
        for hh in range(nh):
            acc_ref[hh] = acc_ref[hh] + pv[hh]
        gates(nxt, 0)
        return i + 1, alive()

    lax.while_loop(left_cond, left_body, (jnp.int32(0), more.astype(jnp.int32)))

    o_t = jnp.concatenate([acc_ref[hh] for hh in range(nh)], axis=0)
    o_ref[0] = o_t.T.astype(o_ref.dtype)


SB_GROUP = 8


def _sb_attention(q_t, k, v_t):
    B, S, _ = k.shape
    n_blk = S // TOK_BLOCK
    nh = SB_GROUP
    pair = nh * SB_HEAD_DIM
    return pl.pallas_call(
        _sb_kernel,
        grid=(B, SB_HEADS // nh, n_blk),
        in_specs=[pl.BlockSpec((1, 1, pair, TOK_BLOCK), lambda b, h, i: (b, i, h, 0)),
                  pl.BlockSpec((1, S, pair), lambda b, h, i: (b, 0, h)),
                  pl.BlockSpec((1, n_blk, pair, TOK_BLOCK), lambda b, h, i: (b, 0, h, 0)),
                  pl.BlockSpec((TOK_BLOCK, TOK_BLOCK), lambda b, h, i: (0, 0))],
        out_specs=pl.BlockSpec((1, TOK_BLOCK, pair), lambda b, h, i: (b, i, h)),
        out_shape=jax.ShapeDtypeStruct((B, S, SB_WIDTH), BF16),
        scratch_shapes=[pltpu.VMEM((nh, SB_HEAD_DIM, TOK_BLOCK), F32),
                        pltpu.VMEM((nh, SUBLANES, TOK_BLOCK), F32),
                        pltpu.VMEM((nh, TOK_BLOCK, TOK_BLOCK), BF16),
                        pltpu.VMEM((2, nh, TOK_BLOCK, TOK_BLOCK), F32),
                        pltpu.VMEM((2, nh, SUBLANES, TOK_BLOCK), F32),
                        pltpu.VMEM((2, nh, TOK_BLOCK, TOK_BLOCK), BF16)],
        compiler_params=_cparams(3),
        name="stick_breaking_attention",
    )(q_t, k, v_t, jnp.asarray(_strict_causal_tile()))


def _out_kernel(oda_ref, osb_ref, x_ref, g_ref, scl_ref, shift_ref, gate_ref, wg_ref, w_ref, fg_ref, out_ref, *,
                final_norm):
    x = x_ref[0]
    h = _modulated_norm(x, g_ref[...], scl_ref[0], shift_ref[0])
    split = oda_ref.shape[-1]

    def gated(o_ref, cols):
        g = jnp.dot(h, wg_ref[:, cols], preferred_element_type=F32)
        return (o_ref[0].astype(F32) * (g / (1.0 + jnp.exp(-g)))).astype(BF16)

    y = jnp.dot(gated(oda_ref, slice(0, split)), w_ref[0:split, :], preferred_element_type=F32)
    y = y + jnp.dot(gated(osb_ref, slice(split, 2 * split)), w_ref[split:, :], preferred_element_type=F32)
    xo = x + gate_ref[0] * y
    if final_norm:
        xo = xo * lax.rsqrt(jnp.mean(xo * xo, axis=-1, keepdims=True) + EPS) * fg_ref[...]
    out_ref[0] = xo


def _out_project(o_da, o_sb, x, norm_g, scl, shift, gate, w_gate, w_out, final_g, final_norm):
    B, S, D = x.shape
    half_spec = pl.BlockSpec((1, OUT_ROW_TILE, DA_WIDTH), lambda b, i: (b, i, 0))
    row_spec = pl.BlockSpec((1, OUT_ROW_TILE, D), lambda b, i: (b, i, 0))
    vec_spec = pl.BlockSpec((1, D), lambda b, i: (0, 0))
    mod_spec = pl.BlockSpec((1, 1, D), lambda b, i: (b, 0, 0))
    return pl.pallas_call(
        functools.partial(_out_kernel, final_norm=final_norm),
        grid=(B, S // OUT_ROW_TILE),
        in_specs=[half_spec, half_spec, row_spec, vec_spec, mod_spec, mod_spec, mod_spec,
                  pl.BlockSpec(w_gate.shape, lambda b, i: (0, 0)),
                  pl.BlockSpec(w_out.shape, lambda b, i: (0, 0)),
                  vec_spec],
        out_specs=row_spec,
        out_shape=jax.ShapeDtypeStruct((B, S, D), F32),
        compiler_params=_cparams(2),
        name="gate_out_proj",
    )(o_da, o_sb, x, norm_g.reshape(1, D), scl, shift, gate, w_gate, w_out, final_g.reshape(1, D))


def _split_w_in(w):
    sizes = (DA_WIDTH, DA_WIDTH, DA_WIDTH, DA_WIDTH, SB_WIDTH, SB_WIDTH, SB_WIDTH, SB_WIDTH)
    q_da, k_da, v_da, g_da, q_sb, k_sb, v_sb, g_sb = jnp.split(w, np.cumsum(sizes)[:-1].tolist(), axis=1)
    w_tok = jnp.concatenate([k_da, k_sb], axis=1).astype(BF16)
    w_gate = jnp.concatenate([g_da, g_sb], axis=1).astype(BF16)
    q_scale = 1.0 / math.sqrt(DA_HEAD_DIM)
    w_feat_t = jnp.concatenate([q_da * (LOG2E * q_scale), v_da, q_sb * (0.5 * q_scale), v_sb], axis=1)
    return w_tok, w_feat_t.T.astype(BF16), w_gate


def kernel(x, c, norm_g, w_ada, b_ada, w_in, lambda_q1, lambda_k1, lambda_q2, lambda_k2, subln_g, w_out,
           rel_bias, final_g):
    B, S, D = x.shape
    depth = w_in.shape[0]
    assert S % ROW_TILE == 0 and ROW_TILE % TOK_BLOCK == 0 and S % DA_BLOCK == 0
    assert S % OUT_ROW_TILE == 0 and OUT_ROW_TILE % TOK_BLOCK == 0
    assert DA_HEAD_DIM == SB_HEAD_DIM and DA_WIDTH == SB_WIDTH and DA_WIDTH + SB_WIDTH == w_out.shape[1]
    bias = _bias_tiles(rel_bias)
    for l in range(depth):
        lambda_init = _lambda_init(l)
        mod, lam = _modulation(c, w_ada[l], b_ada[l], lambda_q1[l], lambda_k1[l], lambda_q2[l], lambda_k2[l],
                               lambda_init)
        shift, scl, gate = (m.reshape(B, 1, D) for m in jnp.split(mod, 3, axis=-1))
        w_tok, w_feat_t, w_gate = _split_w_in(w_in[l])
        k_da, k_sb, q_da, v_da, q_sb, v_sb = _project(x, norm_g[l], scl, shift, w_tok, w_feat_t)
        o_da = _diff_attention(q_da, k_da, v_da, bias, lam[:, :1], subln_g[l], lambda_init)
        o_sb = _sb_attention(q_sb, k_sb, v_sb)
        x = _out_project(o_da, o_sb, x, norm_g[l], scl, shift, gate, w_gate, w_out[l].astype(BF16), final_g,
                         l == depth - 1)
    return x
```

```python
import functools
import math

import numpy as np
import jax
import jax.numpy as jnp
from jax import lax
from jax.experimental import pallas as pl
from jax.experimental.pallas import tpu as pltpu

DA_HEADS = 4
DA_HEAD_DIM = 64
DA_V_DIM = 2 * DA_HEAD_DIM
DA_WIDTH = DA_HEADS * DA_V_DIM
SB_HEADS = 8
SB_HEAD_DIM = 64
SB_WIDTH = SB_HEADS * SB_HEAD_DIM
N_BUCKETS = 32
MAX_DISTANCE = 128
EPS = 1e-6
NEG_INF = -1e30

SUBLANES = 8
LANES = 128
TOK_BLOCK = 256
CHUNK = TOK_BLOCK // SUBLANES
DA_BLOCK = 2 * TOK_BLOCK
LOG2E = math.log2(math.e)
DIAG_TILE, NEAR_TILE = 0, 1
ROW_TILE = 1024
OUT_ROW_TILE = 1024
VMEM_LIMIT = 56 * 1024 * 1024

F32 = jnp.float32
BF16 = jnp.bfloat16


def _lambda_init(layer_idx):
    return 0.8 - 0.6 * math.exp(-0.3 * layer_idx)


def _block_token(pos):
    return (pos % SUBLANES) * CHUNK + pos // SUBLANES


def _bucket_tiles():
    pos = np.arange(DA_BLOCK)
    key_tok = (pos // TOK_BLOCK) * TOK_BLOCK + _block_token(pos % TOK_BLOCK)
    rel0 = pos[None, :] - key_tok[:, None]
    rel = np.stack([np.maximum(rel0, 0), rel0 + DA_BLOCK]).astype(np.int32)
    max_exact = N_BUCKETS // 2
    nf = np.maximum(rel, 1).astype(np.float32)
    large = max_exact + (np.log(nf / np.float32(max_exact)) / np.float32(math.log(MAX_DISTANCE / max_exact))
                         * np.float32(N_BUCKETS - max_exact)).astype(np.int32)
    large = np.minimum(large, N_BUCKETS - 1)
    bucket = np.where(rel < max_exact, rel, large).astype(np.int32)
    bucket[0] = np.where(rel0 < 0, N_BUCKETS, bucket[0])
    return bucket


def _position_to_token():
    pos = np.arange(TOK_BLOCK)
    m = np.zeros((TOK_BLOCK, TOK_BLOCK), np.float32)
    m[pos, _block_token(pos)] = 1.0
    return m


def _strict_causal_tile():
    pos = np.arange(TOK_BLOCK)
    return (_block_token(pos)[:, None] < pos[None, :]).astype(np.float32)


def _value_column_scale(n):
    g = (np.arange(n) % TOK_BLOCK) // SUBLANES
    return (2.0 ** -(CHUNK - g)).astype(np.float32).reshape(1, n)


def _cparams(n_axes):
    return pltpu.CompilerParams(dimension_semantics=("arbitrary",) * n_axes, vmem_limit_bytes=VMEM_LIMIT)


def _mod_kernel(c_ref, w_ref, b_ref, lq1_ref, lk1_ref, lq2_ref, lk2_ref, mod_ref, lam_ref, *, lambda_init):
    c = c_ref[...]
    silu_c = c / (1.0 + jnp.exp(-c))
    mod_ref[...] = jnp.dot(silu_c, w_ref[...], preferred_element_type=F32,
                           precision=lax.Precision.HIGHEST) + b_ref[...]
    s1 = jnp.sum(lq1_ref[...] * lk1_ref[...], axis=-1, keepdims=True)
    s2 = jnp.sum(lq2_ref[...] * lk2_ref[...], axis=-1, keepdims=True)
    lam_ref[...] = jnp.broadcast_to(jnp.exp(s1) - jnp.exp(s2) + lambda_init, lam_ref.shape)


def _modulation(c, w_ada, b_ada, lq1, lk1, lq2, lk2, lambda_init):
    B, D = c.shape
    n_out = w_ada.shape[1]
    col = 512
    vec = lambda: pl.BlockSpec((1, DA_HEAD_DIM), lambda j: (0, 0))
    return pl.pallas_call(
        functools.partial(_mod_kernel, lambda_init=lambda_init),
        grid=(n_out // col,),
        in_specs=[pl.BlockSpec((B, D), lambda j: (0, 0)),
                  pl.BlockSpec((D, col), lambda j: (0, j)),
                  pl.BlockSpec((1, col), lambda j: (0, j)),
                  vec(), vec(), vec(), vec()],
        out_specs=[pl.BlockSpec((B, col), lambda j: (0, j)),
                   pl.BlockSpec((1, LANES), lambda j: (0, 0))],
        out_shape=[jax.ShapeDtypeStruct((B, n_out), F32), jax.ShapeDtypeStruct((1, LANES), F32)],
        compiler_params=_cparams(1),
        name="adaln_mod",
    )(c, w_ada, b_ada.reshape(1, n_out), lq1.reshape(1, -1), lk1.reshape(1, -1), lq2.reshape(1, -1),
      lk2.reshape(1, -1))


def _bias_kernel(rb_ref, bucket_ref, out_ref):
    h = pl.program_id(0)
    far = rb_ref[N_BUCKETS - 1, h]
    for t in range(2):
        bucket = bucket_ref[t]
        tile = jnp.zeros(bucket.shape, F32)
        for b in range(N_BUCKETS - 1):
            tile = jnp.where(bucket == b, (rb_ref[b, h] - far) * LOG2E, tile)
        out_ref[0, t] = jnp.where(bucket == N_BUCKETS, NEG_INF, tile)


def _bias_tiles(rel_bias):
    buckets = _bucket_tiles()
    near_rest = buckets[NEAR_TILE].copy()
    near_rest[TOK_BLOCK:, :LANES] = N_BUCKETS - 1
    assert (near_rest == N_BUCKETS - 1).all() and (buckets[DIAG_TILE][TOK_BLOCK:, :TOK_BLOCK] == N_BUCKETS).all()
    buckets = jnp.asarray(buckets)
    return pl.pallas_call(
        _bias_kernel,
        grid=(DA_HEADS,),
        in_specs=[pl.BlockSpec(memory_space=pltpu.SMEM),
                  pl.BlockSpec((2, DA_BLOCK, DA_BLOCK), lambda h: (0, 0, 0))],
        out_specs=pl.BlockSpec((1, 2, DA_BLOCK, DA_BLOCK), lambda h: (h, 0, 0, 0)),
        out_shape=jax.ShapeDtypeStruct((DA_HEADS, 2, DA_BLOCK, DA_BLOCK), F32),
        compiler_params=_cparams(1),
        name="rel_bias_tiles",
    )(rel_bias, buckets)


def _modulated_norm(x, g, scl, shift):
    ms = jnp.mean(x * x, axis=-1, keepdims=True)
    xn = x * lax.rsqrt(ms + EPS) * g
    return (xn * (1.0 + scl) + shift).astype(BF16)


def _proj_kernel(x_ref, g_ref, scl_ref, shift_ref, wt_ref, wf_ref, vscale_ref, perm_ref,
                 kda_ref, ksb_ref, qda_ref, vda_ref, qsb_ref, vsb_ref):
    h = _modulated_norm(x_ref[0], g_ref[...], scl_ref[0], shift_ref[0])
    hp = jnp.concatenate(
        [jnp.dot(perm_ref[...], h[j * TOK_BLOCK:(j + 1) * TOK_BLOCK], preferred_element_type=F32).astype(BF16)
         for j in range(h.shape[0] // TOK_BLOCK)], axis=0)
    width = kda_ref.shape[-1]
    for i, o_ref in enumerate((kda_ref, ksb_ref)):
        o_ref[0] = jnp.dot(hp, wt_ref[:, i * width:(i + 1) * width], preferred_element_type=F32).astype(BF16)
    for i, (o_ref, src) in enumerate(((qda_ref, h), (vda_ref, hp), (qsb_ref, h), (vsb_ref, hp))):
        r = lax.dot_general(wf_ref[i * width:(i + 1) * width, :], src, (((1,), (1,)), ((), ())),
                            preferred_element_type=F32)
        if o_ref is vsb_ref:
            r = r * vscale_ref[...]
        r = r.astype(BF16)
        for t in range(o_ref.shape[1]):
            o_ref[0, t] = r[:, t * TOK_BLOCK:(t + 1) * TOK_BLOCK]


def _project(x, norm_g, scl, shift, w_tok, w_feat_t):
    B, S, D = x.shape
    n_blk = S // TOK_BLOCK
    blk_per_tile = ROW_TILE // TOK_BLOCK
    width = DA_WIDTH
    tok_spec = pl.BlockSpec((1, ROW_TILE, width), lambda b, i: (b, i, 0))
    feat_spec = pl.BlockSpec((1, blk_per_tile, width, TOK_BLOCK), lambda b, i: (b, i, 0, 0))
    tok_shape = jax.ShapeDtypeStruct((B, S, width), BF16)
    feat_shape = jax.ShapeDtypeStruct((B, n_blk, width, TOK_BLOCK), BF16)
    mod_spec = pl.BlockSpec((1, 1, D), lambda b, i: (b, 0, 0))
    return pl.pallas_call(
        _proj_kernel,
        grid=(B, S // ROW_TILE),
        in_specs=[pl.BlockSpec((1, ROW_TILE, D), lambda b, i: (b, i, 0)),
                  pl.BlockSpec((1, D), lambda b, i: (0, 0)),
                  mod_spec, mod_spec,
                  pl.BlockSpec(w_tok.shape, lambda b, i: (0, 0)),
                  pl.BlockSpec(w_feat_t.shape, lambda b, i: (0, 0)),
                  pl.BlockSpec((1, ROW_TILE), lambda b, i: (0, 0)),
                  pl.BlockSpec((TOK_BLOCK, TOK_BLOCK), lambda b, i: (0, 0))],
        out_specs=[tok_spec] * 2 + [feat_spec] * 4,
        out_shape=[tok_shape] * 2 + [feat_shape] * 4,
        compiler_params=_cparams(2),
        name="norm_in_proj",
    )(x, norm_g.reshape(1, D), scl, shift, w_tok, w_feat_t, jnp.asarray(_value_column_scale(ROW_TILE)),
      jnp.asarray(_position_to_token(), BF16))


def _store_row_groups(q_t, qz_ref, group):
    row = lax.broadcasted_iota(jnp.int32, q_t.shape, 0)
    for i in range(qz_ref.shape[0]):
        inside = (row >= i * group) & (row < (i + 1) * group)
        qz_ref[i] = jnp.where(inside, q_t, jnp.zeros_like(q_t))


def _key_block(k_ref, kj, rows=TOK_BLOCK):
    return k_ref[0, pl.ds(pl.multiple_of(kj * rows, rows), rows), :]


def _da_tile_passes(n_q):
    plain = [(qi, kj) for qi in range(n_q) for kj in range(qi - 1)]
    near = [(qi, qi - 1) for qi in range(1, n_q)]
    diag = [(qi, qi) for qi in range(n_q)]
    passes, start = [], 0
    for tiles, kind in ((plain, None), (near, NEAR_TILE), (diag, DIAG_TILE)):
        passes.append((start, len(tiles), kind))
        start += len(tiles)
    order = np.asarray(plain + near + diag, np.int32)
    return order[:, 0], order[:, 1], tuple(passes)


def _da_kernel(lam_ref, qs_ref, ks_ref, q_ref, k_ref, v_ref, bias_ref, g_ref, o_ref,
               acc_ref, m_ref, l_ref, qz_ref, s_ref, smax_ref, p_ref, a_ref, *, out_scale, passes):
    n_q, n_map = acc_ref.shape[0], acc_ref.shape[1]
    n_sub = DA_BLOCK // TOK_BLOCK
    acc_ref[...] = jnp.zeros(acc_ref.shape, F32)
    m_ref[...] = jnp.full(m_ref.shape, NEG_INF, F32)
    l_ref[...] = jnp.zeros(l_ref.shape, F32)

    for qi in range(n_q):
        q_t = jnp.concatenate([q_ref[0, n_sub * qi + j] for j in range(n_sub)], axis=-1)
        _store_row_groups(q_t, qz_ref.at[qi], DA_HEAD_DIM)

    def scores(qi, kj, maps=None):
        kb = _key_block(k_ref, kj, DA_BLOCK)
        return [jnp.dot(kb, qz_ref[qi, mp], preferred_element_type=F32)
                for mp in (range(n_map) if maps is None else maps)]

    half, lane0 = TOK_BLOCK, LANES

    def put_scores(ss, kind):
        for mp in range(n_map):
            s = ss[mp]
            hd = mp // 2
            if kind == NEAR_TILE:
                corner = s[half:, :lane0] + bias_ref[hd, kind, half:, :lane0]
                s = jnp.concatenate([s[:half], jnp.concatenate([corner, s[half:, lane0:]], axis=1)], axis=0)
                s_ref[mp] = s
                smax_ref[mp] = jnp.max(s, axis=0, keepdims=True)
            elif kind == DIAG_TILE:
                top = s[:half] + bias_ref[hd, kind, :half, :]
                right = s[half:, half:] + bias_ref[hd, kind, half:, half:]
                s_ref[mp, :half, :] = top
                s_ref[mp, half:, half:] = right
                top_max = jnp.max(top, axis=0, keepdims=True)
                smax_ref[mp] = jnp.concatenate(
                    [top_max[:, :half], jnp.maximum(top_max[:, half:], jnp.max(right, axis=0, keepdims=True))], axis=1)
            else:
                s_ref[mp] = s
                smax_ref[mp] = jnp.max(s, axis=0, keepdims=True)

    def softmax_step(qi, slot, kind):
        for mp in range(n_map):
            m_old = m_ref[qi, mp]
            m_new = jnp.maximum(m_old, smax_ref[mp])
            alpha = jnp.exp2(m_old - m_new)
            if kind == DIAG_TILE:
                p_top = jnp.exp2(s_ref[mp, :half, :] - m_new)
                p_right = jnp.exp2(s_ref[mp, half:, half:] - m_new[:, half:])
                top_sum = jnp.sum(p_top, axis=0, keepdims=True)
                p_sum = jnp.concatenate(
                    [top_sum[:, :half], top_sum[:, half:] + jnp.sum(p_right, axis=0, keepdims=True)], axis=1)
                p_ref[slot, mp, :half, :] = p_top.astype(BF16)
                p_ref[slot, mp, half:, :half] = jnp.zeros((half, half), BF16)
                p_ref[slot, mp, half:, half:] = p_right.astype(BF16)
            else:
                p = jnp.exp2(s_ref[mp] - m_new)
                p_sum = jnp.sum(p, axis=0, keepdims=True)
                p_ref[slot, mp] = p.astype(BF16)
            l_ref[qi, mp] = alpha * l_ref[qi, mp] + p_sum
            m_ref[qi, mp] = m_new
            a_ref[mp] = alpha

    def weighted_values(kj, slot, maps=None):
        out = []
        for mp in (range(n_map) if maps is None else maps):
            pv = None
            for j in range(n_sub):
                vh = v_ref[0, n_sub * kj + j, (mp // 2) * DA_V_DIM:(mp // 2 + 1) * DA_V_DIM, :]
                part = jnp.dot(vh, p_ref[slot, mp, j * TOK_BLOCK:(j + 1) * TOK_BLOCK, :], preferred_element_type=F32)
                pv = part if pv is None else pv + part
            out.append(pv)
        return out

    def add_weighted(pv, qi):
        for mp in range(n_map):
            acc_ref[qi, mp] = a_ref[mp] * acc_ref[qi, mp] + pv[mp]

    for start, count, kind in passes:
        if count == 0:
            continue
        last = start + count - 1
        put_scores(scores(qs_ref[start], ks_ref[start]), kind)
        second = min(start + 1, last)
        next_scores = scores(qs_ref[second], ks_ref[second])
        softmax_step(qs_ref[start], start % 2, kind)
        put_scores(next_scores, kind)

        def one_step(cur, cur_slot, last=last, kind=kind):
            prev = cur - 1
            nxt = jnp.minimum(cur + 1, last)
            pv, next_scores = [], []
            for mp in range(n_map):
                pv += weighted_values(ks_ref[prev], 1 - cur_slot, (mp,))
                next_scores += scores(qs_ref[nxt], ks_ref[nxt], (mp,))
            add_weighted(pv, qs_ref[prev])
            softmax_step(qs_ref[cur], cur_slot, kind)
            put_scores(next_scores, kind)

        rest = count - 1
        first_slot = (start + 1) % 2

        def two_steps(j, carry, start=start, first_slot=first_slot, one_step=one_step):
            cur = start + 1 + 2 * j
            one_step(cur, first_slot)
            one_step(cur + 1, 1 - first_slot)
            return carry

        lax.fori_loop(0, rest // 2, two_steps, 0)
        if rest % 2:
            one_step(last, last % 2)
        add_weighted(weighted_values(ks_ref[last], last % 2), qs_ref[last])

    lam = lam_ref[0, 0]

    gain = jnp.concatenate([g_ref[...]] * (DA_BLOCK // LANES), axis=1) * out_scale

    def finish(qi, carry):
        for hd in range(n_map // 2):
            o_t = (acc_ref[qi, 2 * hd] / l_ref[qi, 2 * hd]
                   - lam * (acc_ref[qi, 2 * hd + 1] / l_ref[qi, 2 * hd + 1]))
            o_t = o_t * lax.rsqrt(jnp.mean(o_t * o_t, axis=0, keepdims=True) + EPS) * gain
            o_ref[0, hd * DA_V_DIM:(hd + 1) * DA_V_DIM,
                  pl.ds(pl.multiple_of(qi * DA_BLOCK, DA_BLOCK), DA_BLOCK)] = o_t.astype(o_ref.dtype)
        return carry

    lax.fori_loop(0, n_q, finish, 0)


DA_GROUP = 2


def _diff_attention(q_t, k, v_t, bias, lam, subln_g, lambda_init):
    B, S, _ = k.shape
    n_blk = S // TOK_BLOCK
    n_q = S // DA_BLOCK
    nhd = DA_GROUP
    n_map = 2 * nhd
    wide = nhd * DA_V_DIM
    qs, ks, passes = _da_tile_passes(n_q)
    smem = pl.BlockSpec(memory_space=pltpu.SMEM)
    return pl.pallas_call(
        functools.partial(_da_kernel, out_scale=1.0 - lambda_init, passes=passes),
        grid=(B, DA_HEADS // nhd),
        in_specs=[smem, smem, smem,
                  pl.BlockSpec((1, n_blk, wide, TOK_BLOCK), lambda b, h: (b, 0, h, 0)),
                  pl.BlockSpec((1, S, wide), lambda b, h: (b, 0, h)),
                  pl.BlockSpec((1, n_blk, wide, TOK_BLOCK), lambda b, h: (b, 0, h, 0)),
                  pl.BlockSpec((nhd, 2, DA_BLOCK, DA_BLOCK), lambda b, h: (h, 0, 0, 0)),
                  pl.BlockSpec((DA_V_DIM, LANES), lambda b, h: (0, 0))],
        out_specs=pl.BlockSpec((1, wide, S), lambda b, h: (b, h, 0)),
        out_shape=jax.ShapeDtypeStruct((B, DA_WIDTH, S), BF16),
        scratch_shapes=[pltpu.VMEM((n_q, n_map, DA_V_DIM, DA_BLOCK), F32),
                        pltpu.VMEM((n_q, n_map, 1, DA_BLOCK), F32),
                        pltpu.VMEM((n_q, n_map, 1, DA_BLOCK), F32),
                        pltpu.VMEM((n_q, n_map, wide, DA_BLOCK), BF16),
                        pltpu.VMEM((n_map, DA_BLOCK, DA_BLOCK), F32),
                        pltpu.VMEM((n_map, 1, DA_BLOCK), F32),
                        pltpu.VMEM((2, n_map, DA_BLOCK, DA_BLOCK), BF16),
                        pltpu.VMEM((n_map, 1, DA_BLOCK), F32)],
        compiler_params=_cparams(2),
        name="diff_attention",
    )(lam, jnp.asarray(qs), jnp.asarray(ks), q_t, k, v_t, bias,
      jnp.broadcast_to(subln_g.reshape(DA_V_DIM, 1), (DA_V_DIM, LANES)))


def _sb_kernel(q_ref, k_ref, v_ref, mask_ref, o_ref, acc_ref, carry_ref, qz_ref, u_ref, tot_ref, w_ref):
    qi = pl.program_id(2)
    nh = acc_ref.shape[0]
    slab = qz_ref.shape[1]
    slab_heads = slab // SB_HEAD_DIM
    for sl in range(nh // slab_heads):
        _store_row_groups(q_ref[0, 0, sl * slab:(sl + 1) * slab, :], qz_ref.at[pl.ds(sl * slab_heads, slab_heads)],
                          SB_HEAD_DIM)
    carry_ref[...] = jnp.ones(carry_ref.shape, F32)
    sub = lax.broadcasted_iota(jnp.int32, (SUBLANES, TOK_BLOCK), 0)
    rows = lambda x, g: x[g * SUBLANES:(g + 1) * SUBLANES]

    def logits(kj):
        kb = _key_block(k_ref, kj)
        return [jnp.dot(kb[:, (hh // slab_heads) * slab:(hh // slab_heads + 1) * slab], qz_ref[hh],
                        preferred_element_type=F32) for hh in range(nh)]

    def gates(zs, slot, mask=None):
        for hh in range(nh):
            t = jnp.tanh(zs[hh])
            if mask is not None:
                t = jnp.where(mask > 0.5, t, -1.0)
            u = 1.0 - t
            u_ref[slot, hh] = u
            tot = rows(u, CHUNK - 1)
            for g in reversed(range(CHUNK - 1)):
                tot = tot * rows(u, g)
            tot_ref[slot, hh] = tot * 2.0 ** -CHUNK

    def weights(slot, keep_carry=None):
        for hh in range(nh):
            suffix = tot_ref[slot, hh]
            for d in (1, 2, 4):
                shifted = pltpu.roll(suffix, SUBLANES - d, axis=0)
                suffix = jnp.where(sub + d < SUBLANES, suffix * shifted, suffix)
            above = jnp.where(sub + 1 < SUBLANES, pltpu.roll(suffix, SUBLANES - 1, axis=0), 1.0)
            carry = carry_ref[hh]
            new_carry = carry * jnp.broadcast_to(suffix[0:1], carry.shape)
            carry_ref[hh] = new_carry if keep_carry is None else new_carry * keep_carry
            run = above * carry
            w = [None] * CHUNK
            for g in reversed(range(CHUNK)):
                ug = u_ref[slot, hh, g * SUBLANES:(g + 1) * SUBLANES, :]
                w[g] = (2.0 - ug) * run
                run = run * ug
            w_ref[slot, hh] = jnp.concatenate(w, axis=0).astype(BF16)

    def weighted_values(kj, slot):
        vb = v_ref[0, kj]
        return [jnp.dot(vb[hh * SB_HEAD_DIM:(hh + 1) * SB_HEAD_DIM], w_ref[slot, hh], preferred_element_type=F32)
                for hh in range(nh)]

    def alive():
        return (jnp.max(carry_ref[...]) > 0.0).astype(jnp.int32)

    before = jnp.maximum(qi - 1, 0)
    z_diag = logits(qi)
    z_before = logits(before)
    gates(z_diag, 0, mask_ref[...])
    gates(z_before, 1)
    weights(0, keep_carry=(qi > 0).astype(F32))
    weights(1)
    pv_diag = weighted_values(qi, 0)
    pv_before = weighted_values(before, 1)
    for hh in range(nh):
        acc_ref[hh] = pv_diag[hh] + pv_before[hh]

    more = jnp.logical_and(qi >= 2, alive() > 0)
    pl.when(more)(lambda: gates(logits(jnp.maximum(qi - 2, 0)), 0))

    def left_cond(state):
        i, live = state
        return jnp.logical_and(i < qi - 1, live > 0)

    def left_body(state):
        i, _ = state
        kj = qi - 2 - i
        nxt = logits(jnp.maximum(kj - 1, 0))
        weights(0)
        pv = weighted_values(kj, 0)
        for hh in range(nh):
            acc_ref[hh] = acc_ref[hh] + pv[hh]
        gates(nxt, 0)
        return i + 1, alive()

    lax.while_loop(left_cond, left_body, (jnp.int32(0), more.astype(jnp.int32)))

    for hh in range(nh):
        o_ref[0, hh * SB_HEAD_DIM:(hh + 1) * SB_HEAD_DIM, :] = acc_ref[hh].astype(o_ref.dtype)


SB_GROUP = 8


def _sb_attention(q_t, k, v_t):
    B, S, _ = k.shape
    n_blk = S // TOK_BLOCK
    nh = SB_GROUP
    pair = nh * SB_HEAD_DIM
    return pl.pallas_call(
        _sb_kernel,
        grid=(B, SB_HEADS // nh, n_blk),
        in_specs=[pl.BlockSpec((1, 1, pair, TOK_BLOCK), lambda b, h, i: (b, i, h, 0)),
                  pl.BlockSpec((1, S, pair), lambda b, h, i: (b, 0, h)),
                  pl.BlockSpec((1, n_blk, pair, TOK_BLOCK), lambda b, h, i: (b, 0, h, 0)),
                  pl.BlockSpec((TOK_BLOCK, TOK_BLOCK), lambda b, h, i: (0, 0))],
        out_specs=pl.BlockSpec((1, pair, TOK_BLOCK), lambda b, h, i: (b, h, i)),
        out_shape=jax.ShapeDtypeStruct((B, SB_WIDTH, S), BF16),
        scratch_shapes=[pltpu.VMEM((nh, SB_HEAD_DIM, TOK_BLOCK), F32),
                        pltpu.VMEM((nh, SUBLANES, TOK_BLOCK), F32),
                        pltpu.VMEM((nh, TOK_BLOCK, TOK_BLOCK), BF16),
                        pltpu.VMEM((2, nh, TOK_BLOCK, TOK_BLOCK), F32),
                        pltpu.VMEM((2, nh, SUBLANES, TOK_BLOCK), F32),
                        pltpu.VMEM((2, nh, TOK_BLOCK, TOK_BLOCK), BF16)],
        compiler_params=_cparams(3),
        name="stick_breaking_attention",
    )(q_t, k, v_t, jnp.asarray(_strict_causal_tile()))


def _out_kernel(oda_ref, osb_ref, x_ref, g_ref, scl_ref, shift_ref, gate_ref, wg_ref, w_ref, fg_ref, out_ref, *,
                final_norm):
    x = x_ref[0]
    h = _modulated_norm(x, g_ref[...], scl_ref[0], shift_ref[0])
    split = oda_ref.shape[1]

    def gated(o_ref, rows):
        g = lax.dot_general(wg_ref[rows, :], h, (((1,), (1,)), ((), ())), preferred_element_type=F32)
        return (o_ref[0].astype(F32) * (g / (1.0 + jnp.exp(-g)))).astype(BF16)

    contract_features = (((0,), (0,)), ((), ()))
    y = lax.dot_general(gated(oda_ref, slice(0, split)), w_ref[0:split, :], contract_features,
                        preferred_element_type=F32)
    y = y + lax.dot_general(gated(osb_ref, slice(split, 2 * split)), w_ref[split:, :], contract_features,
                            preferred_element_type=F32)
    xo = x + gate_ref[0] * y
    if final_norm:
        xo = xo * lax.rsqrt(jnp.mean(xo * xo, axis=-1, keepdims=True) + EPS) * fg_ref[...]
    out_ref[0] = xo


def _out_project(o_da, o_sb, x, norm_g, scl, shift, gate, w_gate, w_out, final_g, final_norm):
    B, S, D = x.shape
    half_spec = pl.BlockSpec((1, DA_WIDTH, OUT_ROW_TILE), lambda b, i: (b, 0, i))
    row_spec = pl.BlockSpec((1, OUT_ROW_TILE, D), lambda b, i: (b, i, 0))
    vec_spec = pl.BlockSpec((1, D), lambda b, i: (0, 0))
    mod_spec = pl.BlockSpec((1, 1, D), lambda b, i: (b, 0, 0))
    return pl.pallas_call(
        functools.partial(_out_kernel, final_norm=final_norm),
        grid=(B, S // OUT_ROW_TILE),
        in_specs=[half_spec, half_spec, row_spec, vec_spec, mod_spec, mod_spec, mod_spec,
                  pl.BlockSpec(w_gate.shape, lambda b, i: (0, 0)),
                  pl.BlockSpec(w_out.shape, lambda b, i: (0, 0)),
                  vec_spec],
        out_specs=row_spec,
        out_shape=jax.ShapeDtypeStruct((B, S, D), F32),
        compiler_params=_cparams(2),
        name="gate_out_proj",
    )(o_da, o_sb, x, norm_g.reshape(1, D), scl, shift, gate, w_gate, w_out, final_g.reshape(1, D))


def _split_w_in(w):
    sizes = (DA_WIDTH, DA_WIDTH, DA_WIDTH, DA_WIDTH, SB_WIDTH, SB_WIDTH, SB_WIDTH, SB_WIDTH)
    q_da, k_da, v_da, g_da, q_sb, k_sb, v_sb, g_sb = jnp.split(w, np.cumsum(sizes)[:-1].tolist(), axis=1)
    w_tok = jnp.concatenate([k_da, k_sb], axis=1).astype(BF16)
    w_gate = jnp.concatenate([g_da, g_sb], axis=1).T.astype(BF16)
    q_scale = 1.0 / math.sqrt(DA_HEAD_DIM)
    w_feat_t = jnp.concatenate([q_da * (LOG2E * q_scale), v_da, q_sb * (0.5 * q_scale), v_sb], axis=1)
    return w_tok, w_feat_t.T.astype(BF16), w_gate


def kernel(x, c, norm_g, w_ada, b_ada, w_in, lambda_q1, lambda_k1, lambda_q2, lambda_k2, subln_g, w_out,
           rel_bias, final_g):
    B, S, D = x.shape
    depth = w_in.shape[0]
    assert S % ROW_TILE == 0 and ROW_TILE % TOK_BLOCK == 0 and S % DA_BLOCK == 0
    assert S % OUT_ROW_TILE == 0 and OUT_ROW_TILE % TOK_BLOCK == 0
    assert DA_HEAD_DIM == SB_HEAD_DIM and DA_WIDTH == SB_WIDTH and DA_WIDTH + SB_WIDTH == w_out.shape[1]
    bias = _bias_tiles(rel_bias)
    for l in range(depth):
        lambda_init = _lambda_init(l)
        mod, lam = _modulation(c, w_ada[l], b_ada[l], lambda_q1[l], lambda_k1[l], lambda_q2[l], lambda_k2[l],
                               lambda_init)
        shift, scl, gate = (m.reshape(B, 1, D) for m in jnp.split(mod, 3, axis=-1))
        w_tok, w_feat_t, w_gate = _split_w_in(w_in[l])
        k_da, k_sb, q_da, v_da, q_sb, v_sb = _project(x, norm_g[l], scl, shift, w_tok, w_feat_t)
        o_da = _diff_attention(q_da, k_da, v_da, bias, lam[:, :1], subln_g[l], lambda_init)
        o_sb = _sb_attention(q_sb, k_sb, v_sb)
        x = _out_project(o_da, o_sb, x, norm_g[l], scl, shift, gate, w_gate, w_out[l].astype(BF16), final_g,
                         l == depth - 1)
    return x
```

```python
import functools
import math

import numpy as np
import jax
import jax.numpy as jnp
from jax import lax
from jax.experimental import pallas as pl
from jax.experimental.pallas import tpu as pltpu

DA_HEADS = 4
DA_HEAD_DIM = 64
DA_V_DIM = 2 * DA_HEAD_DIM
DA_WIDTH = DA_HEADS * DA_V_DIM
SB_HEADS = 8
SB_HEAD_DIM = 64
SB_WIDTH = SB_HEADS * SB_HEAD_DIM
N_BUCKETS = 32
MAX_DISTANCE = 128
EPS = 1e-6
NEG_INF = -1e30

SUBLANES = 8
LANES = 128
TOK_BLOCK = 256
CHUNK = TOK_BLOCK // SUBLANES
DA_BLOCK = 2 * TOK_BLOCK
LOG2E = math.log2(math.e)
DIAG_TILE, NEAR_TILE = 0, 1
ROW_TILE = 1024
OUT_ROW_TILE = 1024
VMEM_LIMIT = 56 * 1024 * 1024

F32 = jnp.float32
BF16 = jnp.bfloat16


def _lambda_init(layer_idx):
    return 0.8 - 0.6 * math.exp(-0.3 * layer_idx)


def _block_token(pos):
    return (pos % SUBLANES) * CHUNK + pos // SUBLANES


def _bucket_tiles():
    pos = np.arange(DA_BLOCK)
    key_tok = (pos // TOK_BLOCK) * TOK_BLOCK + _block_token(pos % TOK_BLOCK)
    rel0 = pos[None, :] - key_tok[:, None]
    rel = np.stack([np.maximum(rel0, 0), rel0 + DA_BLOCK]).astype(np.int32)
    max_exact = N_BUCKETS // 2
    nf = np.maximum(rel, 1).astype(np.float32)
    large = max_exact + (np.log(nf / np.float32(max_exact)) / np.float32(math.log(MAX_DISTANCE / max_exact))
                         * np.float32(N_BUCKETS - max_exact)).astype(np.int32)
    large = np.minimum(large, N_BUCKETS - 1)
    bucket = np.where(rel < max_exact, rel, large).astype(np.int32)
    bucket[0] = np.where(rel0 < 0, N_BUCKETS, bucket[0])
    return bucket


def _position_to_token():
    pos = np.arange(TOK_BLOCK)
    m = np.zeros((TOK_BLOCK, TOK_BLOCK), np.float32)
    m[pos, _block_token(pos)] = 1.0
    return m


def _strict_causal_tile():
    pos = np.arange(TOK_BLOCK)
    return (_block_token(pos)[:, None] < pos[None, :]).astype(np.float32)


def _value_column_scale(n):
    g = (np.arange(n) % TOK_BLOCK) // SUBLANES
    return (2.0 ** -(CHUNK - g)).astype(np.float32).reshape(1, n)


def _cparams(n_axes):
    return pltpu.CompilerParams(dimension_semantics=("arbitrary",) * n_axes, vmem_limit_bytes=VMEM_LIMIT)


def _mod_kernel(c_ref, w_ref, b_ref, lq1_ref, lk1_ref, lq2_ref, lk2_ref, mod_ref, lam_ref, *, lambda_init):
    c = c_ref[...]
    silu_c = c / (1.0 + jnp.exp(-c))
    mod_ref[...] = jnp.dot(silu_c, w_ref[...], preferred_element_type=F32,
                           precision=lax.Precision.HIGHEST) + b_ref[...]
    s1 = jnp.sum(lq1_ref[...] * lk1_ref[...], axis=-1, keepdims=True)
    s2 = jnp.sum(lq2_ref[...] * lk2_ref[...], axis=-1, keepdims=True)
    lam_ref[...] = jnp.broadcast_to(jnp.exp(s1) - jnp.exp(s2) + lambda_init, lam_ref.shape)


def _modulation(c, w_ada, b_ada, lq1, lk1, lq2, lk2, lambda_init):
    B, D = c.shape
    n_out = w_ada.shape[1]
    col = 512
    vec = lambda: pl.BlockSpec((1, DA_HEAD_DIM), lambda j: (0, 0))
    return pl.pallas_call(
        functools.partial(_mod_kernel, lambda_init=lambda_init),
        grid=(n_out // col,),
        in_specs=[pl.BlockSpec((B, D), lambda j: (0, 0)),
                  pl.BlockSpec((D, col), lambda j: (0, j)),
                  pl.BlockSpec((1, col), lambda j: (0, j)),
                  vec(), vec(), vec(), vec()],
        out_specs=[pl.BlockSpec((B, col), lambda j: (0, j)),
                   pl.BlockSpec((1, LANES), lambda j: (0, 0))],
        out_shape=[jax.ShapeDtypeStruct((B, n_out), F32), jax.ShapeDtypeStruct((1, LANES), F32)],
        compiler_params=_cparams(1),
        name="adaln_mod",
    )(c, w_ada, b_ada.reshape(1, n_out), lq1.reshape(1, -1), lk1.reshape(1, -1), lq2.reshape(1, -1),
      lk2.reshape(1, -1))


def _bias_kernel(rb_ref, bucket_ref, out_ref):
    h = pl.program_id(0)
    far = rb_ref[N_BUCKETS - 1, h]
    for t in range(2):
        bucket = bucket_ref[t]
        tile = jnp.zeros(bucket.shape, F32)
        for b in range(N_BUCKETS - 1):
            tile = jnp.where(bucket == b, (rb_ref[b, h] - far) * LOG2E, tile)
        out_ref[0, t] = jnp.where(bucket == N_BUCKETS, NEG_INF, tile)


def _bias_tiles(rel_bias):
    buckets = _bucket_tiles()
    near_rest = buckets[NEAR_TILE].copy()
    near_rest[TOK_BLOCK:, :LANES] = N_BUCKETS - 1
    assert (near_rest == N_BUCKETS - 1).all() and (buckets[DIAG_TILE][TOK_BLOCK:, :TOK_BLOCK] == N_BUCKETS).all()
    buckets = jnp.asarray(buckets)
    return pl.pallas_call(
        _bias_kernel,
        grid=(DA_HEADS,),
        in_specs=[pl.BlockSpec(memory_space=pltpu.SMEM),
                  pl.BlockSpec((2, DA_BLOCK, DA_BLOCK), lambda h: (0, 0, 0))],
        out_specs=pl.BlockSpec((1, 2, DA_BLOCK, DA_BLOCK), lambda h: (h, 0, 0, 0)),
        out_shape=jax.ShapeDtypeStruct((DA_HEADS, 2, DA_BLOCK, DA_BLOCK), F32),
        compiler_params=_cparams(1),
        name="rel_bias_tiles",
    )(rel_bias, buckets)


def _modulated_norm(x, g, scl, shift):
    ms = jnp.mean(x * x, axis=-1, keepdims=True)
    xn = x * lax.rsqrt(ms + EPS) * g
    return (xn * (1.0 + scl) + shift).astype(BF16)


def _proj_kernel(x_ref, g_ref, scl_ref, shift_ref, wt_ref, wf_ref, vscale_ref, perm_ref,
                 kda_ref, ksb_ref, qda_ref, vda_ref, qsb_ref, vsb_ref):
    h = _modulated_norm(x_ref[0], g_ref[...], scl_ref[0], shift_ref[0])
    hp = jnp.concatenate(
        [jnp.dot(perm_ref[...], h[j * TOK_BLOCK:(j + 1) * TOK_BLOCK], preferred_element_type=F32).astype(BF16)
         for j in range(h.shape[0] // TOK_BLOCK)], axis=0)
    width = kda_ref.shape[-1]
    for i, o_ref in enumerate((kda_ref, ksb_ref)):
        o_ref[0] = jnp.dot(hp, wt_ref[:, i * width:(i + 1) * width], preferred_element_type=F32).astype(BF16)
    for i, (o_ref, src) in enumerate(((qda_ref, h), (vda_ref, hp), (qsb_ref, h), (vsb_ref, hp))):
        r = lax.dot_general(wf_ref[i * width:(i + 1) * width, :], src, (((1,), (1,)), ((), ())),
                            preferred_element_type=F32)
        if o_ref is vsb_ref:
            r = r * vscale_ref[...]
        r = r.astype(BF16)
        for t in range(o_ref.shape[1]):
            o_ref[0, t] = r[:, t * TOK_BLOCK:(t + 1) * TOK_BLOCK]


def _project(x, norm_g, scl, shift, w_tok, w_feat_t):
    B, S, D = x.shape
    n_blk = S // TOK_BLOCK
    blk_per_tile = ROW_TILE // TOK_BLOCK
    width = DA_WIDTH
    tok_spec = pl.BlockSpec((1, ROW_TILE, width), lambda b, i: (b, i, 0))
    feat_spec = pl.BlockSpec((1, blk_per_tile, width, TOK_BLOCK), lambda b, i: (b, i, 0, 0))
    tok_shape = jax.ShapeDtypeStruct((B, S, width), BF16)
    feat_shape = jax.ShapeDtypeStruct((B, n_blk, width, TOK_BLOCK), BF16)
    mod_spec = pl.BlockSpec((1, 1, D), lambda b, i: (b, 0, 0))
    return pl.pallas_call(
        _proj_kernel,
        grid=(B, S // ROW_TILE),
        in_specs=[pl.BlockSpec((1, ROW_TILE, D), lambda b, i: (b, i, 0)),
                  pl.BlockSpec((1, D), lambda b, i: (0, 0)),
                  mod_spec, mod_spec,
                  pl.BlockSpec(w_tok.shape, lambda b, i: (0, 0)),
                  pl.BlockSpec(w_feat_t.shape, lambda b, i: (0, 0)),
                  pl.BlockSpec((1, ROW_TILE), lambda b, i: (0, 0)),
                  pl.BlockSpec((TOK_BLOCK, TOK_BLOCK), lambda b, i: (0, 0))],
        out_specs=[tok_spec] * 2 + [feat_spec] * 4,
        out_shape=[tok_shape] * 2 + [feat_shape] * 4,
        compiler_params=_cparams(2),
        name="norm_in_proj",
    )(x, norm_g.reshape(1, D), scl, shift, w_tok, w_feat_t, jnp.asarray(_value_column_scale(ROW_TILE)),
      jnp.asarray(_position_to_token(), BF16))


def _store_row_groups(q_t, qz_ref, group):
    row = lax.broadcasted_iota(jnp.int32, q_t.shape, 0)
    for i in range(qz_ref.shape[0]):
        inside = (row >= i * group) & (row < (i + 1) * group)
        qz_ref[i] = jnp.where(inside, q_t, jnp.zeros_like(q_t))


def _key_block(k_ref, kj, rows=TOK_BLOCK):
    return k_ref[0, pl.ds(pl.multiple_of(kj * rows, rows), rows), :]


def _da_tile_passes(n_q):
    plain = [(qi, kj) for qi in range(n_q) for kj in range(qi - 1)]
    near = [(qi, qi - 1) for qi in range(1, n_q)]
    diag = [(qi, qi) for qi in range(n_q)]
    passes, start = [], 0
    for tiles, kind in ((plain, None), (near, NEAR_TILE), (diag, DIAG_TILE)):
        passes.append((start, len(tiles), kind))
        start += len(tiles)
    order = np.asarray(plain + near + diag, np.int32)
    return order[:, 0], order[:, 1], tuple(passes)


def _da_kernel(lam_ref, qs_ref, ks_ref, q_ref, k_ref, v_ref, bias_ref, g_ref, o_ref,
               acc_ref, m_ref, l_ref, qz_ref, s_ref, smax_ref, p_ref, a_ref, *, out_scale, passes):
    n_q, n_map = acc_ref.shape[0], acc_ref.shape[1]
    n_sub = DA_BLOCK // TOK_BLOCK
    acc_ref[...] = jnp.zeros(acc_ref.shape, F32)
    m_ref[...] = jnp.full(m_ref.shape, NEG_INF, F32)
    l_ref[...] = jnp.zeros(l_ref.shape, F32)

    for qi in range(n_q):
        q_t = jnp.concatenate([q_ref[0, n_sub * qi + j] for j in range(n_sub)], axis=-1)
        _store_row_groups(q_t, qz_ref.at[qi], DA_HEAD_DIM)

    def scores(qi, kj, maps=None):
        kb = _key_block(k_ref, kj, DA_BLOCK)
        return [jnp.dot(kb, qz_ref[qi, mp], preferred_element_type=F32)
                for mp in (range(n_map) if maps is None else maps)]

    half, lane0 = TOK_BLOCK, LANES

    def put_scores(ss, kind):
        for mp in range(n_map):
            s = ss[mp]
            hd = mp // 2
            if kind == NEAR_TILE:
                corner = s[half:, :lane0] + bias_ref[hd, kind, half:, :lane0]
                s = jnp.concatenate([s[:half], jnp.concatenate([corner, s[half:, lane0:]], axis=1)], axis=0)
                s_ref[mp] = s
                smax_ref[mp] = jnp.max(s, axis=0, keepdims=True)
            elif kind == DIAG_TILE:
                top = s[:half] + bias_ref[hd, kind, :half, :]
                right = s[half:, half:] + bias_ref[hd, kind, half:, half:]
                s_ref[mp, :half, :] = top
                s_ref[mp, half:, half:] = right
                top_max = jnp.max(top, axis=0, keepdims=True)
                smax_ref[mp] = jnp.concatenate(
                    [top_max[:, :half], jnp.maximum(top_max[:, half:], jnp.max(right, axis=0, keepdims=True))], axis=1)
            else:
                s_ref[mp] = s
                smax_ref[mp] = jnp.max(s, axis=0, keepdims=True)

    def softmax_step(qi, slot, kind):
        for mp in range(n_map):
            m_old = m_ref[qi, mp]
            m_new = jnp.maximum(m_old, smax_ref[mp])
            alpha = jnp.exp2(m_old - m_new)
            if kind == DIAG_TILE:
                p_top = jnp.exp2(s_ref[mp, :half, :] - m_new)
                p_right = jnp.exp2(s_ref[mp, half:, half:] - m_new[:, half:])
                top_sum = jnp.sum(p_top, axis=0, keepdims=True)
                p_sum = jnp.concatenate(
                    [top_sum[:, :half], top_sum[:, half:] + jnp.sum(p_right, axis=0, keepdims=True)], axis=1)
                p_ref[slot, mp, :half, :] = p_top.astype(BF16)
                p_ref[slot, mp, half:, :half] = jnp.zeros((half, half), BF16)
                p_ref[slot, mp, half:, half:] = p_right.astype(BF16)
            else:
                p = jnp.exp2(s_ref[mp] - m_new)
                p_sum = jnp.sum(p, axis=0, keepdims=True)
                p_ref[slot, mp] = p.astype(BF16)
            l_ref[qi, mp] = alpha * l_ref[qi, mp] + p_sum
            m_ref[qi, mp] = m_new
            a_ref[mp] = alpha

    def weighted_values(kj, slot, maps=None):
        out = []
        for mp in (range(n_map) if maps is None else maps):
            pv = None
            for j in range(n_sub):
                vh = v_ref[0, n_sub * kj + j, (mp // 2) * DA_V_DIM:(mp // 2 + 1) * DA_V_DIM, :]
                part = jnp.dot(vh, p_ref[slot, mp, j * TOK_BLOCK:(j + 1) * TOK_BLOCK, :], preferred_element_type=F32)
                pv = part if pv is None else pv + part
            out.append(pv)
        return out

    def add_weighted(pv, qi):
        for mp in range(n_map):
            acc_ref[qi, mp] = a_ref[mp] * acc_ref[qi, mp] + pv[mp]

    for start, count, kind in passes:
        if count == 0:
            continue
        last = start + count - 1
        put_scores(scores(qs_ref[start], ks_ref[start]), kind)
        second = min(start + 1, last)
        next_scores = scores(qs_ref[second], ks_ref[second])
        softmax_step(qs_ref[start], start % 2, kind)
        put_scores(next_scores, kind)

        def one_step(cur, cur_slot, last=last, kind=kind):
            prev = cur - 1
            nxt = jnp.minimum(cur + 1, last)
            pv, next_scores = [], []
            for mp in range(n_map):
                pv += weighted_values(ks_ref[prev], 1 - cur_slot, (mp,))
                next_scores += scores(qs_ref[nxt], ks_ref[nxt], (mp,))
            add_weighted(pv, qs_ref[prev])
            softmax_step(qs_ref[cur], cur_slot, kind)
            put_scores(next_scores, kind)

        rest = count - 1
        first_slot = (start + 1) % 2

        def two_steps(j, carry, start=start, first_slot=first_slot, one_step=one_step):
            cur = start + 1 + 2 * j
            one_step(cur, first_slot)
            one_step(cur + 1, 1 - first_slot)
            return carry

        lax.fori_loop(0, rest // 2, two_steps, 0)
        if rest % 2:
            one_step(last, last % 2)
        add_weighted(weighted_values(ks_ref[last], last % 2), qs_ref[last])

    lam = lam_ref[0, 0]

    gain = jnp.concatenate([g_ref[...]] * (DA_BLOCK // LANES), axis=1) * out_scale

    def finish(qi, carry):
        for hd in range(n_map // 2):
            o_t = (acc_ref[qi, 2 * hd] / l_ref[qi, 2 * hd]
                   - lam * (acc_ref[qi, 2 * hd + 1] / l_ref[qi, 2 * hd + 1]))
            o_t = o_t * lax.rsqrt(jnp.mean(o_t * o_t, axis=0, keepdims=True) + EPS) * gain
            o_ref[0, hd * DA_V_DIM:(hd + 1) * DA_V_DIM,
                  pl.ds(pl.multiple_of(qi * DA_BLOCK, DA_BLOCK), DA_BLOCK)] = o_t.astype(o_ref.dtype)
        return carry

    lax.fori_loop(0, n_q, finish, 0)


DA_GROUP = 2


def _diff_attention(q_t, k, v_t, bias, lam, subln_g, lambda_init):
    B, S, _ = k.shape
    n_blk = S // TOK_BLOCK
    n_q = S // DA_BLOCK
    nhd = DA_GROUP
    n_map = 2 * nhd
    wide = nhd * DA_V_DIM
    qs, ks, passes = _da_tile_passes(n_q)
    smem = pl.BlockSpec(memory_space=pltpu.SMEM)
    return pl.pallas_call(
        functools.partial(_da_kernel, out_scale=1.0 - lambda_init, passes=passes),
        grid=(B, DA_HEADS // nhd),
        in_specs=[smem, smem, smem,
                  pl.BlockSpec((1, n_blk, wide, TOK_BLOCK), lambda b, h: (b, 0, h, 0)),
                  pl.BlockSpec((1, S, wide), lambda b, h: (b, 0, h)),
                  pl.BlockSpec((1, n_blk, wide, TOK_BLOCK), lambda b, h: (b, 0, h, 0)),
                  pl.BlockSpec((nhd, 2, DA_BLOCK, DA_BLOCK), lambda b, h: (h, 0, 0, 0)),
                  pl.BlockSpec((DA_V_DIM, LANES), lambda b, h: (0, 0))],
        out_specs=pl.BlockSpec((1, wide, S), lambda b, h: (b, h, 0)),
        out_shape=jax.ShapeDtypeStruct((B, DA_WIDTH, S), BF16),
        scratch_shapes=[pltpu.VMEM((n_q, n_map, DA_V_DIM, DA_BLOCK), F32),
                        pltpu.VMEM((n_q, n_map, 1, DA_BLOCK), F32),
                        pltpu.VMEM((n_q, n_map, 1, DA_BLOCK), F32),
                        pltpu.VMEM((n_q, n_map, wide, DA_BLOCK), BF16),
                        pltpu.VMEM((n_map, DA_BLOCK, DA_BLOCK), F32),
                        pltpu.VMEM((n_map, 1, DA_BLOCK), F32),
                        pltpu.VMEM((2, n_map, DA_BLOCK, DA_BLOCK), BF16),
                        pltpu.VMEM((n_map, 1, DA_BLOCK), F32)],
        compiler_params=_cparams(2),
        name="diff_attention",
    )(lam, jnp.asarray(qs), jnp.asarray(ks), q_t, k, v_t, bias,
      jnp.broadcast_to(subln_g.reshape(DA_V_DIM, 1), (DA_V_DIM, LANES)))


def _sb_kernel(q_ref, k_ref, v_ref, mask_ref, o_ref, acc_ref, carry_ref, qz_ref, u_ref, tot_ref, w_ref):
    qi = pl.program_id(2)
    nh = acc_ref.shape[0]
    slab = qz_ref.shape[1]
    slab_heads = slab // SB_HEAD_DIM
    for sl in range(nh // slab_heads):
        _store_row_groups(q_ref[0, 0, sl * slab:(sl + 1) * slab, :], qz_ref.at[pl.ds(sl * slab_heads, slab_heads)],
                          SB_HEAD_DIM)
    carry_ref[...] = jnp.ones(carry_ref.shape, F32)
    sub = lax.broadcasted_iota(jnp.int32, (SUBLANES, TOK_BLOCK), 0)
    rows = lambda x, g: x[g * SUBLANES:(g + 1) * SUBLANES]

    def logits(kj):
        kb = _key_block(k_ref, kj)
        return [jnp.dot(kb[:, (hh // slab_heads) * slab:(hh // slab_heads + 1) * slab], qz_ref[hh],
                        preferred_element_type=F32) for hh in range(nh)]

    def gates(zs, slot, mask=None):
        for hh in range(nh):
            t = jnp.tanh(zs[hh])
            if mask is not None:
                t = jnp.where(mask > 0.5, t, -1.0)
            u = 1.0 - t
            u_ref[slot, hh] = u
            tot = rows(u, CHUNK - 1)
            for g in reversed(range(CHUNK - 1)):
                tot = tot * rows(u, g)
            tot_ref[slot, hh] = tot * 2.0 ** -CHUNK

    def weights(slot, keep_carry=None):
        for hh in range(nh):
            suffix = tot_ref[slot, hh]
            for d in (1, 2, 4):
                shifted = pltpu.roll(suffix, SUBLANES - d, axis=0)
                suffix = jnp.where(sub + d < SUBLANES, suffix * shifted, suffix)
            above = jnp.where(sub + 1 < SUBLANES, pltpu.roll(suffix, SUBLANES - 1, axis=0), 1.0)
            carry = carry_ref[hh]
            new_carry = carry * jnp.broadcast_to(suffix[0:1], carry.shape)
            carry_ref[hh] = new_carry if keep_carry is None else new_carry * keep_carry
            run = above * carry
            w = [None] * CHUNK
            for g in reversed(range(CHUNK)):
                ug = u_ref[slot, hh, g * SUBLANES:(g + 1) * SUBLANES, :]
                w[g] = (2.0 - ug) * run
                run = run * ug
            w_ref[slot, hh] = jnp.concatenate(w, axis=0).astype(BF16)

    def weighted_values(kj, slot):
        vb = v_ref[0, kj]
        return [jnp.dot(vb[hh * SB_HEAD_DIM:(hh + 1) * SB_HEAD_DIM], w_ref[slot, hh], preferred_element_type=F32)
                for hh in range(nh)]

    def alive():
        return (jnp.max(carry_ref[...]) > 0.0).astype(jnp.int32)

    before = jnp.maximum(qi - 1, 0)
    z_diag = logits(qi)
    z_before = logits(before)
    gates(z_diag, 0, mask_ref[...])
    gates(z_before, 1)
    weights(0, keep_carry=(qi > 0).astype(F32))
    weights(1)
    pv_diag = weighted_values(qi, 0)
    pv_before = weighted_values(before, 1)
    for hh in range(nh):
        acc_ref[hh] = pv_diag[hh] + pv_before[hh]

    more = jnp.logical_and(qi >= 2, alive() > 0)
    pl.when(more)(lambda: gates(logits(jnp.maximum(qi - 2, 0)), 0))

    def left_cond(state):
        i, live = state
        return jnp.logical_and(i < qi - 1, live > 0)

    def left_body(state):
        i, _ = state
        kj = qi - 2 - i
        nxt = logits(jnp.maximum(kj - 1, 0))
        weights(0)
        pv = weighted_values(kj, 0)
        for hh in range(nh):
            acc_ref[hh] = acc_ref[hh] + pv[hh]
        gates(nxt, 0)
        return i + 1, alive()

    lax.while_loop(left_cond, left_body, (jnp.int32(0), more.astype(jnp.int32)))

    for hh in range(nh):
        o_ref[0, hh * SB_HEAD_DIM:(hh + 1) * SB_HEAD_DIM, :] = acc_ref[hh].astype(o_ref.dtype)


SB_GROUP = 8


def _sb_attention(q_t, k, v_t):
    B, S, _ = k.shape
    n_blk = S // TOK_BLOCK
    nh = SB_GROUP
    pair = nh * SB_HEAD_DIM
    return pl.pallas_call(
        _sb_kernel,
        grid=(B, SB_HEADS // nh, n_blk),
        in_specs=[pl.BlockSpec((1, 1, pair, TOK_BLOCK), lambda b, h, i: (b, i, h, 0)),
                  pl.BlockSpec((1, S, pair), lambda b, h, i: (b, 0, h)),
                  pl.BlockSpec((1, n_blk, pair, TOK_BLOCK), lambda b, h, i: (b, 0, h, 0)),
                  pl.BlockSpec((TOK_BLOCK, TOK_BLOCK), lambda b, h, i: (0, 0))],
        out_specs=pl.BlockSpec((1, pair, TOK_BLOCK), lambda b, h, i: (b, h, i)),
        out_shape=jax.ShapeDtypeStruct((B, SB_WIDTH, S), BF16),
        scratch_shapes=[pltpu.VMEM((nh, SB_HEAD_DIM, TOK_BLOCK), F32),
                        pltpu.VMEM((nh, SUBLANES, TOK_BLOCK), F32),
                        pltpu.VMEM((nh, TOK_BLOCK, TOK_BLOCK), BF16),
                        pltpu.VMEM((2, nh, TOK_BLOCK, TOK_BLOCK), F32),
                        pltpu.VMEM((2, nh, SUBLANES, TOK_BLOCK), F32),
                        pltpu.VMEM((2, nh, TOK_BLOCK, TOK_BLOCK), BF16)],
        compiler_params=_cparams(3),
        name="stick_breaking_attention",
    )(q_t, k, v_t, jnp.asarray(_strict_causal_tile()))


def _out_kernel(oda_ref, osb_ref, x_ref, g_ref, scl_ref, shift_ref, gate_ref, wg_ref, w_ref, fg_ref, out_ref, *,
                final_norm):
    x = x_ref[0]
    h = _modulated_norm(x, g_ref[...], scl_ref[0], shift_ref[0])
    split = oda_ref.shape[1]

    def gated(o_ref, rows):
        g = lax.dot_general(wg_ref[:, rows], h, (((0,), (1,)), ((), ())), preferred_element_type=F32)
        return (o_ref[0].astype(F32) * (g / (1.0 + jnp.exp(-g)))).astype(BF16)

    contract_features = (((0,), (0,)), ((), ()))
    y = lax.dot_general(gated(oda_ref, slice(0, split)), w_ref[0:split, :], contract_features,
                        preferred_element_type=F32)
    y = y + lax.dot_general(gated(osb_ref, slice(split, 2 * split)), w_ref[split:, :], contract_features,
                            preferred_element_type=F32)
    xo = x + gate_ref[0] * y
    if final_norm:
        xo = xo * lax.rsqrt(jnp.mean(xo * xo, axis=-1, keepdims=True) + EPS) * fg_ref[...]
    out_ref[0] = xo


def _out_project(o_da, o_sb, x, norm_g, scl, shift, gate, w_gate, w_out, final_g, final_norm):
    B, S, D = x.shape
    half_spec = pl.BlockSpec((1, DA_WIDTH, OUT_ROW_TILE), lambda b, i: (b, 0, i))
    row_spec = pl.BlockSpec((1, OUT_ROW_TILE, D), lambda b, i: (b, i, 0))
    vec_spec = pl.BlockSpec((1, D), lambda b, i: (0, 0))
    mod_spec = pl.BlockSpec((1, 1, D), lambda b, i: (b, 0, 0))
    return pl.pallas_call(
        functools.partial(_out_kernel, final_norm=final_norm),
        grid=(B, S // OUT_ROW_TILE),
        in_specs=[half_spec, half_spec, row_spec, vec_spec, mod_spec, mod_spec, mod_spec,
                  pl.BlockSpec(w_gate.shape, lambda b, i: (0, 0)),
                  pl.BlockSpec(w_out.shape, lambda b, i: (0, 0)),
                  vec_spec],
        out_specs=row_spec,
        out_shape=jax.ShapeDtypeStruct((B, S, D), F32),
        compiler_params=_cparams(2),
        name="gate_out_proj",
    )(o_da, o_sb, x, norm_g.reshape(1, D), scl, shift, gate, w_gate, w_out, final_g.reshape(1, D))


def _split_w_in(w):
    sizes = (DA_WIDTH, DA_WIDTH, DA_WIDTH, DA_WIDTH, SB_WIDTH, SB_WIDTH, SB_WIDTH, SB_WIDTH)
    q_da, k_da, v_da, g_da, q_sb, k_sb, v_sb, g_sb = jnp.split(w, np.cumsum(sizes)[:-1].tolist(), axis=1)
    w_tok = jnp.concatenate([k_da, k_sb], axis=1).astype(BF16)
    w_gate = jnp.concatenate([g_da, g_sb], axis=1).astype(BF16)
    q_scale = 1.0 / math.sqrt(DA_HEAD_DIM)
    w_feat_t = jnp.concatenate([q_da * (LOG2E * q_scale), v_da, q_sb * (0.5 * q_scale), v_sb], axis=1)
    return w_tok, w_feat_t.T.astype(BF16), w_gate


def kernel(x, c, norm_g, w_ada, b_ada, w_in, lambda_q1, lambda_k1, lambda_q2, lambda_k2, subln_g, w_out,
           rel_bias, final_g):
    B, S, D = x.shape
    depth = w_in.shape[0]
    assert S % ROW_TILE == 0 and ROW_TILE % TOK_BLOCK == 0 and S % DA_BLOCK == 0
    assert S % OUT_ROW_TILE == 0 and OUT_ROW_TILE % TOK_BLOCK == 0
    assert DA_HEAD_DIM == SB_HEAD_DIM and DA_WIDTH == SB_WIDTH and DA_WIDTH + SB_WIDTH == w_out.shape[1]
    bias = _bias_tiles(rel_bias)
    for l in range(depth):
        lambda_init = _lambda_init(l)
        mod, lam = _modulation(c, w_ada[l], b_ada[l], lambda_q1[l], lambda_k1[l], lambda_q2[l], lambda_k2[l],
                               lambda_init)
        shift, scl, gate = (m.reshape(B, 1, D) for m in jnp.split(mod, 3, axis=-1))
        w_tok, w_feat_t, w_gate = _split_w_in(w_in[l])
        k_da, k_sb, q_da, v_da, q_sb, v_sb = _project(x, norm_g[l], scl, shift, w_tok, w_feat_t)
        o_da = _diff_attention(q_da, k_da, v_da, bias, lam[:, :1], subln_g[l], lambda_init)
        o_sb = _sb_attention(q_sb, k_sb, v_sb)
        x = _out_project(o_da, o_sb, x, norm_g[l], scl, shift, gate, w_gate, w_out[l].astype(BF16), final_g,
                         l == depth - 1)
    return x
```

```python
import functools
import math

import numpy as np
import jax
import jax.numpy as jnp
from jax import lax
from jax.experimental import pallas as pl
from jax.experimental.pallas import tpu as pltpu

DA_HEADS = 4
DA_HEAD_DIM = 64
DA_V_DIM = 2 * DA_HEAD_DIM
DA_WIDTH = DA_HEADS * DA_V_DIM
SB_HEADS = 8
SB_HEAD_DIM = 64
SB_WIDTH = SB_HEADS * SB_HEAD_DIM
N_BUCKETS = 32
MAX_DISTANCE = 128
EPS = 1e-6
NEG_INF = -1e30

SUBLANES = 8
LANES = 128
TOK_BLOCK = 256
CHUNK = TOK_BLOCK // SUBLANES
DA_BLOCK = 2 * TOK_BLOCK
LOG2E = math.log2(math.e)
DIAG_TILE, NEAR_TILE = 0, 1
ROW_TILE = 1024
OUT_ROW_TILE = 1024
VMEM_LIMIT = 56 * 1024 * 1024

F32 = jnp.float32
BF16 = jnp.bfloat16


def _lambda_init(layer_idx):
    return 0.8 - 0.6 * math.exp(-0.3 * layer_idx)


def _block_token(pos):
    return (pos % SUBLANES) * CHUNK + pos // SUBLANES


def _bucket_tiles():
    pos = np.arange(DA_BLOCK)
    key_tok = (pos // TOK_BLOCK) * TOK_BLOCK + _block_token(pos % TOK_BLOCK)
    rel0 = pos[None, :] - key_tok[:, None]
    rel = np.stack([np.maximum(rel0, 0), rel0 + DA_BLOCK]).astype(np.int32)
    max_exact = N_BUCKETS // 2
    nf = np.maximum(rel, 1).astype(np.float32)
    large = max_exact + (np.log(nf / np.float32(max_exact)) / np.float32(math.log(MAX_DISTANCE / max_exact))
                         * np.float32(N_BUCKETS - max_exact)).astype(np.int32)
    large = np.minimum(large, N_BUCKETS - 1)
    bucket = np.where(rel < max_exact, rel, large).astype(np.int32)
    bucket[0] = np.where(rel0 < 0, N_BUCKETS, bucket[0])
    return bucket


def _position_to_token():
    pos = np.arange(TOK_BLOCK)
    m = np.zeros((TOK_BLOCK, TOK_BLOCK), np.float32)
    m[pos, _block_token(pos)] = 1.0
    return m


def _strict_causal_tile():
    pos = np.arange(TOK_BLOCK)
    return (_block_token(pos)[:, None] < pos[None, :]).astype(np.float32)


def _value_column_scale(n):
    g = (np.arange(n) % TOK_BLOCK) // SUBLANES
    return (2.0 ** -(CHUNK - g)).astype(np.float32).reshape(1, n)


def _cparams(n_axes):
    return pltpu.CompilerParams(dimension_semantics=("arbitrary",) * n_axes, vmem_limit_bytes=VMEM_LIMIT)


def _mod_kernel(c_ref, w_ref, b_ref, lq1_ref, lk1_ref, lq2_ref, lk2_ref, mod_ref, lam_ref, *, lambda_init):
    c = c_ref[...]
    silu_c = c / (1.0 + jnp.exp(-c))
    mod_ref[...] = jnp.dot(silu_c, w_ref[...], preferred_element_type=F32,
                           precision=lax.Precision.HIGHEST) + b_ref[...]
    s1 = jnp.sum(lq1_ref[...] * lk1_ref[...], axis=-1, keepdims=True)
    s2 = jnp.sum(lq2_ref[...] * lk2_ref[...], axis=-1, keepdims=True)
    lam_ref[...] = jnp.broadcast_to(jnp.exp(s1) - jnp.exp(s2) + lambda_init, lam_ref.shape)


def _modulation(c, w_ada, b_ada, lq1, lk1, lq2, lk2, lambda_init):
    B, D = c.shape
    n_out = w_ada.shape[1]
    col = 512
    vec = lambda: pl.BlockSpec((1, DA_HEAD_DIM), lambda j: (0, 0))
    return pl.pallas_call(
        functools.partial(_mod_kernel, lambda_init=lambda_init),
        grid=(n_out // col,),
        in_specs=[pl.BlockSpec((B, D), lambda j: (0, 0)),
                  pl.BlockSpec((D, col), lambda j: (0, j)),
                  pl.BlockSpec((1, col), lambda j: (0, j)),
                  vec(), vec(), vec(), vec()],
        out_specs=[pl.BlockSpec((B, col), lambda j: (0, j)),
                   pl.BlockSpec((1, LANES), lambda j: (0, 0))],
        out_shape=[jax.ShapeDtypeStruct((B, n_out), F32), jax.ShapeDtypeStruct((1, LANES), F32)],
        compiler_params=_cparams(1),
        name="adaln_mod",
    )(c, w_ada, b_ada.reshape(1, n_out), lq1.reshape(1, -1), lk1.reshape(1, -1), lq2.reshape(1, -1),
      lk2.reshape(1, -1))


def _bias_kernel(rb_ref, bucket_ref, out_ref):
    h = pl.program_id(0)
    far = rb_ref[N_BUCKETS - 1, h]
    for t in range(2):
        bucket = bucket_ref[t]
        tile = jnp.zeros(bucket.shape, F32)
        for b in range(N_BUCKETS - 1):
            tile = jnp.where(bucket == b, (rb_ref[b, h] - far) * LOG2E, tile)
        out_ref[0, t] = jnp.where(bucket == N_BUCKETS, NEG_INF, tile)


def _bias_tiles(rel_bias):
    buckets = _bucket_tiles()
    near_rest = buckets[NEAR_TILE].copy()
    near_rest[TOK_BLOCK:, :LANES] = N_BUCKETS - 1
    assert (near_rest == N_BUCKETS - 1).all() and (buckets[DIAG_TILE][TOK_BLOCK:, :TOK_BLOCK] == N_BUCKETS).all()
    buckets = jnp.asarray(buckets)
    return pl.pallas_call(
        _bias_kernel,
        grid=(DA_HEADS,),
        in_specs=[pl.BlockSpec(memory_space=pltpu.SMEM),
                  pl.BlockSpec((2, DA_BLOCK, DA_BLOCK), lambda h: (0, 0, 0))],
        out_specs=pl.BlockSpec((1, 2, DA_BLOCK, DA_BLOCK), lambda h: (h, 0, 0, 0)),
        out_shape=jax.ShapeDtypeStruct((DA_HEADS, 2, DA_BLOCK, DA_BLOCK), F32),
        compiler_params=_cparams(1),
        name="rel_bias_tiles",
    )(rel_bias, buckets)


def _modulated_norm(x, g, scl, shift):
    ms = jnp.mean(x * x, axis=-1, keepdims=True)
    xn = x * lax.rsqrt(ms + EPS) * g
    return (xn * (1.0 + scl) + shift).astype(BF16)


def _proj_kernel(x_ref, g_ref, scl_ref, shift_ref, wt_ref, wf_ref, vscale_ref, perm_ref,
                 kda_ref, ksb_ref, qda_ref, vda_ref, qsb_ref, vsb_ref):
    h = _modulated_norm(x_ref[0], g_ref[...], scl_ref[0], shift_ref[0])
    hp = jnp.concatenate(
        [jnp.dot(perm_ref[...], h[j * TOK_BLOCK:(j + 1) * TOK_BLOCK], preferred_element_type=F32).astype(BF16)
         for j in range(h.shape[0] // TOK_BLOCK)], axis=0)
    width = kda_ref.shape[-1]
    for i, o_ref in enumerate((kda_ref, ksb_ref)):
        o_ref[0] = jnp.dot(hp, wt_ref[:, i * width:(i + 1) * width], preferred_element_type=F32).astype(BF16)
    for i, (o_ref, src) in enumerate(((qda_ref, h), (vda_ref, hp), (qsb_ref, h), (vsb_ref, hp))):
        r = lax.dot_general(wf_ref[i * width:(i + 1) * width, :], src, (((1,), (1,)), ((), ())),
                            preferred_element_type=F32)
        if o_ref is vsb_ref:
            r = r * vscale_ref[...]
        r = r.astype(BF16)
        for t in range(o_ref.shape[1]):
            o_ref[0, t] = r[:, t * TOK_BLOCK:(t + 1) * TOK_BLOCK]


def _project(x, norm_g, scl, shift, w_tok, w_feat_t):
    B, S, D = x.shape
    n_blk = S // TOK_BLOCK
    blk_per_tile = ROW_TILE // TOK_BLOCK
    width = DA_WIDTH
    tok_spec = pl.BlockSpec((1, ROW_TILE, width), lambda b, i: (b, i, 0))
    feat_spec = pl.BlockSpec((1, blk_per_tile, width, TOK_BLOCK), lambda b, i: (b, i, 0, 0))
    tok_shape = jax.ShapeDtypeStruct((B, S, width), BF16)
    feat_shape = jax.ShapeDtypeStruct((B, n_blk, width, TOK_BLOCK), BF16)
    mod_spec = pl.BlockSpec((1, 1, D), lambda b, i: (b, 0, 0))
    return pl.pallas_call(
        _proj_kernel,
        grid=(B, S // ROW_TILE),
        in_specs=[pl.BlockSpec((1, ROW_TILE, D), lambda b, i: (b, i, 0)),
                  pl.BlockSpec((1, D), lambda b, i: (0, 0)),
                  mod_spec, mod_spec,
                  pl.BlockSpec(w_tok.shape, lambda b, i: (0, 0)),
                  pl.BlockSpec(w_feat_t.shape, lambda b, i: (0, 0)),
                  pl.BlockSpec((1, ROW_TILE), lambda b, i: (0, 0)),
                  pl.BlockSpec((TOK_BLOCK, TOK_BLOCK), lambda b, i: (0, 0))],
        out_specs=[tok_spec] * 2 + [feat_spec] * 4,
        out_shape=[tok_shape] * 2 + [feat_shape] * 4,
        compiler_params=_cparams(2),
        name="norm_in_proj",
    )(x, norm_g.reshape(1, D), scl, shift, w_tok, w_feat_t, jnp.asarray(_value_column_scale(ROW_TILE)),
      jnp.asarray(_position_to_token(), BF16))


def _store_row_groups(q_t, qz_ref, group):
    row = lax.broadcasted_iota(jnp.int32, q_t.shape, 0)
    for i in range(qz_ref.shape[0]):
        inside = (row >= i * group) & (row < (i + 1) * group)
        qz_ref[i] = jnp.where(inside, q_t, jnp.zeros_like(q_t))


def _key_block(k_ref, kj, rows=TOK_BLOCK):
    return k_ref[0, pl.ds(pl.multiple_of(kj * rows, rows), rows), :]


def _da_tile_passes(n_q):
    plain = [(qi, kj) for qi in range(n_q) for kj in range(qi - 1)]
    near = [(qi, qi - 1) for qi in range(1, n_q)]
    diag = [(qi, qi) for qi in range(n_q)]
    passes, start = [], 0
    for tiles, kind in ((plain, None), (near, NEAR_TILE), (diag, DIAG_TILE)):
        passes.append((start, len(tiles), kind))
        start += len(tiles)
    order = np.asarray(plain + near + diag, np.int32)
    return order[:, 0], order[:, 1], tuple(passes)


def _da_kernel(lam_ref, qs_ref, ks_ref, q_ref, k_ref, v_ref, bias_ref, g_ref, o_ref,
               acc_ref, m_ref, l_ref, qz_ref, s_ref, smax_ref, p_ref, a_ref, *, out_scale, passes):
    n_q, n_map = acc_ref.shape[0], acc_ref.shape[1]
    n_sub = DA_BLOCK // TOK_BLOCK
    acc_ref[...] = jnp.zeros(acc_ref.shape, F32)
    m_ref[...] = jnp.full(m_ref.shape, NEG_INF, F32)
    l_ref[...] = jnp.zeros(l_ref.shape, F32)

    for qi in range(n_q):
        q_t = jnp.concatenate([q_ref[0, n_sub * qi + j] for j in range(n_sub)], axis=-1)
        _store_row_groups(q_t, qz_ref.at[qi], DA_HEAD_DIM)

    def scores(qi, kj, maps=None):
        kb = _key_block(k_ref, kj, DA_BLOCK)
        return [jnp.dot(kb, qz_ref[qi, mp], preferred_element_type=F32)
                for mp in (range(n_map) if maps is None else maps)]

    half, lane0 = TOK_BLOCK, LANES

    def put_scores(ss, kind):
        for mp in range(n_map):
            s = ss[mp]
            hd = mp // 2
            if kind == NEAR_TILE:
                corner = s[half:, :lane0] + bias_ref[hd, kind, half:, :lane0]
                s = jnp.concatenate([s[:half], jnp.concatenate([corner, s[half:, lane0:]], axis=1)], axis=0)
                s_ref[mp] = s
                smax_ref[mp] = jnp.max(s, axis=0, keepdims=True)
            elif kind == DIAG_TILE:
                top = s[:half] + bias_ref[hd, kind, :half, :]
                right = s[half:, half:] + bias_ref[hd, kind, half:, half:]
                s_ref[mp, :half, :] = top
                s_ref[mp, half:, half:] = right
                top_max = jnp.max(top, axis=0, keepdims=True)
                smax_ref[mp] = jnp.concatenate(
                    [top_max[:, :half], jnp.maximum(top_max[:, half:], jnp.max(right, axis=0, keepdims=True))], axis=1)
            else:
                s_ref[mp] = s
                smax_ref[mp] = jnp.max(s, axis=0, keepdims=True)

    def softmax_step(qi, slot, kind):
        for mp in range(n_map):
            m_old = m_ref[qi, mp]
            m_new = jnp.maximum(m_old, smax_ref[mp])
            alpha = jnp.exp2(m_old - m_new)
            if kind == DIAG_TILE:
                p_top = jnp.exp2(s_ref[mp, :half, :] - m_new)
                p_right = jnp.exp2(s_ref[mp, half:, half:] - m_new[:, half:])
                top_sum = jnp.sum(p_top, axis=0, keepdims=True)
                p_sum = jnp.concatenate(
                    [top_sum[:, :half], top_sum[:, half:] + jnp.sum(p_right, axis=0, keepdims=True)], axis=1)
                p_ref[slot, mp, :half, :] = p_top.astype(BF16)
                p_ref[slot, mp, half:, :half] = jnp.zeros((half, half), BF16)
                p_ref[slot, mp, half:, half:] = p_right.astype(BF16)
            else:
                p = jnp.exp2(s_ref[mp] - m_new)
                p_sum = jnp.sum(p, axis=0, keepdims=True)
                p_ref[slot, mp] = p.astype(BF16)
            l_ref[qi, mp] = alpha * l_ref[qi, mp] + p_sum
            m_ref[qi, mp] = m_new
            a_ref[mp] = alpha

    def weighted_values(kj, slot, maps=None):
        out = []
        for mp in (range(n_map) if maps is None else maps):
            pv = None
            for j in range(n_sub):
                vh = v_ref[0, n_sub * kj + j, (mp // 2) * DA_V_DIM:(mp // 2 + 1) * DA_V_DIM, :]
                part = jnp.dot(vh, p_ref[slot, mp, j * TOK_BLOCK:(j + 1) * TOK_BLOCK, :], preferred_element_type=F32)
                pv = part if pv is None else pv + part
            out.append(pv)
        return out

    def add_weighted(pv, qi):
        for mp in range(n_map):
            acc_ref[qi, mp] = a_ref[mp] * acc_ref[qi, mp] + pv[mp]

    for start, count, kind in passes:
        if count == 0:
            continue
        last = start + count - 1
        put_scores(scores(qs_ref[start], ks_ref[start]), kind)
        second = min(start + 1, last)
        next_scores = scores(qs_ref[second], ks_ref[second])
        softmax_step(qs_ref[start], start % 2, kind)
        put_scores(next_scores, kind)

        def one_step(cur, cur_slot, last=last, kind=kind):
            prev = cur - 1
            nxt = jnp.minimum(cur + 1, last)
            pv, next_scores = [], []
            for mp in range(n_map):
                next_scores += scores(qs_ref[nxt], ks_ref[nxt], (mp,))
                pv += weighted_values(ks_ref[prev], 1 - cur_slot, (mp,))
            add_weighted(pv, qs_ref[prev])
            softmax_step(qs_ref[cur], cur_slot, kind)
            put_scores(next_scores, kind)

        rest = count - 1
        first_slot = (start + 1) % 2

        def two_steps(j, carry, start=start, first_slot=first_slot, one_step=one_step):
            cur = start + 1 + 2 * j
            one_step(cur, first_slot)
            one_step(cur + 1, 1 - first_slot)
            return carry

        lax.fori_loop(0, rest // 2, two_steps, 0)
        if rest % 2:
            one_step(last, last % 2)
        add_weighted(weighted_values(ks_ref[last], last % 2), qs_ref[last])

    lam = lam_ref[0, 0]

    gain = jnp.concatenate([g_ref[...]] * (DA_BLOCK // LANES), axis=1) * out_scale

    def finish(qi, carry):
        for hd in range(n_map // 2):
            o_t = (acc_ref[qi, 2 * hd] / l_ref[qi, 2 * hd]
                   - lam * (acc_ref[qi, 2 * hd + 1] / l_ref[qi, 2 * hd + 1]))
            o_t = o_t * lax.rsqrt(jnp.mean(o_t * o_t, axis=0, keepdims=True) + EPS) * gain
            o_ref[0, hd * DA_V_DIM:(hd + 1) * DA_V_DIM,
                  pl.ds(pl.multiple_of(qi * DA_BLOCK, DA_BLOCK), DA_BLOCK)] = o_t.astype(o_ref.dtype)
        return carry

    lax.fori_loop(0, n_q, finish, 0)


DA_GROUP = 2


def _diff_attention(q_t, k, v_t, bias, lam, subln_g, lambda_init):
    B, S, _ = k.shape
    n_blk = S // TOK_BLOCK
    n_q = S // DA_BLOCK
    nhd = DA_GROUP
    n_map = 2 * nhd
    wide = nhd * DA_V_DIM
    qs, ks, passes = _da_tile_passes(n_q)
    smem = pl.BlockSpec(memory_space=pltpu.SMEM)
    return pl.pallas_call(
        functools.partial(_da_kernel, out_scale=1.0 - lambda_init, passes=passes),
        grid=(B, DA_HEADS // nhd),
        in_specs=[smem, smem, smem,
                  pl.BlockSpec((1, n_blk, wide, TOK_BLOCK), lambda b, h: (b, 0, h, 0)),
                  pl.BlockSpec((1, S, wide), lambda b, h: (b, 0, h)),
                  pl.BlockSpec((1, n_blk, wide, TOK_BLOCK), lambda b, h: (b, 0, h, 0)),
                  pl.BlockSpec((nhd, 2, DA_BLOCK, DA_BLOCK), lambda b, h: (h, 0, 0, 0)),
                  pl.BlockSpec((DA_V_DIM, LANES), lambda b, h: (0, 0))],
        out_specs=pl.BlockSpec((1, wide, S), lambda b, h: (b, h, 0)),
        out_shape=jax.ShapeDtypeStruct((B, DA_WIDTH, S), BF16),
        scratch_shapes=[pltpu.VMEM((n_q, n_map, DA_V_DIM, DA_BLOCK), F32),
                        pltpu.VMEM((n_q, n_map, 1, DA_BLOCK), F32),
                        pltpu.VMEM((n_q, n_map, 1, DA_BLOCK), F32),
                        pltpu.VMEM((n_q, n_map, wide, DA_BLOCK), BF16),
                        pltpu.VMEM((n_map, DA_BLOCK, DA_BLOCK), F32),
                        pltpu.VMEM((n_map, 1, DA_BLOCK), F32),
                        pltpu.VMEM((2, n_map, DA_BLOCK, DA_BLOCK), BF16),
                        pltpu.VMEM((n_map, 1, DA_BLOCK), F32)],
        compiler_params=_cparams(2),
        name="diff_attention",
    )(lam, jnp.asarray(qs), jnp.asarray(ks), q_t, k, v_t, bias,
      jnp.broadcast_to(subln_g.reshape(DA_V_DIM, 1), (DA_V_DIM, LANES)))


def _sb_kernel(q_ref, k_ref, v_ref, mask_ref, o_ref, acc_ref, carry_ref, qz_ref, u_ref, tot_ref, w_ref):
    qi = pl.program_id(2)
    nh = acc_ref.shape[0]
    slab = qz_ref.shape[1]
    slab_heads = slab // SB_HEAD_DIM
    for sl in range(nh // slab_heads):
        _store_row_groups(q_ref[0, 0, sl * slab:(sl + 1) * slab, :], qz_ref.at[pl.ds(sl * slab_heads, slab_heads)],
                          SB_HEAD_DIM)
    carry_ref[...] = jnp.ones(carry_ref.shape, F32)
    sub = lax.broadcasted_iota(jnp.int32, (SUBLANES, TOK_BLOCK), 0)
    rows = lambda x, g: x[g * SUBLANES:(g + 1) * SUBLANES]

    def logits(kj):
        kb = _key_block(k_ref, kj)
        return [jnp.dot(kb[:, (hh // slab_heads) * slab:(hh // slab_heads + 1) * slab], qz_ref[hh],
                        preferred_element_type=F32) for hh in range(nh)]

    def gates(zs, slot, mask=None):
        for hh in range(nh):
            t = jnp.tanh(zs[hh])
            if mask is not None:
                t = jnp.where(mask > 0.5, t, -1.0)
            u = 1.0 - t
            u_ref[slot, hh] = u
            tot = rows(u, CHUNK - 1)
            for g in reversed(range(CHUNK - 1)):
                tot = tot * rows(u, g)
            tot_ref[slot, hh] = tot * 2.0 ** -CHUNK

    def weights(slot, keep_carry=None):
        for hh in range(nh):
            suffix = tot_ref[slot, hh]
            for d in (1, 2, 4):
                shifted = pltpu.roll(suffix, SUBLANES - d, axis=0)
                suffix = jnp.where(sub + d < SUBLANES, suffix * shifted, suffix)
            above = jnp.where(sub + 1 < SUBLANES, pltpu.roll(suffix, SUBLANES - 1, axis=0), 1.0)
            carry = carry_ref[hh]
            new_carry = carry * jnp.broadcast_to(suffix[0:1], carry.shape)
            carry_ref[hh] = new_carry if keep_carry is None else new_carry * keep_carry
            run = above * carry
            w = [None] * CHUNK
            for g in reversed(range(CHUNK)):
                ug = u_ref[slot, hh, g * SUBLANES:(g + 1) * SUBLANES, :]
                w[g] = (2.0 - ug) * run
                run = run * ug
            w_ref[slot, hh] = jnp.concatenate(w, axis=0).astype(BF16)

    def weighted_values(kj, slot):
        vb = v_ref[0, kj]
        return [jnp.dot(vb[hh * SB_HEAD_DIM:(hh + 1) * SB_HEAD_DIM], w_ref[slot, hh], preferred_element_type=F32)
                for hh in range(nh)]

    def alive():
        return (jnp.max(carry_ref[...]) > 0.0).astype(jnp.int32)

    before = jnp.maximum(qi - 1, 0)
    z_diag = logits(qi)
    z_before = logits(before)
    gates(z_diag, 0, mask_ref[...])
    gates(z_before, 1)
    weights(0, keep_carry=(qi > 0).astype(F32))
    weights(1)
    pv_diag = weighted_values(qi, 0)
    pv_before = weighted_values(before, 1)
    for hh in range(nh):
        acc_ref[hh] = pv_diag[hh] + pv_before[hh]

    more = jnp.logical_and(qi >= 2, alive() > 0)
    pl.when(more)(lambda: gates(logits(jnp.maximum(qi - 2, 0)), 0))

    def left_cond(state):
        i, live = state
        return jnp.logical_and(i < qi - 1, live > 0)

    def left_body(state):
        i, _ = state
        kj = qi - 2 - i
        nxt = logits(jnp.maximum(kj - 1, 0))
        weights(0)
        pv = weighted_values(kj, 0)
        for hh in range(nh):
            acc_ref[hh] = acc_ref[hh] + pv[hh]
        gates(nxt, 0)
        return i + 1, alive()

    lax.while_loop(left_cond, left_body, (jnp.int32(0), more.astype(jnp.int32)))

    for hh in range(nh):
        o_ref[0, hh * SB_HEAD_DIM:(hh + 1) * SB_HEAD_DIM, :] = acc_ref[hh].astype(o_ref.dtype)


SB_GROUP = 8


def _sb_attention(q_t, k, v_t):
    B, S, _ = k.shape
    n_blk = S // TOK_BLOCK
    nh = SB_GROUP
    pair = nh * SB_HEAD_DIM
    return pl.pallas_call(
        _sb_kernel,
        grid=(B, SB_HEADS // nh, n_blk),
        in_specs=[pl.BlockSpec((1, 1, pair, TOK_BLOCK), lambda b, h, i: (b, i, h, 0)),
                  pl.BlockSpec((1, S, pair), lambda b, h, i: (b, 0, h)),
                  pl.BlockSpec((1, n_blk, pair, TOK_BLOCK), lambda b, h, i: (b, 0, h, 0)),
                  pl.BlockSpec((TOK_BLOCK, TOK_BLOCK), lambda b, h, i: (0, 0))],
        out_specs=pl.BlockSpec((1, pair, TOK_BLOCK), lambda b, h, i: (b, h, i)),
        out_shape=jax.ShapeDtypeStruct((B, SB_WIDTH, S), BF16),
        scratch_shapes=[pltpu.VMEM((nh, SB_HEAD_DIM, TOK_BLOCK), F32),
                        pltpu.VMEM((nh, SUBLANES, TOK_BLOCK), F32),
                        pltpu.VMEM((nh, TOK_BLOCK, TOK_BLOCK), BF16),
                        pltpu.VMEM((2, nh, TOK_BLOCK, TOK_BLOCK), F32),
                        pltpu.VMEM((2, nh, SUBLANES, TOK_BLOCK), F32),
                        pltpu.VMEM((2, nh, TOK_BLOCK, TOK_BLOCK), BF16)],
        compiler_params=_cparams(3),
        name="stick_breaking_attention",
    )(q_t, k, v_t, jnp.asarray(_strict_causal_tile()))


def _out_kernel(oda_ref, osb_ref, x_ref, g_ref, scl_ref, shift_ref, gate_ref, wg_ref, w_ref, fg_ref, out_ref, *,
                final_norm):
    x = x_ref[0]
    h = _modulated_norm(x, g_ref[...], scl_ref[0], shift_ref[0])
    split = oda_ref.shape[1]

    def gated(o_ref, rows):
        g = lax.dot_general(wg_ref[:, rows], h, (((0,), (1,)), ((), ())), preferred_element_type=F32)
        return (o_ref[0].astype(F32) * (g / (1.0 + jnp.exp(-g)))).astype(BF16)

    contract_features = (((0,), (0,)), ((), ()))
    y = lax.dot_general(gated(oda_ref, slice(0, split)), w_ref[0:split, :], contract_features,
                        preferred_element_type=F32)
    y = y + lax.dot_general(gated(osb_ref, slice(split, 2 * split)), w_ref[split:, :], contract_features,
                            preferred_element_type=F32)
    xo = x + gate_ref[0] * y
    if final_norm:
        xo = xo * lax.rsqrt(jnp.mean(xo * xo, axis=-1, keepdims=True) + EPS) * fg_ref[...]
    out_ref[0] = xo


def _out_project(o_da, o_sb, x, norm_g, scl, shift, gate, w_gate, w_out, final_g, final_norm):
    B, S, D = x.shape
    half_spec = pl.BlockSpec((1, DA_WIDTH, OUT_ROW_TILE), lambda b, i: (b, 0, i))
    row_spec = pl.BlockSpec((1, OUT_ROW_TILE, D), lambda b, i: (b, i, 0))
    vec_spec = pl.BlockSpec((1, D), lambda b, i: (0, 0))
    mod_spec = pl.BlockSpec((1, 1, D), lambda b, i: (b, 0, 0))
    return pl.pallas_call(
        functools.partial(_out_kernel, final_norm=final_norm),
        grid=(B, S // OUT_ROW_TILE),
        in_specs=[half_spec, half_spec, row_spec, vec_spec, mod_spec, mod_spec, mod_spec,
                  pl.BlockSpec(w_gate.shape, lambda b, i: (0, 0)),
                  pl.BlockSpec(w_out.shape, lambda b, i: (0, 0)),
                  vec_spec],
        out_specs=row_spec,
        out_shape=jax.ShapeDtypeStruct((B, S, D), F32),
        compiler_params=_cparams(2),
        name="gate_out_proj",
    )(o_da, o_sb, x, norm_g.reshape(1, D), scl, shift, gate, w_gate, w_out, final_g.reshape(1, D))


def _split_w_in(w):
    sizes = (DA_WIDTH, DA_WIDTH, DA_WIDTH, DA_WIDTH, SB_WIDTH, SB_WIDTH, SB_WIDTH, SB_WIDTH)
    q_da, k_da, v_da, g_da, q_sb, k_sb, v_sb, g_sb = jnp.split(w, np.cumsum(sizes)[:-1].tolist(), axis=1)
    w_tok = jnp.concatenate([k_da, k_sb], axis=1).astype(BF16)
    w_gate = jnp.concatenate([g_da, g_sb], axis=1).astype(BF16)
    q_scale = 1.0 / math.sqrt(DA_HEAD_DIM)
    w_feat_t = jnp.concatenate([q_da * (LOG2E * q_scale), v_da, q_sb * (0.5 * q_scale), v_sb], axis=1)
    return w_tok, w_feat_t.T.astype(BF16), w_gate


def kernel(x, c, norm_g, w_ada, b_ada, w_in, lambda_q1, lambda_k1, lambda_q2, lambda_k2, subln_g, w_out,
           rel_bias, final_g):
    B, S, D = x.shape
    depth = w_in.shape[0]
    assert S % ROW_TILE == 0 and ROW_TILE % TOK_BLOCK == 0 and S % DA_BLOCK == 0
    assert S % OUT_ROW_TILE == 0 and OUT_ROW_TILE % TOK_BLOCK == 0
    assert DA_HEAD_DIM == SB_HEAD_DIM and DA_WIDTH == SB_WIDTH and DA_WIDTH + SB_WIDTH == w_out.shape[1]
    bias = _bias_tiles(rel_bias)
    for l in range(depth):
        lambda_init = _lambda_init(l)
        mod, lam = _modulation(c, w_ada[l], b_ada[l], lambda_q1[l], lambda_k1[l], lambda_q2[l], lambda_k2[l],
                               lambda_init)
        shift, scl, gate = (m.reshape(B, 1, D) for m in jnp.split(mod, 3, axis=-1))
        w_tok, w_feat_t, w_gate = _split_w_in(w_in[l])
        k_da, k_sb, q_da, v_da, q_sb, v_sb = _project(x, norm_g[l], scl, shift, w_tok, w_feat_t)
        o_da = _diff_attention(q_da, k_da, v_da, bias, lam[:, :1], subln_g[l], lambda_init)
        o_sb = _sb_attention(q_sb, k_sb, v_sb)
        x = _out_project(o_da, o_sb, x, norm_g[l], scl, shift, gate, w_gate, w_out[l].astype(BF16), final_g,
                         l == depth - 1)
    return x
```

```python
import functools
import math

import numpy as np
import jax
import jax.numpy as jnp
from jax import lax
from jax.experimental import pallas as pl
from jax.experimental.pallas import tpu as pltpu

DA_HEADS = 4
DA_HEAD_DIM = 64
DA_V_DIM = 2 * DA_HEAD_DIM
DA_WIDTH = DA_HEADS * DA_V_DIM
SB_HEADS = 8
SB_HEAD_DIM = 64
SB_WIDTH = SB_HEADS * SB_HEAD_DIM
N_BUCKETS = 32
MAX_DISTANCE = 128
EPS = 1e-6
NEG_INF = -1e30

SUBLANES = 8
LANES = 128
TOK_BLOCK = 256
CHUNK = TOK_BLOCK // SUBLANES
DA_BLOCK = 2 * TOK_BLOCK
LOG2E = math.log2(math.e)
DIAG_TILE, NEAR_TILE = 0, 1
ROW_TILE = 1024
OUT_ROW_TILE = 1024
VMEM_LIMIT = 56 * 1024 * 1024

F32 = jnp.float32
BF16 = jnp.bfloat16


def _lambda_init(layer_idx):
    return 0.8 - 0.6 * math.exp(-0.3 * layer_idx)


def _block_token(pos):
    return (pos % SUBLANES) * CHUNK + pos // SUBLANES


def _bucket_tiles():
    pos = np.arange(DA_BLOCK)
    key_tok = (pos // TOK_BLOCK) * TOK_BLOCK + _block_token(pos % TOK_BLOCK)
    rel0 = pos[None, :] - key_tok[:, None]
    rel = np.stack([np.maximum(rel0, 0), rel0 + DA_BLOCK]).astype(np.int32)
    max_exact = N_BUCKETS // 2
    nf = np.maximum(rel, 1).astype(np.float32)
    large = max_exact + (np.log(nf / np.float32(max_exact)) / np.float32(math.log(MAX_DISTANCE / max_exact))
                         * np.float32(N_BUCKETS - max_exact)).astype(np.int32)
    large = np.minimum(large, N_BUCKETS - 1)
    bucket = np.where(rel < max_exact, rel, large).astype(np.int32)
    bucket[0] = np.where(rel0 < 0, N_BUCKETS, bucket[0])
    return bucket


def _position_to_token():
    pos = np.arange(TOK_BLOCK)
    m = np.zeros((TOK_BLOCK, TOK_BLOCK), np.float32)
    m[pos, _block_token(pos)] = 1.0
    return m


def _strict_causal_tile():
    pos = np.arange(TOK_BLOCK)
    return (_block_token(pos)[:, None] < pos[None, :]).astype(np.float32)


def _value_column_scale(n):
    g = (np.arange(n) % TOK_BLOCK) // SUBLANES
    return (2.0 ** -(CHUNK - g)).astype(np.float32).reshape(1, n)


def _cparams(n_axes):
    return pltpu.CompilerParams(dimension_semantics=("arbitrary",) * n_axes, vmem_limit_bytes=VMEM_LIMIT)


def _mod_kernel(c_ref, w_ref, b_ref, lq1_ref, lk1_ref, lq2_ref, lk2_ref, mod_ref, lam_ref, *, lambda_init):
    c = c_ref[...]
    silu_c = c / (1.0 + jnp.exp(-c))
    mod_ref[...] = jnp.dot(silu_c, w_ref[...], preferred_element_type=F32,
                           precision=lax.Precision.HIGHEST) + b_ref[...]
    s1 = jnp.sum(lq1_ref[...] * lk1_ref[...], axis=-1, keepdims=True)
    s2 = jnp.sum(lq2_ref[...] * lk2_ref[...], axis=-1, keepdims=True)
    lam_ref[...] = jnp.broadcast_to(jnp.exp(s1) - jnp.exp(s2) + lambda_init, lam_ref.shape)


def _modulation(c, w_ada, b_ada, lq1, lk1, lq2, lk2, lambda_init):
    B, D = c.shape
    n_out = w_ada.shape[1]
    col = 512
    vec = lambda: pl.BlockSpec((1, DA_HEAD_DIM), lambda j: (0, 0))
    return pl.pallas_call(
        functools.partial(_mod_kernel, lambda_init=lambda_init),
        grid=(n_out // col,),
        in_specs=[pl.BlockSpec((B, D), lambda j: (0, 0)),
                  pl.BlockSpec((D, col), lambda j: (0, j)),
                  pl.BlockSpec((1, col), lambda j: (0, j)),
                  vec(), vec(), vec(), vec()],
        out_specs=[pl.BlockSpec((B, col), lambda j: (0, j)),
                   pl.BlockSpec((1, LANES), lambda j: (0, 0))],
        out_shape=[jax.ShapeDtypeStruct((B, n_out), F32), jax.ShapeDtypeStruct((1, LANES), F32)],
        compiler_params=_cparams(1),
        name="adaln_mod",
    )(c, w_ada, b_ada.reshape(1, n_out), lq1.reshape(1, -1), lk1.reshape(1, -1), lq2.reshape(1, -1),
      lk2.reshape(1, -1))


def _bias_kernel(rb_ref, *refs):
    h = pl.program_id(0)
    far = rb_ref[N_BUCKETS - 1, h]
    n = len(refs) // 2
    for bucket_ref, out_ref in zip(refs[:n], refs[n:]):
        bucket = bucket_ref[...]
        tile = jnp.zeros(bucket.shape, F32)
        for b in range(N_BUCKETS - 1):
            tile = jnp.where(bucket == b, (rb_ref[b, h] - far) * LOG2E, tile)
        out_ref[0] = jnp.where(bucket == N_BUCKETS, NEG_INF, tile)


def _bias_tiles(rel_bias):
    buckets = _bucket_tiles()
    near_rest = buckets[NEAR_TILE].copy()
    near_rest[TOK_BLOCK:, :LANES] = N_BUCKETS - 1
    assert (near_rest == N_BUCKETS - 1).all() and (buckets[DIAG_TILE][TOK_BLOCK:, :TOK_BLOCK] == N_BUCKETS).all()
    pieces = [buckets[DIAG_TILE][:TOK_BLOCK, :], buckets[DIAG_TILE][TOK_BLOCK:, TOK_BLOCK:],
              buckets[NEAR_TILE][TOK_BLOCK:, :LANES]]
    return pl.pallas_call(
        _bias_kernel,
        grid=(DA_HEADS,),
        in_specs=[pl.BlockSpec(memory_space=pltpu.SMEM)] + [pl.BlockSpec(p.shape, lambda h: (0, 0)) for p in pieces],
        out_specs=[pl.BlockSpec((1,) + p.shape, lambda h: (h, 0, 0)) for p in pieces],
        out_shape=[jax.ShapeDtypeStruct((DA_HEADS,) + p.shape, F32) for p in pieces],
        compiler_params=_cparams(1),
        name="rel_bias_tiles",
    )(rel_bias, *[jnp.asarray(p) for p in pieces])


def _modulated_norm(x, g, scl, shift):
    ms = jnp.mean(x * x, axis=-1, keepdims=True)
    xn = x * lax.rsqrt(ms + EPS) * g
    return (xn * (1.0 + scl) + shift).astype(BF16)


def _proj_kernel(x_ref, g_ref, scl_ref, shift_ref, wt_ref, wf_ref, vscale_ref, perm_ref,
                 kda_ref, ksb_ref, qda_ref, vda_ref, qsb_ref, vsb_ref):
    h = _modulated_norm(x_ref[0], g_ref[...], scl_ref[0], shift_ref[0])
    hp = jnp.concatenate(
        [jnp.dot(perm_ref[...], h[j * TOK_BLOCK:(j + 1) * TOK_BLOCK], preferred_element_type=F32).astype(BF16)
         for j in range(h.shape[0] // TOK_BLOCK)], axis=0)
    width = kda_ref.shape[-1]
    for i, o_ref in enumerate((kda_ref, ksb_ref)):
        o_ref[0] = jnp.dot(hp, wt_ref[:, i * width:(i + 1) * width], preferred_element_type=F32).astype(BF16)
    for i, (o_ref, src) in enumerate(((qda_ref, h), (vda_ref, hp), (qsb_ref, h), (vsb_ref, hp))):
        r = lax.dot_general(wf_ref[i * width:(i + 1) * width, :], src, (((1,), (1,)), ((), ())),
                            preferred_element_type=F32)
        if o_ref is vsb_ref:
            r = r * vscale_ref[...]
        r = r.astype(BF16)
        for t in range(o_ref.shape[1]):
            o_ref[0, t] = r[:, t * TOK_BLOCK:(t + 1) * TOK_BLOCK]


def _project(x, norm_g, scl, shift, w_tok, w_feat_t):
    B, S, D = x.shape
    n_blk = S // TOK_BLOCK
    blk_per_tile = ROW_TILE // TOK_BLOCK
    width = DA_WIDTH
    tok_spec = pl.BlockSpec((1, ROW_TILE, width), lambda b, i: (b, i, 0))
    feat_spec = pl.BlockSpec((1, blk_per_tile, width, TOK_BLOCK), lambda b, i: (b, i, 0, 0))
    tok_shape = jax.ShapeDtypeStruct((B, S, width), BF16)
    feat_shape = jax.ShapeDtypeStruct((B, n_blk, width, TOK_BLOCK), BF16)
    mod_spec = pl.BlockSpec((1, 1, D), lambda b, i: (b, 0, 0))
    return pl.pallas_call(
        _proj_kernel,
        grid=(B, S // ROW_TILE),
        in_specs=[pl.BlockSpec((1, ROW_TILE, D), lambda b, i: (b, i, 0)),
                  pl.BlockSpec((1, D), lambda b, i: (0, 0)),
                  mod_spec, mod_spec,
                  pl.BlockSpec(w_tok.shape, lambda b, i: (0, 0)),
                  pl.BlockSpec(w_feat_t.shape, lambda b, i: (0, 0)),
                  pl.BlockSpec((1, ROW_TILE), lambda b, i: (0, 0)),
                  pl.BlockSpec((TOK_BLOCK, TOK_BLOCK), lambda b, i: (0, 0))],
        out_specs=[tok_spec] * 2 + [feat_spec] * 4,
        out_shape=[tok_shape] * 2 + [feat_shape] * 4,
        compiler_params=_cparams(2),
        name="norm_in_proj",
    )(x, norm_g.reshape(1, D), scl, shift, w_tok, w_feat_t, jnp.asarray(_value_column_scale(ROW_TILE)),
      jnp.asarray(_position_to_token(), BF16))


def _store_row_groups(q_t, qz_ref, group):
    row = lax.broadcasted_iota(jnp.int32, q_t.shape, 0)
    for i in range(qz_ref.shape[0]):
        inside = (row >= i * group) & (row < (i + 1) * group)
        qz_ref[i] = jnp.where(inside, q_t, jnp.zeros_like(q_t))


def _key_block(k_ref, kj, rows=TOK_BLOCK):
    return k_ref[0, pl.ds(pl.multiple_of(kj * rows, rows), rows), :]


def _da_tile_passes(n_q):
    plain = [(qi, kj) for qi in range(n_q) for kj in range(qi - 1)]
    near = [(qi, qi - 1) for qi in range(1, n_q)]
    diag = [(qi, qi) for qi in range(n_q)]
    passes, start = [], 0
    for tiles, kind in ((plain, None), (near, NEAR_TILE), (diag, DIAG_TILE)):
        passes.append((start, len(tiles), kind))
        start += len(tiles)
    order = np.asarray(plain + near + diag, np.int32)
    return order[:, 0], order[:, 1], tuple(passes)


def _da_kernel(lam_ref, qs_ref, ks_ref, q_ref, k_ref, v_ref, top_bias_ref, right_bias_ref, near_bias_ref, g_ref, o_ref,
               acc_ref, m_ref, l_ref, qz_ref, s_ref, smax_ref, p_ref, a_ref, *, out_scale, passes):
    n_q, n_map = acc_ref.shape[0], acc_ref.shape[1]
    n_sub = DA_BLOCK // TOK_BLOCK
    acc_ref[...] = jnp.zeros(acc_ref.shape, F32)
    m_ref[...] = jnp.full(m_ref.shape, NEG_INF, F32)
    l_ref[...] = jnp.zeros(l_ref.shape, F32)

    for qi in range(n_q):
        q_t = jnp.concatenate([q_ref[0, n_sub * qi + j] for j in range(n_sub)], axis=-1)
        _store_row_groups(q_t, qz_ref.at[qi], DA_HEAD_DIM)

    def scores(qi, kj, maps=None):
        kb = _key_block(k_ref, kj, DA_BLOCK)
        return [jnp.dot(kb, qz_ref[qi, mp], preferred_element_type=F32)
                for mp in (range(n_map) if maps is None else maps)]

    half, lane0 = TOK_BLOCK, LANES

    def put_scores(ss, kind):
        for mp in range(n_map):
            s = ss[mp]
            hd = mp // 2
            if kind == NEAR_TILE:
                corner = s[half:, :lane0] + near_bias_ref[hd]
                s = jnp.concatenate([s[:half], jnp.concatenate([corner, s[half:, lane0:]], axis=1)], axis=0)
                s_ref[mp] = s
                smax_ref[mp] = jnp.max(s, axis=0, keepdims=True)
            elif kind == DIAG_TILE:
                top = s[:half] + top_bias_ref[hd]
                right = s[half:, half:] + right_bias_ref[hd]
                s_ref[mp, :half, :] = top
                s_ref[mp, half:, half:] = right
                top_max = jnp.max(top, axis=0, keepdims=True)
                smax_ref[mp] = jnp.concatenate(
                    [top_max[:, :half], jnp.maximum(top_max[:, half:], jnp.max(right, axis=0, keepdims=True))], axis=1)
            else:
                s_ref[mp] = s
                smax_ref[mp] = jnp.max(s, axis=0, keepdims=True)

    def softmax_step(qi, slot, kind):
        for mp in range(n_map):
            m_old = m_ref[qi, mp]
            m_new = jnp.maximum(m_old, smax_ref[mp])
            alpha = jnp.exp2(m_old - m_new)
            if kind == DIAG_TILE:
                p_top = jnp.exp2(s_ref[mp, :half, :] - m_new)
                p_right = jnp.exp2(s_ref[mp, half:, half:] - m_new[:, half:])
                top_sum = jnp.sum(p_top, axis=0, keepdims=True)
                p_sum = jnp.concatenate(
                    [top_sum[:, :half], top_sum[:, half:] + jnp.sum(p_right, axis=0, keepdims=True)], axis=1)
                p_ref[slot, mp, :half, :] = p_top.astype(BF16)
                p_ref[slot, mp, half:, :half] = jnp.zeros((half, half), BF16)
                p_ref[slot, mp, half:, half:] = p_right.astype(BF16)
            else:
                p = jnp.exp2(s_ref[mp] - m_new)
                p_sum = jnp.sum(p, axis=0, keepdims=True)
                p_ref[slot, mp] = p.astype(BF16)
            l_ref[qi, mp] = alpha * l_ref[qi, mp] + p_sum
            m_ref[qi, mp] = m_new
            a_ref[mp] = alpha

    def weighted_values(kj, slot, maps=None):
        out = []
        for mp in (range(n_map) if maps is None else maps):
            pv = None
            for j in range(n_sub):
                vh = v_ref[0, n_sub * kj + j, (mp // 2) * DA_V_DIM:(mp // 2 + 1) * DA_V_DIM, :]
                part = jnp.dot(vh, p_ref[slot, mp, j * TOK_BLOCK:(j + 1) * TOK_BLOCK, :], preferred_element_type=F32)
                pv = part if pv is None else pv + part
            out.append(pv)
        return out

    def add_weighted(pv, qi):
        for mp in range(n_map):
            acc_ref[qi, mp] = a_ref[mp] * acc_ref[qi, mp] + pv[mp]

    for start, count, kind in passes:
        if count == 0:
            continue
        last = start + count - 1
        put_scores(scores(qs_ref[start], ks_ref[start]), kind)
        second = min(start + 1, last)
        next_scores = scores(qs_ref[second], ks_ref[second])
        softmax_step(qs_ref[start], start % 2, kind)
        put_scores(next_scores, kind)

        def one_step(cur, cur_slot, last=last, kind=kind):
            prev = cur - 1
            nxt = jnp.minimum(cur + 1, last)
            pv, next_scores = [], []
            for mp in range(n_map):
                next_scores += scores(qs_ref[nxt], ks_ref[nxt], (mp,))
                pv += weighted_values(ks_ref[prev], 1 - cur_slot, (mp,))
            add_weighted(pv, qs_ref[prev])
            softmax_step(qs_ref[cur], cur_slot, kind)
            put_scores(next_scores, kind)

        rest = count - 1
        first_slot = (start + 1) % 2

        def two_steps(j, carry, start=start, first_slot=first_slot, one_step=one_step):
            cur = start + 1 + 2 * j
            one_step(cur, first_slot)
            one_step(cur + 1, 1 - first_slot)
            return carry

        lax.fori_loop(0, rest // 2, two_steps, 0)
        if rest % 2:
            one_step(last, last % 2)
        add_weighted(weighted_values(ks_ref[last], last % 2), qs_ref[last])

    lam = lam_ref[0, 0]

    gain = jnp.concatenate([g_ref[...]] * (DA_BLOCK // LANES), axis=1) * out_scale

    def finish(qi, carry):
        for hd in range(n_map // 2):
            o_t = (acc_ref[qi, 2 * hd] / l_ref[qi, 2 * hd]
                   - lam * (acc_ref[qi, 2 * hd + 1] / l_ref[qi, 2 * hd + 1]))
            o_t = o_t * lax.rsqrt(jnp.mean(o_t * o_t, axis=0, keepdims=True) + EPS) * gain
            o_ref[0, hd * DA_V_DIM:(hd + 1) * DA_V_DIM,
                  pl.ds(pl.multiple_of(qi * DA_BLOCK, DA_BLOCK), DA_BLOCK)] = o_t.astype(o_ref.dtype)
        return carry

    lax.fori_loop(0, n_q, finish, 0)


DA_GROUP = 2


def _diff_attention(q_t, k, v_t, bias_pieces, lam, subln_g, lambda_init):
    B, S, _ = k.shape
    n_blk = S // TOK_BLOCK
    n_q = S // DA_BLOCK
    nhd = DA_GROUP
    n_map = 2 * nhd
    wide = nhd * DA_V_DIM
    qs, ks, passes = _da_tile_passes(n_q)
    smem = pl.BlockSpec(memory_space=pltpu.SMEM)
    return pl.pallas_call(
        functools.partial(_da_kernel, out_scale=1.0 - lambda_init, passes=passes),
        grid=(B, DA_HEADS // nhd),
        in_specs=[smem, smem, smem,
                  pl.BlockSpec((1, n_blk, wide, TOK_BLOCK), lambda b, h: (b, 0, h, 0)),
                  pl.BlockSpec((1, S, wide), lambda b, h: (b, 0, h)),
                  pl.BlockSpec((1, n_blk, wide, TOK_BLOCK), lambda b, h: (b, 0, h, 0)),
                  *[pl.BlockSpec((nhd,) + p.shape[1:], lambda b, h: (h, 0, 0)) for p in bias_pieces],
                  pl.BlockSpec((DA_V_DIM, LANES), lambda b, h: (0, 0))],
        out_specs=pl.BlockSpec((1, wide, S), lambda b, h: (b, h, 0)),
        out_shape=jax.ShapeDtypeStruct((B, DA_WIDTH, S), BF16),
        scratch_shapes=[pltpu.VMEM((n_q, n_map, DA_V_DIM, DA_BLOCK), F32),
                        pltpu.VMEM((n_q, n_map, 1, DA_BLOCK), F32),
                        pltpu.VMEM((n_q, n_map, 1, DA_BLOCK), F32),
                        pltpu.VMEM((n_q, n_map, wide, DA_BLOCK), BF16),
                        pltpu.VMEM((n_map, DA_BLOCK, DA_BLOCK), F32),
                        pltpu.VMEM((n_map, 1, DA_BLOCK), F32),
                        pltpu.VMEM((2, n_map, DA_BLOCK, DA_BLOCK), BF16),
                        pltpu.VMEM((n_map, 1, DA_BLOCK), F32)],
        compiler_params=_cparams(2),
        name="diff_attention",
    )(lam, jnp.asarray(qs), jnp.asarray(ks), q_t, k, v_t, *bias_pieces,
      jnp.broadcast_to(subln_g.reshape(DA_V_DIM, 1), (DA_V_DIM, LANES)))


def _sb_kernel(q_ref, k_ref, v_ref, mask_ref, o_ref, acc_ref, carry_ref, qz_ref, u_ref, tot_ref, w_ref):
    qi = pl.program_id(2)
    nh = acc_ref.shape[0]
    slab = qz_ref.shape[1]
    slab_heads = slab // SB_HEAD_DIM
    for sl in range(nh // slab_heads):
        _store_row_groups(q_ref[0, 0, sl * slab:(sl + 1) * slab, :], qz_ref.at[pl.ds(sl * slab_heads, slab_heads)],
                          SB_HEAD_DIM)
    carry_ref[...] = jnp.ones(carry_ref.shape, F32)
    sub = lax.broadcasted_iota(jnp.int32, (SUBLANES, TOK_BLOCK), 0)
    rows = lambda x, g: x[g * SUBLANES:(g + 1) * SUBLANES]

    def logits(kj):
        kb = _key_block(k_ref, kj)
        return [jnp.dot(kb[:, (hh // slab_heads) * slab:(hh // slab_heads + 1) * slab], qz_ref[hh],
                        preferred_element_type=F32) for hh in range(nh)]

    def gates(zs, slot, mask=None):
        for hh in range(nh):
            t = jnp.tanh(zs[hh])
            if mask is not None:
                t = jnp.where(mask > 0.5, t, -1.0)
            u = 1.0 - t
            u_ref[slot, hh] = u
            tot = rows(u, CHUNK - 1)
            for g in reversed(range(CHUNK - 1)):
                tot = tot * rows(u, g)
            tot_ref[slot, hh] = tot * 2.0 ** -CHUNK

    def weights(slot, keep_carry=None):
        for hh in range(nh):
            suffix = tot_ref[slot, hh]
            for d in (1, 2, 4):
                shifted = pltpu.roll(suffix, SUBLANES - d, axis=0)
                suffix = jnp.where(sub + d < SUBLANES, suffix * shifted, suffix)
            above = jnp.where(sub + 1 < SUBLANES, pltpu.roll(suffix, SUBLANES - 1, axis=0), 1.0)
            carry = carry_ref[hh]
            new_carry = carry * jnp.broadcast_to(suffix[0:1], carry.shape)
            carry_ref[hh] = new_carry if keep_carry is None else new_carry * keep_carry
            run = above * carry
            w = [None] * CHUNK
            for g in reversed(range(CHUNK)):
                ug = u_ref[slot, hh, g * SUBLANES:(g + 1) * SUBLANES, :]
                w[g] = (2.0 - ug) * run
                run = run * ug
            w_ref[slot, hh] = jnp.concatenate(w, axis=0).astype(BF16)

    def weighted_values(kj, slot):
        vb = v_ref[0, kj]
        return [jnp.dot(vb[hh * SB_HEAD_DIM:(hh + 1) * SB_HEAD_DIM], w_ref[slot, hh], preferred_element_type=F32)
                for hh in range(nh)]

    def alive():
        return (jnp.max(carry_ref[...]) > 0.0).astype(jnp.int32)

    before = jnp.maximum(qi - 1, 0)
    z_diag = logits(qi)
    z_before = logits(before)
    gates(z_diag, 0, mask_ref[...])
    gates(z_before, 1)
    weights(0, keep_carry=(qi > 0).astype(F32))
    weights(1)
    pv_diag = weighted_values(qi, 0)
    pv_before = weighted_values(before, 1)
    for hh in range(nh):
        acc_ref[hh] = pv_diag[hh] + pv_before[hh]

    more = jnp.logical_and(qi >= 2, alive() > 0)
    pl.when(more)(lambda: gates(logits(jnp.maximum(qi - 2, 0)), 0))

    def left_cond(state):
        i, live = state
        return jnp.logical_and(i < qi - 1, live > 0)

    def left_body(state):
        i, _ = state
        kj = qi - 2 - i
        nxt = logits(jnp.maximum(kj - 1, 0))
        weights(0)
        pv = weighted_values(kj, 0)
        for hh in range(nh):
            acc_ref[hh] = acc_ref[hh] + pv[hh]
        gates(nxt, 0)
        return i + 1, alive()

    lax.while_loop(left_cond, left_body, (jnp.int32(0), more.astype(jnp.int32)))

    for hh in range(nh):
        o_ref[0, hh * SB_HEAD_DIM:(hh + 1) * SB_HEAD_DIM, :] = acc_ref[hh].astype(o_ref.dtype)


SB_GROUP = 8


def _sb_attention(q_t, k, v_t):
    B, S, _ = k.shape
    n_blk = S // TOK_BLOCK
    nh = SB_GROUP
    pair = nh * SB_HEAD_DIM
    return pl.pallas_call(
        _sb_kernel,
        grid=(B, SB_HEADS // nh, n_blk),
        in_specs=[pl.BlockSpec((1, 1, pair, TOK_BLOCK), lambda b, h, i: (b, i, h, 0)),
                  pl.BlockSpec((1, S, pair), lambda b, h, i: (b, 0, h)),
                  pl.BlockSpec((1, n_blk, pair, TOK_BLOCK), lambda b, h, i: (b, 0, h, 0)),
                  pl.BlockSpec((TOK_BLOCK, TOK_BLOCK), lambda b, h, i: (0, 0))],
        out_specs=pl.BlockSpec((1, pair, TOK_BLOCK), lambda b, h, i: (b, h, i)),
        out_shape=jax.ShapeDtypeStruct((B, SB_WIDTH, S), BF16),
        scratch_shapes=[pltpu.VMEM((nh, SB_HEAD_DIM, TOK_BLOCK), F32),
                        pltpu.VMEM((nh, SUBLANES, TOK_BLOCK), F32),
                        pltpu.VMEM((nh, TOK_BLOCK, TOK_BLOCK), BF16),
                        pltpu.VMEM((2, nh, TOK_BLOCK, TOK_BLOCK), F32),
                        pltpu.VMEM((2, nh, SUBLANES, TOK_BLOCK), F32),
                        pltpu.VMEM((2, nh, TOK_BLOCK, TOK_BLOCK), BF16)],
        compiler_params=_cparams(3),
        name="stick_breaking_attention",
    )(q_t, k, v_t, jnp.asarray(_strict_causal_tile()))


def _out_kernel(oda_ref, osb_ref, x_ref, g_ref, scl_ref, shift_ref, gate_ref, wg_ref, w_ref, fg_ref, out_ref, *,
                final_norm):
    x = x_ref[0]
    h = _modulated_norm(x, g_ref[...], scl_ref[0], shift_ref[0])
    split = oda_ref.shape[1]

    def gated(o_ref, rows):
        g = lax.dot_general(wg_ref[:, rows], h, (((0,), (1,)), ((), ())), preferred_element_type=F32)
        return (o_ref[0].astype(F32) * (g / (1.0 + jnp.exp(-g)))).astype(BF16)

    contract_features = (((0,), (0,)), ((), ()))
    y = lax.dot_general(gated(oda_ref, slice(0, split)), w_ref[0:split, :], contract_features,
                        preferred_element_type=F32)
    y = y + lax.dot_general(gated(osb_ref, slice(split, 2 * split)), w_ref[split:, :], contract_features,
                            preferred_element_type=F32)
    xo = x + gate_ref[0] * y
    if final_norm:
        xo = xo * lax.rsqrt(jnp.mean(xo * xo, axis=-1, keepdims=True) + EPS) * fg_ref[...]
    out_ref[0] = xo


def _out_project(o_da, o_sb, x, norm_g, scl, shift, gate, w_gate, w_out, final_g, final_norm):
    B, S, D = x.shape
    half_spec = pl.BlockSpec((1, DA_WIDTH, OUT_ROW_TILE), lambda b, i: (b, 0, i))
    row_spec = pl.BlockSpec((1, OUT_ROW_TILE, D), lambda b, i: (b, i, 0))
    vec_spec = pl.BlockSpec((1, D), lambda b, i: (0, 0))
    mod_spec = pl.BlockSpec((1, 1, D), lambda b, i: (b, 0, 0))
    return pl.pallas_call(
        functools.partial(_out_kernel, final_norm=final_norm),
        grid=(B, S // OUT_ROW_TILE),
        in_specs=[half_spec, half_spec, row_spec, vec_spec, mod_spec, mod_spec, mod_spec,
                  pl.BlockSpec(w_gate.shape, lambda b, i: (0, 0)),
                  pl.BlockSpec(w_out.shape, lambda b, i: (0, 0)),
                  vec_spec],
        out_specs=row_spec,
        out_shape=jax.ShapeDtypeStruct((B, S, D), F32),
        compiler_params=_cparams(2),
        name="gate_out_proj",
    )(o_da, o_sb, x, norm_g.reshape(1, D), scl, shift, gate, w_gate, w_out, final_g.reshape(1, D))


def _split_w_in(w):
    sizes = (DA_WIDTH, DA_WIDTH, DA_WIDTH, DA_WIDTH, SB_WIDTH, SB_WIDTH, SB_WIDTH, SB_WIDTH)
    q_da, k_da, v_da, g_da, q_sb, k_sb, v_sb, g_sb = jnp.split(w, np.cumsum(sizes)[:-1].tolist(), axis=1)
    w_tok = jnp.concatenate([k_da, k_sb], axis=1).astype(BF16)
    w_gate = jnp.concatenate([g_da, g_sb], axis=1).astype(BF16)
    q_scale = 1.0 / math.sqrt(DA_HEAD_DIM)
    w_feat_t = jnp.concatenate([q_da * (LOG2E * q_scale), v_da, q_sb * (0.5 * q_scale), v_sb], axis=1)
    return w_tok, w_feat_t.T.astype(BF16), w_gate


def kernel(x, c, norm_g, w_ada, b_ada, w_in, lambda_q1, lambda_k1, lambda_q2, lambda_k2, subln_g, w_out,
           rel_bias, final_g):
    B, S, D = x.shape
    depth = w_in.shape[0]
    assert S % ROW_TILE == 0 and ROW_TILE % TOK_BLOCK == 0 and S % DA_BLOCK == 0
    assert S % OUT_ROW_TILE == 0 and OUT_ROW_TILE % TOK_BLOCK == 0
    assert DA_HEAD_DIM == SB_HEAD_DIM and DA_WIDTH == SB_WIDTH and DA_WIDTH + SB_WIDTH == w_out.shape[1]
    bias = _bias_tiles(rel_bias)
    for l in range(depth):
        lambda_init = _lambda_init(l)
        mod, lam = _modulation(c, w_ada[l], b_ada[l], lambda_q1[l], lambda_k1[l], lambda_q2[l], lambda_k2[l],
                               lambda_init)
        shift, scl, gate = (m.reshape(B, 1, D) for m in jnp.split(mod, 3, axis=-1))
        w_tok, w_feat_t, w_gate = _split_w_in(w_in[l])
        k_da, k_sb, q_da, v_da, q_sb, v_sb = _project(x, norm_g[l], scl, shift, w_tok, w_feat_t)
        o_da = _diff_attention(q_da, k_da, v_da, bias, lam[:, :1], subln_g[l], lambda_init)
        o_sb = _sb_attention(q_sb, k_sb, v_sb)
        x = _out_project(o_da, o_sb, x, norm_g[l], scl, shift, gate, w_gate, w_out[l].astype(BF16), final_g,
                         l == depth - 1)
    return x
```

```python
import functools
import math

import numpy as np
import jax
import jax.numpy as jnp
from jax import lax
from jax.experimental import pallas as pl
from jax.experimental.pallas import tpu as pltpu

DA_HEADS = 4
DA_HEAD_DIM = 64
DA_V_DIM = 2 * DA_HEAD_DIM
DA_WIDTH = DA_HEADS * DA_V_DIM
SB_HEADS = 8
SB_HEAD_DIM = 64
SB_WIDTH = SB_HEADS * SB_HEAD_DIM
N_BUCKETS = 32
MAX_DISTANCE = 128
EPS = 1e-6
NEG_INF = -1e30

SUBLANES = 8
LANES = 128
TOK_BLOCK = 256
CHUNK = TOK_BLOCK // SUBLANES
DA_BLOCK = 2 * TOK_BLOCK
LOG2E = math.log2(math.e)
DIAG_TILE, NEAR_TILE = 0, 1
ROW_TILE = 1024
OUT_ROW_TILE = 1024
VMEM_LIMIT = 56 * 1024 * 1024

F32 = jnp.float32
BF16 = jnp.bfloat16


def _lambda_init(layer_idx):
    return 0.8 - 0.6 * math.exp(-0.3 * layer_idx)


def _block_token(pos):
    return (pos % SUBLANES) * CHUNK + pos // SUBLANES


def _bucket_tiles():
    pos = np.arange(DA_BLOCK)
    key_tok = (pos // TOK_BLOCK) * TOK_BLOCK + _block_token(pos % TOK_BLOCK)
    rel0 = pos[None, :] - key_tok[:, None]
    rel = np.stack([np.maximum(rel0, 0), rel0 + DA_BLOCK]).astype(np.int32)
    max_exact = N_BUCKETS // 2
    nf = np.maximum(rel, 1).astype(np.float32)
    large = max_exact + (np.log(nf / np.float32(max_exact)) / np.float32(math.log(MAX_DISTANCE / max_exact))
                         * np.float32(N_BUCKETS - max_exact)).astype(np.int32)
    large = np.minimum(large, N_BUCKETS - 1)
    bucket = np.where(rel < max_exact, rel, large).astype(np.int32)
    bucket[0] = np.where(rel0 < 0, N_BUCKETS, bucket[0])
    return bucket


def _position_to_token():
    pos = np.arange(TOK_BLOCK)
    m = np.zeros((TOK_BLOCK, TOK_BLOCK), np.float32)
    m[pos, _block_token(pos)] = 1.0
    return m


def _strict_causal_tile():
    pos = np.arange(TOK_BLOCK)
    return (_block_token(pos)[:, None] < pos[None, :]).astype(np.float32)


def _value_column_scale(n):
    g = (np.arange(n) % TOK_BLOCK) // SUBLANES
    return (2.0 ** -(CHUNK - g)).astype(np.float32).reshape(1, n)


def _cparams(n_axes):
    return pltpu.CompilerParams(dimension_semantics=("arbitrary",) * n_axes, vmem_limit_bytes=VMEM_LIMIT)


def _mod_kernel(c_ref, w_ref, b_ref, lq1_ref, lk1_ref, lq2_ref, lk2_ref, mod_ref, lam_ref, *, lambda_init):
    c = c_ref[...]
    silu_c = c / (1.0 + jnp.exp(-c))
    mod_ref[...] = jnp.dot(silu_c, w_ref[...], preferred_element_type=F32,
                           precision=lax.Precision.HIGHEST) + b_ref[...]
    s1 = jnp.sum(lq1_ref[...] * lk1_ref[...], axis=-1, keepdims=True)
    s2 = jnp.sum(lq2_ref[...] * lk2_ref[...], axis=-1, keepdims=True)
    lam_ref[...] = jnp.broadcast_to(jnp.exp(s1) - jnp.exp(s2) + lambda_init, lam_ref.shape)


def _mod_bias_kernel(c_ref, w_ref, b_ref, lq1_ref, lk1_ref, lq2_ref, lk2_ref, rb_ref, *refs, lambda_init, n_pieces):
    bucket_refs, (mod_ref, lam_ref), piece_refs = refs[:n_pieces], refs[n_pieces:n_pieces + 2], refs[n_pieces + 2:]
    _mod_kernel(c_ref, w_ref, b_ref, lq1_ref, lk1_ref, lq2_ref, lk2_ref, mod_ref, lam_ref, lambda_init=lambda_init)
    _bias_kernel(rb_ref, *bucket_refs, *piece_refs)


def _modulation(c, w_ada, b_ada, lq1, lk1, lq2, lk2, lambda_init, rel_bias):
    B, D = c.shape
    n_out = w_ada.shape[1]
    col = n_out // DA_HEADS
    assert col * DA_HEADS == n_out and col % LANES == 0
    pieces = _bias_piece_tables()
    vec = lambda: pl.BlockSpec((1, DA_HEAD_DIM), lambda j: (0, 0))
    return pl.pallas_call(
        functools.partial(_mod_bias_kernel, lambda_init=lambda_init, n_pieces=len(pieces)),
        grid=(DA_HEADS,),
        in_specs=[pl.BlockSpec((B, D), lambda j: (0, 0)),
                  pl.BlockSpec((D, col), lambda j: (0, j)),
                  pl.BlockSpec((1, col), lambda j: (0, j)),
                  vec(), vec(), vec(), vec(),
                  pl.BlockSpec(memory_space=pltpu.SMEM)]
                 + [pl.BlockSpec(p.shape, lambda j: (0, 0)) for p in pieces],
        out_specs=[pl.BlockSpec((B, col), lambda j: (0, j)),
                   pl.BlockSpec((1, LANES), lambda j: (0, 0))]
                  + [pl.BlockSpec((1,) + p.shape, lambda j: (j, 0, 0)) for p in pieces],
        out_shape=[jax.ShapeDtypeStruct((B, n_out), F32), jax.ShapeDtypeStruct((1, LANES), F32)]
                  + [jax.ShapeDtypeStruct((DA_HEADS,) + p.shape, F32) for p in pieces],
        compiler_params=_cparams(1),
        name="adaln_mod_rel_bias",
    )(c, w_ada, b_ada.reshape(1, n_out), lq1.reshape(1, -1), lk1.reshape(1, -1), lq2.reshape(1, -1),
      lk2.reshape(1, -1), rel_bias, *[jnp.asarray(p) for p in pieces])


def _bias_kernel(rb_ref, *refs):
    h = pl.program_id(0)
    far = rb_ref[N_BUCKETS - 1, h]
    n = len(refs) // 2
    for bucket_ref, out_ref in zip(refs[:n], refs[n:]):
        bucket = bucket_ref[...]
        tile = jnp.zeros(bucket.shape, F32)
        for b in range(N_BUCKETS - 1):
            tile = jnp.where(bucket == b, (rb_ref[b, h] - far) * LOG2E, tile)
        out_ref[0] = jnp.where(bucket == N_BUCKETS, NEG_INF, tile)


def _bias_piece_tables():
    buckets = _bucket_tiles()
    near_rest = buckets[NEAR_TILE].copy()
    near_rest[TOK_BLOCK:, :LANES] = N_BUCKETS - 1
    assert (near_rest == N_BUCKETS - 1).all() and (buckets[DIAG_TILE][TOK_BLOCK:, :TOK_BLOCK] == N_BUCKETS).all()
    return [buckets[DIAG_TILE][:TOK_BLOCK, :], buckets[DIAG_TILE][TOK_BLOCK:, TOK_BLOCK:],
            buckets[NEAR_TILE][TOK_BLOCK:, :LANES]]


def _modulated_norm(x, g, scl, shift):
    ms = jnp.mean(x * x, axis=-1, keepdims=True)
    xn = x * lax.rsqrt(ms + EPS) * g
    return (xn * (1.0 + scl) + shift).astype(BF16)


def _proj_kernel(x_ref, g_ref, scl_ref, shift_ref, wt_ref, wf_ref, vscale_ref, perm_ref,
                 kda_ref, ksb_ref, qda_ref, vda_ref, qsb_ref, vsb_ref):
    h = _modulated_norm(x_ref[0], g_ref[...], scl_ref[0], shift_ref[0])
    hp = jnp.concatenate(
        [jnp.dot(perm_ref[...], h[j * TOK_BLOCK:(j + 1) * TOK_BLOCK], preferred_element_type=F32).astype(BF16)
         for j in range(h.shape[0] // TOK_BLOCK)], axis=0)
    width = kda_ref.shape[-1]
    for i, o_ref in enumerate((kda_ref, ksb_ref)):
        o_ref[0] = jnp.dot(hp, wt_ref[:, i * width:(i + 1) * width], preferred_element_type=F32).astype(BF16)
    for i, (o_ref, src) in enumerate(((qda_ref, h), (vda_ref, hp), (qsb_ref, h), (vsb_ref, hp))):
        r = lax.dot_general(wf_ref[i * width:(i + 1) * width, :], src, (((1,), (1,)), ((), ())),
                            preferred_element_type=F32)
        if o_ref is vsb_ref:
            r = r * vscale_ref[...]
        r = r.astype(BF16)
        for t in range(o_ref.shape[1]):
            o_ref[0, t] = r[:, t * TOK_BLOCK:(t + 1) * TOK_BLOCK]


def _project(x, norm_g, scl, shift, w_tok, w_feat_t):
    B, S, D = x.shape
    n_blk = S // TOK_BLOCK
    blk_per_tile = ROW_TILE // TOK_BLOCK
    width = DA_WIDTH
    tok_spec = pl.BlockSpec((1, ROW_TILE, width), lambda b, i: (b, i, 0))
    feat_spec = pl.BlockSpec((1, blk_per_tile, width, TOK_BLOCK), lambda b, i: (b, i, 0, 0))
    tok_shape = jax.ShapeDtypeStruct((B, S, width), BF16)
    feat_shape = jax.ShapeDtypeStruct((B, n_blk, width, TOK_BLOCK), BF16)
    mod_spec = pl.BlockSpec((1, 1, D), lambda b, i: (b, 0, 0))
    return pl.pallas_call(
        _proj_kernel,
        grid=(B, S // ROW_TILE),
        in_specs=[pl.BlockSpec((1, ROW_TILE, D), lambda b, i: (b, i, 0)),
                  pl.BlockSpec((1, D), lambda b, i: (0, 0)),
                  mod_spec, mod_spec,
                  pl.BlockSpec(w_tok.shape, lambda b, i: (0, 0)),
                  pl.BlockSpec(w_feat_t.shape, lambda b, i: (0, 0)),
                  pl.BlockSpec((1, ROW_TILE), lambda b, i: (0, 0)),
                  pl.BlockSpec((TOK_BLOCK, TOK_BLOCK), lambda b, i: (0, 0))],
        out_specs=[tok_spec] * 2 + [feat_spec] * 4,
        out_shape=[tok_shape] * 2 + [feat_shape] * 4,
        compiler_params=_cparams(2),
        name="norm_in_proj",
    )(x, norm_g.reshape(1, D), scl, shift, w_tok, w_feat_t, jnp.asarray(_value_column_scale(ROW_TILE)),
      jnp.asarray(_position_to_token(), BF16))


def _store_row_groups(q_t, qz_ref, group):
    row = lax.broadcasted_iota(jnp.int32, q_t.shape, 0)
    for i in range(qz_ref.shape[0]):
        inside = (row >= i * group) & (row < (i + 1) * group)
        qz_ref[i] = jnp.where(inside, q_t, jnp.zeros_like(q_t))


def _key_block(k_ref, kj, rows=TOK_BLOCK):
    return k_ref[0, pl.ds(pl.multiple_of(kj * rows, rows), rows), :]


def _da_tile_passes(n_q):
    plain = [(qi, kj) for qi in range(n_q) for kj in range(qi - 1)]
    near = [(qi, qi - 1) for qi in range(1, n_q)]
    diag = [(qi, qi) for qi in range(n_q)]
    passes, start = [], 0
    for tiles, kind in ((plain, None), (near, NEAR_TILE), (diag, DIAG_TILE)):
        passes.append((start, len(tiles), kind))
        start += len(tiles)
    order = np.asarray(plain + near + diag, np.int32)
    return order[:, 0], order[:, 1], tuple(passes)


def _da_kernel(lam_ref, qs_ref, ks_ref, q_ref, k_ref, v_ref, top_bias_ref, right_bias_ref, near_bias_ref, g_ref, o_ref,
               acc_ref, m_ref, l_ref, qz_ref, s_ref, smax_ref, p_ref, a_ref, *, out_scale, passes):
    n_q, n_map = acc_ref.shape[0], acc_ref.shape[1]
    n_sub = DA_BLOCK // TOK_BLOCK
    acc_ref[...] = jnp.zeros(acc_ref.shape, F32)
    m_ref[...] = jnp.full(m_ref.shape, NEG_INF, F32)
    l_ref[...] = jnp.zeros(l_ref.shape, F32)

    for qi in range(n_q):
        q_t = jnp.concatenate([q_ref[0, n_sub * qi + j] for j in range(n_sub)], axis=-1)
        _store_row_groups(q_t, qz_ref.at[qi], DA_HEAD_DIM)

    def scores(qi, kj, maps=None):
        kb = _key_block(k_ref, kj, DA_BLOCK)
        return [jnp.dot(kb, qz_ref[qi, mp], preferred_element_type=F32)
                for mp in (range(n_map) if maps is None else maps)]

    half, lane0 = TOK_BLOCK, LANES

    def put_scores(ss, kind):
        for mp in range(n_map):
            s = ss[mp]
            hd = mp // 2
            if kind == NEAR_TILE:
                corner = s[half:, :lane0] + near_bias_ref[hd]
                s = jnp.concatenate([s[:half], jnp.concatenate([corner, s[half:, lane0:]], axis=1)], axis=0)
                s_ref[mp] = s
                smax_ref[mp] = jnp.max(s, axis=0, keepdims=True)
            elif kind == DIAG_TILE:
                top = s[:half] + top_bias_ref[hd]
                right = s[half:, half:] + right_bias_ref[hd]
                s_ref[mp, :half, :] = top
                s_ref[mp, half:, half:] = right
                top_max = jnp.max(top, axis=0, keepdims=True)
                smax_ref[mp] = jnp.concatenate(
                    [top_max[:, :half], jnp.maximum(top_max[:, half:], jnp.max(right, axis=0, keepdims=True))], axis=1)
            else:
                s_ref[mp] = s
                smax_ref[mp] = jnp.max(s, axis=0, keepdims=True)

    def softmax_step(qi, slot, kind):
        for mp in range(n_map):
            m_old = m_ref[qi, mp]
            m_new = jnp.maximum(m_old, smax_ref[mp])
            alpha = jnp.exp2(m_old - m_new)
            if kind == DIAG_TILE:
                p_top = jnp.exp2(s_ref[mp, :half, :] - m_new)
                p_right = jnp.exp2(s_ref[mp, half:, half:] - m_new[:, half:])
                top_sum = jnp.sum(p_top, axis=0, keepdims=True)
                p_sum = jnp.concatenate(
                    [top_sum[:, :half], top_sum[:, half:] + jnp.sum(p_right, axis=0, keepdims=True)], axis=1)
                p_ref[slot, mp, :half, :] = p_top.astype(BF16)
                p_ref[slot, mp, half:, :half] = jnp.zeros((half, half), BF16)
                p_ref[slot, mp, half:, half:] = p_right.astype(BF16)
            else:
                p = jnp.exp2(s_ref[mp] - m_new)
                p_sum = jnp.sum(p, axis=0, keepdims=True)
                p_ref[slot, mp] = p.astype(BF16)
            l_ref[qi, mp] = alpha * l_ref[qi, mp] + p_sum
            m_ref[qi, mp] = m_new
            a_ref[mp] = alpha

    def weighted_values(kj, slot, maps=None):
        out = []
        for mp in (range(n_map) if maps is None else maps):
            pv = None
            for j in range(n_sub):
                vh = v_ref[0, n_sub * kj + j, (mp // 2) * DA_V_DIM:(mp // 2 + 1) * DA_V_DIM, :]
                part = jnp.dot(vh, p_ref[slot, mp, j * TOK_BLOCK:(j + 1) * TOK_BLOCK, :], preferred_element_type=F32)
                pv = part if pv is None else pv + part
            out.append(pv)
        return out

    def add_weighted(pv, qi):
        for mp in range(n_map):
            acc_ref[qi, mp] = a_ref[mp] * acc_ref[qi, mp] + pv[mp]

    for start, count, kind in passes:
        if count == 0:
            continue
        last = start + count - 1
        put_scores(scores(qs_ref[start], ks_ref[start]), kind)
        second = min(start + 1, last)
        next_scores = scores(qs_ref[second], ks_ref[second])
        softmax_step(qs_ref[start], start % 2, kind)
        put_scores(next_scores, kind)

        def one_step(cur, cur_slot, last=last, kind=kind):
            prev = cur - 1
            nxt = jnp.minimum(cur + 1, last)
            pv, next_scores = [], []
            for mp in range(n_map):
                next_scores += scores(qs_ref[nxt], ks_ref[nxt], (mp,))
                pv += weighted_values(ks_ref[prev], 1 - cur_slot, (mp,))
            add_weighted(pv, qs_ref[prev])
            softmax_step(qs_ref[cur], cur_slot, kind)
            put_scores(next_scores, kind)

        rest = count - 1
        first_slot = (start + 1) % 2

        def two_steps(j, carry, start=start, first_slot=first_slot, one_step=one_step):
            cur = start + 1 + 2 * j
            one_step(cur, first_slot)
            one_step(cur + 1, 1 - first_slot)
            return carry

        lax.fori_loop(0, rest // 2, two_steps, 0)
        if rest % 2:
            one_step(last, last % 2)
        add_weighted(weighted_values(ks_ref[last], last % 2), qs_ref[last])

    lam = lam_ref[0, 0]

    gain = jnp.concatenate([g_ref[...]] * (DA_BLOCK // LANES), axis=1) * out_scale

    def finish(qi, carry):
        for hd in range(n_map // 2):
            o_t = (acc_ref[qi, 2 * hd] / l_ref[qi, 2 * hd]
                   - lam * (acc_ref[qi, 2 * hd + 1] / l_ref[qi, 2 * hd + 1]))
            o_t = o_t * lax.rsqrt(jnp.mean(o_t * o_t, axis=0, keepdims=True) + EPS) * gain
            o_ref[0, hd * DA_V_DIM:(hd + 1) * DA_V_DIM,
                  pl.ds(pl.multiple_of(qi * DA_BLOCK, DA_BLOCK), DA_BLOCK)] = o_t.astype(o_ref.dtype)
        return carry

    lax.fori_loop(0, n_q, finish, 0)


DA_GROUP = 2


def _diff_attention(q_t, k, v_t, bias_pieces, lam, subln_g, lambda_init):
    B, S, _ = k.shape
    n_blk = S // TOK_BLOCK
    n_q = S // DA_BLOCK
    nhd = DA_GROUP
    n_map = 2 * nhd
    wide = nhd * DA_V_DIM
    qs, ks, passes = _da_tile_passes(n_q)
    smem = pl.BlockSpec(memory_space=pltpu.SMEM)
    return pl.pallas_call(
        functools.partial(_da_kernel, out_scale=1.0 - lambda_init, passes=passes),
        grid=(B, DA_HEADS // nhd),
        in_specs=[smem, smem, smem,
                  pl.BlockSpec((1, n_blk, wide, TOK_BLOCK), lambda b, h: (b, 0, h, 0)),
                  pl.BlockSpec((1, S, wide), lambda b, h: (b, 0, h)),
                  pl.BlockSpec((1, n_blk, wide, TOK_BLOCK), lambda b, h: (b, 0, h, 0)),
                  *[pl.BlockSpec((nhd,) + p.shape[1:], lambda b, h: (h, 0, 0)) for p in bias_pieces],
                  pl.BlockSpec((DA_V_DIM, LANES), lambda b, h: (0, 0))],
        out_specs=pl.BlockSpec((1, wide, S), lambda b, h: (b, h, 0)),
        out_shape=jax.ShapeDtypeStruct((B, DA_WIDTH, S), BF16),
        scratch_shapes=[pltpu.VMEM((n_q, n_map, DA_V_DIM, DA_BLOCK), F32),
                        pltpu.VMEM((n_q, n_map, 1, DA_BLOCK), F32),
                        pltpu.VMEM((n_q, n_map, 1, DA_BLOCK), F32),
                        pltpu.VMEM((n_q, n_map, wide, DA_BLOCK), BF16),
                        pltpu.VMEM((n_map, DA_BLOCK, DA_BLOCK), F32),
                        pltpu.VMEM((n_map, 1, DA_BLOCK), F32),
                        pltpu.VMEM((2, n_map, DA_BLOCK, DA_BLOCK), BF16),
                        pltpu.VMEM((n_map, 1, DA_BLOCK), F32)],
        compiler_params=_cparams(2),
        name="diff_attention",
    )(lam, jnp.asarray(qs), jnp.asarray(ks), q_t, k, v_t, *bias_pieces,
      jnp.broadcast_to(subln_g.reshape(DA_V_DIM, 1), (DA_V_DIM, LANES)))


def _sb_kernel(q_ref, k_ref, v_ref, mask_ref, o_ref, acc_ref, carry_ref, qz_ref, u_ref, tot_ref, w_ref):
    qi = pl.program_id(2)
    nh = acc_ref.shape[0]
    slab = qz_ref.shape[1]
    slab_heads = slab // SB_HEAD_DIM
    for sl in range(nh // slab_heads):
        _store_row_groups(q_ref[0, 0, sl * slab:(sl + 1) * slab, :], qz_ref.at[pl.ds(sl * slab_heads, slab_heads)],
                          SB_HEAD_DIM)
    carry_ref[...] = jnp.ones(carry_ref.shape, F32)
    sub = lax.broadcasted_iota(jnp.int32, (SUBLANES, TOK_BLOCK), 0)
    rows = lambda x, g: x[g * SUBLANES:(g + 1) * SUBLANES]

    def logits(kj):
        kb = _key_block(k_ref, kj)
        return [jnp.dot(kb[:, (hh // slab_heads) * slab:(hh // slab_heads + 1) * slab], qz_ref[hh],
                        preferred_element_type=F32) for hh in range(nh)]

    def gates(zs, slot, mask=None):
        for hh in range(nh):
            t = jnp.tanh(zs[hh])
            if mask is not None:
                t = jnp.where(mask > 0.5, t, -1.0)
            u = 1.0 - t
            u_ref[slot, hh] = u
            tot = rows(u, CHUNK - 1)
            for g in reversed(range(CHUNK - 1)):
                tot = tot * rows(u, g)
            tot_ref[slot, hh] = tot * 2.0 ** -CHUNK

    def weights(slot, keep_carry=None):
        for hh in range(nh):
            suffix = tot_ref[slot, hh]
            for d in (1, 2, 4):
                shifted = pltpu.roll(suffix, SUBLANES - d, axis=0)
                suffix = jnp.where(sub + d < SUBLANES, suffix * shifted, suffix)
            above = jnp.where(sub + 1 < SUBLANES, pltpu.roll(suffix, SUBLANES - 1, axis=0), 1.0)
            carry = carry_ref[hh]
            new_carry = carry * jnp.broadcast_to(suffix[0:1], carry.shape)
            carry_ref[hh] = new_carry if keep_carry is None else new_carry * keep_carry
            run = above * carry
            w = [None] * CHUNK
            for g in reversed(range(CHUNK)):
                ug = u_ref[slot, hh, g * SUBLANES:(g + 1) * SUBLANES, :]
                w[g] = (2.0 - ug) * run
                run = run * ug
            w_ref[slot, hh] = jnp.concatenate(w, axis=0).astype(BF16)

    def weighted_values(kj, slot):
        vb = v_ref[0, kj]
        return [jnp.dot(vb[hh * SB_HEAD_DIM:(hh + 1) * SB_HEAD_DIM], w_ref[slot, hh], preferred_element_type=F32)
                for hh in range(nh)]

    def alive():
        return (jnp.max(carry_ref[...]) > 0.0).astype(jnp.int32)

    before = jnp.maximum(qi - 1, 0)
    z_diag = logits(qi)
    z_before = logits(before)
    gates(z_diag, 0, mask_ref[...])
    gates(z_before, 1)
    weights(0, keep_carry=(qi > 0).astype(F32))
    weights(1)
    pv_diag = weighted_values(qi, 0)
    pv_before = weighted_values(before, 1)
    for hh in range(nh):
        acc_ref[hh] = pv_diag[hh] + pv_before[hh]

    more = jnp.logical_and(qi >= 2, alive() > 0)
    pl.when(more)(lambda: gates(logits(jnp.maximum(qi - 2, 0)), 0))

    def left_cond(state):
        i, live = state
        return jnp.logical_and(i < qi - 1, live > 0)

    def left_body(state):
        i, _ = state
        kj = qi - 2 - i
        nxt = logits(jnp.maximum(kj - 1, 0))
        weights(0)
        pv = weighted_values(kj, 0)
        for hh in range(nh):
            acc_ref[hh] = acc_ref[hh] + pv[hh]
        gates(nxt, 0)
        return i + 1, alive()

    lax.while_loop(left_cond, left_body, (jnp.int32(0), more.astype(jnp.int32)))

    for hh in range(nh):
        o_ref[0, hh * SB_HEAD_DIM:(hh + 1) * SB_HEAD_DIM, :] = acc_ref[hh].astype(o_ref.dtype)


SB_GROUP = 8


def _sb_attention(q_t, k, v_t):
    B, S, _ = k.shape
    n_blk = S // TOK_BLOCK
    nh = SB_GROUP
    pair = nh * SB_HEAD_DIM
    return pl.pallas_call(
        _sb_kernel,
        grid=(B, SB_HEADS // nh, n_blk),
        in_specs=[pl.BlockSpec((1, 1, pair, TOK_BLOCK), lambda b, h, i: (b, i, h, 0)),
                  pl.BlockSpec((1, S, pair), lambda b, h, i: (b, 0, h)),
                  pl.BlockSpec((1, n_blk, pair, TOK_BLOCK), lambda b, h, i: (b, 0, h, 0)),
                  pl.BlockSpec((TOK_BLOCK, TOK_BLOCK), lambda b, h, i: (0, 0))],
        out_specs=pl.BlockSpec((1, pair, TOK_BLOCK), lambda b, h, i: (b, h, i)),
        out_shape=jax.ShapeDtypeStruct((B, SB_WIDTH, S), BF16),
        scratch_shapes=[pltpu.VMEM((nh, SB_HEAD_DIM, TOK_BLOCK), F32),
                        pltpu.VMEM((nh, SUBLANES, TOK_BLOCK), F32),
                        pltpu.VMEM((nh, TOK_BLOCK, TOK_BLOCK), BF16),
                        pltpu.VMEM((2, nh, TOK_BLOCK, TOK_BLOCK), F32),
                        pltpu.VMEM((2, nh, SUBLANES, TOK_BLOCK), F32),
                        pltpu.VMEM((2, nh, TOK_BLOCK, TOK_BLOCK), BF16)],
        compiler_params=_cparams(3),
        name="stick_breaking_attention",
    )(q_t, k, v_t, jnp.asarray(_strict_causal_tile()))


def _out_kernel(oda_ref, osb_ref, x_ref, g_ref, scl_ref, shift_ref, gate_ref, wg_ref, w_ref, fg_ref, out_ref, *,
                final_norm):
    x = x_ref[0]
    h = _modulated_norm(x, g_ref[...], scl_ref[0], shift_ref[0])
    split = oda_ref.shape[1]

    def gated(o_ref, rows):
        g = lax.dot_general(wg_ref[:, rows], h, (((0,), (1,)), ((), ())), preferred_element_type=F32)
        return (o_ref[0].astype(F32) * (g / (1.0 + jnp.exp(-g)))).astype(BF16)

    contract_features = (((0,), (0,)), ((), ()))
    y = lax.dot_general(gated(oda_ref, slice(0, split)), w_ref[0:split, :], contract_features,
                        preferred_element_type=F32)
    y = y + lax.dot_general(gated(osb_ref, slice(split, 2 * split)), w_ref[split:, :], contract_features,
                            preferred_element_type=F32)
    xo = x + gate_ref[0] * y
    if final_norm:
        xo = xo * lax.rsqrt(jnp.mean(xo * xo, axis=-1, keepdims=True) + EPS) * fg_ref[...]
    out_ref[0] = xo


def _out_project(o_da, o_sb, x, norm_g, scl, shift, gate, w_gate, w_out, final_g, final_norm):
    B, S, D = x.shape
    half_spec = pl.BlockSpec((1, DA_WIDTH, OUT_ROW_TILE), lambda b, i: (b, 0, i))
    row_spec = pl.BlockSpec((1, OUT_ROW_TILE, D), lambda b, i: (b, i, 0))
    vec_spec = pl.BlockSpec((1, D), lambda b, i: (0, 0))
    mod_spec = pl.BlockSpec((1, 1, D), lambda b, i: (b, 0, 0))
    return pl.pallas_call(
        functools.partial(_out_kernel, final_norm=final_norm),
        grid=(B, S // OUT_ROW_TILE),
        in_specs=[half_spec, half_spec, row_spec, vec_spec, mod_spec, mod_spec, mod_spec,
                  pl.BlockSpec(w_gate.shape, lambda b, i: (0, 0)),
                  pl.BlockSpec(w_out.shape, lambda b, i: (0, 0)),
                  vec_spec],
        out_specs=row_spec,
        out_shape=jax.ShapeDtypeStruct((B, S, D), F32),
        compiler_params=_cparams(2),
        name="gate_out_proj",
    )(o_da, o_sb, x, norm_g.reshape(1, D), scl, shift, gate, w_gate, w_out, final_g.reshape(1, D))


def _split_w_in(w):
    sizes = (DA_WIDTH, DA_WIDTH, DA_WIDTH, DA_WIDTH, SB_WIDTH, SB_WIDTH, SB_WIDTH, SB_WIDTH)
    q_da, k_da, v_da, g_da, q_sb, k_sb, v_sb, g_sb = jnp.split(w, np.cumsum(sizes)[:-1].tolist(), axis=1)
    w_tok = jnp.concatenate([k_da, k_sb], axis=1).astype(BF16)
    w_gate = jnp.concatenate([g_da, g_sb], axis=1).astype(BF16)
    q_scale = 1.0 / math.sqrt(DA_HEAD_DIM)
    w_feat_t = jnp.concatenate([q_da * (LOG2E * q_scale), v_da, q_sb * (0.5 * q_scale), v_sb], axis=1)
    return w_tok, w_feat_t.T.astype(BF16), w_gate


def kernel(x, c, norm_g, w_ada, b_ada, w_in, lambda_q1, lambda_k1, lambda_q2, lambda_k2, subln_g, w_out,
           rel_bias, final_g):
    B, S, D = x.shape
    depth = w_in.shape[0]
    assert S % ROW_TILE == 0 and ROW_TILE % TOK_BLOCK == 0 and S % DA_BLOCK == 0
    assert S % OUT_ROW_TILE == 0 and OUT_ROW_TILE % TOK_BLOCK == 0
    assert DA_HEAD_DIM == SB_HEAD_DIM and DA_WIDTH == SB_WIDTH and DA_WIDTH + SB_WIDTH == w_out.shape[1]
    for l in range(depth):
        lambda_init = _lambda_init(l)
        mod, lam, *bias = _modulation(c, w_ada[l], b_ada[l], lambda_q1[l], lambda_k1[l], lambda_q2[l],
                                      lambda_k2[l], lambda_init, rel_bias)
        shift, scl, gate = (m.reshape(B, 1, D) for m in jnp.split(mod, 3, axis=-1))
        w_tok, w_feat_t, w_gate = _split_w_in(w_in[l])
        k_da, k_sb, q_da, v_da, q_sb, v_sb = _project(x, norm_g[l], scl, shift, w_tok, w_feat_t)
        o_da = _diff_attention(q_da, k_da, v_da, bias, lam[:, :1], subln_g[l], lambda_init)
        o_sb = _sb_attention(q_sb, k_sb, v_sb)
        x = _out_project(o_da, o_sb, x, norm_g[l], scl, shift, gate, w_gate, w_out[l].astype(BF16), final_g,
                         l == depth - 1)
    return x
```
